```python
import math
import jax
import jax.numpy as jnp
from jax import lax
import numpy as np

D_MODEL = 2048
BATCH = 2
SEQ = 4096
DEPTH = 2
DEC_BATCH = 128
DEC_SEQ = 1
PAST_LEN = 2048
PAGE_SIZE = 128

N_BRANCH = 3
SC_WIDTH = D_MODEL
SC_CONV = 3
SSM_D_INNER = D_MODEL
SSM_HEAD_DIM = 64
SSM_HEADS = SSM_D_INNER // SSM_HEAD_DIM
SSM_GROUPS = 8
SSM_HPG = SSM_HEADS // SSM_GROUPS
SSM_STATE = 128
SSM_CONV = 4
SSM_CHUNK = 128
SSM_CONV_DIM = SSM_D_INNER + 2 * SSM_GROUPS * SSM_STATE
ATT_HEAD_DIM = 128
ATT_GROUPS = ((128, 1), (512, 4), (2048, 16))
ATT_HPG = 4
ATT_HEADS = ATT_HPG * len(ATT_GROUPS)
ATT_WIDTH = ATT_HEADS * ATT_HEAD_DIM
ATT_OUT_WIDTH = ATT_HPG * ATT_HEAD_DIM
ATT_BLOCK = 128
ATT_SCALE = 1.0 / math.sqrt(ATT_HEAD_DIM)
N_BUCKETS = 32
MAX_DISTANCE = 2048
D_FF = 4 * D_MODEL
EPS = 1e-6
N_IN = N_BRANCH * D_MODEL + 3 * SC_WIDTH + SSM_D_INNER + SSM_CONV_DIM + SSM_HEADS + 3 * ATT_WIDTH

kernel_name = 'hybrid_gated_branch_decoder'


def _in_split_points():
    sizes = [N_BRANCH * D_MODEL, SC_WIDTH, SC_WIDTH, SC_WIDTH, SSM_D_INNER, SSM_CONV_DIM, SSM_HEADS,
             ATT_WIDTH, ATT_WIDTH, ATT_WIDTH]
    pts = []
    acc = 0
    for s in sizes[:-1]:
        acc += s
        pts.append(acc)
    return pts


def rmsnorm(x, g):
    xf = x.astype(jnp.float32)
    xf = xf * lax.rsqrt(jnp.mean(xf * xf, axis=-1, keepdims=True) + EPS)
    return (xf * g.astype(jnp.float32)).astype(x.dtype)


def causal_dwconv(u, buf, w):
    width = w.shape[0]
    L = u.shape[1]
    up = jnp.concatenate([buf.astype(u.dtype), u], axis=1)
    y = up[:, 0:L] * w[0]
    for i in range(1, width):
        y = y + up[:, i:i + L] * w[i]
    return y, up[:, L:]


def t5_bucket(dist):
    d = jnp.asarray(dist, jnp.int32)
    max_exact = N_BUCKETS // 2
    df = jnp.maximum(d, 1).astype(jnp.float32)
    large = max_exact + (jnp.log(df / max_exact) / math.log(MAX_DISTANCE / max_exact)
                         * (N_BUCKETS - max_exact)).astype(jnp.int32)
    large = jnp.minimum(large, N_BUCKETS - 1)
    return jnp.where(d < max_exact, d, large)


def dilated_group_prompt(q, k, v, bias_tab, window, dil):
    B, S, H, hd = q.shape
    nk = window // dil
    L = S // dil
    nb = -(-L // ATT_BLOCK)
    Lp = nb * ATT_BLOCK

    def to_stream(t):
        t = t.reshape(B, L, dil, H, hd).transpose(0, 2, 1, 3, 4)
        t = jnp.pad(t, ((0, 0), (0, 0), (0, Lp - L), (0, 0), (0, 0)))
        return t.reshape(B, dil, nb, ATT_BLOCK, H, hd)

    def with_prev(t):
        prev = jnp.pad(t, ((0, 0), (0, 0), (1, 0), (0, 0), (0, 0), (0, 0)))[:, :, :nb]
        return jnp.concatenate([prev, t], axis=3)

    qs = to_stream(q).astype(jnp.float32)
    kb = with_prev(to_stream(k)).astype(jnp.float32)
    vb = with_prev(to_stream(v)).astype(jnp.float32)
    qi = np.arange(ATT_BLOCK)[:, None]
    kj = np.arange(2 * ATT_BLOCK)[None, :]
    sdist = qi + ATT_BLOCK - kj
    band = (sdist >= 0) & (sdist <= nk)
    valid = band[None] & ((np.arange(nb)[:, None, None] > 0) | (kj[None] >= ATT_BLOCK))
    bias = bias_tab[t5_bucket(np.clip(sdist, 0, nk) * dil)].astype(jnp.float32).transpose(2, 0, 1)
    s = jnp.einsum('brnqhd,brnjhd->brnhqj', qs, kb) * ATT_SCALE + bias
    s = jnp.where(valid[None, None, :, None], s, -jnp.inf)
    m = jnp.max(s, axis=-1)
    p = jnp.exp(s - m[..., None])
    l = jnp.sum(p, axis=-1)
    o = jnp.einsum('brnhqj,brnjhd->brnqhd', p, vb)

    def back_vec(t):
        t = t.transpose(0, 1, 2, 4, 3).reshape(B, dil, Lp, H)[:, :, :L]
        return t.transpose(0, 2, 1, 3).reshape(B, S, H)

    o = o.reshape(B, dil, Lp, H, hd)[:, :, :L].transpose(0, 2, 1, 3, 4).reshape(B, S, H, hd)
    return o, back_vec(m), back_vec(l)


def dilated_group_sample(q, k_new, v_new, buf, bias_tab, window, dil):
    T = q.shape[1]
    Wb = buf.shape[1]
    nk = window // dil
    steps = np.arange(nk + 1)
    idx = Wb + np.arange(T)[:, None] - steps[None, :] * dil
    valid = idx >= 0
    kv_new = jnp.stack([k_new, v_new], axis=2)
    past = buf[:, np.clip(idx, 0, Wb - 1)].astype(kv_new.dtype)
    cur = kv_new[:, np.clip(idx - Wb, 0, T - 1)]
    kvg = jnp.where((idx >= Wb)[None, :, :, None, None, None], cur, past).astype(jnp.float32)
    bias = bias_tab[t5_bucket(steps * dil)].astype(jnp.float32).T
    s = jnp.einsum('bthd,btnhd->bthn', q.astype(jnp.float32), kvg[:, :, :, 0]) * ATT_SCALE + bias
    s = jnp.where(valid[None, :, None, :], s, -jnp.inf)
    m = jnp.max(s, axis=-1)
    p = jnp.exp(s - m[..., None])
    l = jnp.sum(p, axis=-1)
    o = jnp.einsum('bthn,btnhd->bthd', p, kvg[:, :, :, 1])
    return (o, m, l), kv_new


def merge_groups(outs):
    ms = jnp.stack([t[1] for t in outs])
    m_all = jnp.max(ms, axis=0)
    w = jnp.exp(ms - m_all)
    num = w[0][..., None] * outs[0][0]
    den = w[0] * outs[0][2]
    for g in range(1, len(outs)):
        num = num + w[g][..., None] * outs[g][0]
        den = den + w[g] * outs[g][2]
    return num / den[..., None]


def ssd_scan(x, dt, A, Bm, Cm, h0, chunk):
    b, L = x.shape[:2]
    nc = L // chunk

    def c(t):
        return t.reshape((b, nc, chunk) + t.shape[2:])

    x, dt, Bm, Cm = c(x), c(dt), c(Bm), c(Cm)
    acum = jnp.cumsum(dt * A, axis=2)
    diff = acum[:, :, :, None] - acum[:, :, None, :]
    causal = np.tril(np.ones((chunk, chunk), bool))[None, None, :, :, None, None]
    decay = jnp.exp(jnp.where(causal, diff, -jnp.inf))
    cb = jnp.einsum('bcign,bcjgn->bcijg', Cm, Bm)
    wgt = cb[..., None] * decay * dt[:, :, None]
    y_diag = jnp.einsum('bcijge,bcjgep->bcigep', wgt, x)
    decay_end = jnp.exp(acum[:, :, -1:] - acum)
    st = jnp.einsum('bcjgn,bcjge,bcjgep->bcgepn', Bm, decay_end * dt, x)
    chunk_decay = jnp.exp(acum[:, :, -1])

    def step(h, inp):
        s_c, d_c = inp
        return h * d_c[..., None, None] + s_c, h

    hT, h_prev = lax.scan(step, h0, (jnp.moveaxis(st, 1, 0), jnp.moveaxis(chunk_decay, 1, 0)))
    h_prev = jnp.moveaxis(h_prev, 0, 1)
    y_off = jnp.einsum('bcign,bcgepn,bcige->bcigep', Cm, h_prev, jnp.exp(acum))
    y = (y_diag + y_off).reshape((b, L) + x.shape[3:])
    return y, hT


def mamba_mixer(z, xbc, dt_raw, conv_buf, h0, conv_w, conv_b, dt_bias, A_log, D_skip, norm_g):
    b, L = z.shape[:2]
    xbc, new_buf = causal_dwconv(xbc, conv_buf, conv_w)
    xbc = jax.nn.silu(xbc + conv_b)
    xs = xbc[..., :SSM_D_INNER].reshape(b, L, SSM_GROUPS, SSM_HPG, SSM_HEAD_DIM).astype(jnp.float32)
    Bm = xbc[..., SSM_D_INNER:SSM_D_INNER + SSM_GROUPS * SSM_STATE].reshape(b, L, SSM_GROUPS, SSM_STATE).astype(jnp.float32)
    Cm = xbc[..., SSM_D_INNER + SSM_GROUPS * SSM_STATE:].reshape(b, L, SSM_GROUPS, SSM_STATE).astype(jnp.float32)
    dt = jax.nn.softplus(dt_raw.astype(jnp.float32) + dt_bias.astype(jnp.float32)).reshape(b, L, SSM_GROUPS, SSM_HPG)
    A = -jnp.exp(A_log.astype(jnp.float32)).reshape(SSM_GROUPS, SSM_HPG)
    h0 = h0.astype(jnp.float32).reshape(b, SSM_GROUPS, SSM_HPG, SSM_HEAD_DIM, SSM_STATE)
    y, hT = ssd_scan(xs, dt, A, Bm, Cm, h0, math.gcd(L, SSM_CHUNK))
    y = y + D_skip.astype(jnp.float32).reshape(SSM_GROUPS, SSM_HPG)[:, :, None] * xs
    y = y.reshape(b, L, SSM_D_INNER) * jax.nn.silu(z.astype(jnp.float32))
    yg = y.reshape(b, L, SSM_GROUPS, SSM_D_INNER // SSM_GROUPS)
    yg = yg * lax.rsqrt(jnp.mean(yg * yg, axis=-1, keepdims=True) + EPS)
    y = yg.reshape(b, L, SSM_D_INNER) * norm_g.astype(jnp.float32)
    return y.astype(z.dtype), new_buf, hT.reshape(b, SSM_HEADS, SSM_HEAD_DIM, SSM_STATE)


def setup_inputs(seed: int = 0) -> dict:
    key = jax.random.key(seed)
    ks = jax.random.split(key, 32)
    f32 = jnp.float32

    def nrm(k, shape, scale):
        return jax.random.normal(k, shape, f32) * scale

    def gain(k, shape):
        return 1.0 + 0.02 * jax.random.normal(k, shape, f32)

    kv_shape = lambda w: (DEPTH, DEC_BATCH, min(w, PAST_LEN), 2, ATT_HPG, ATT_HEAD_DIM)
    dt = jnp.exp(jax.random.uniform(ks[13], (DEPTH, SSM_HEADS), f32, math.log(1e-3), math.log(1e-1)))
    return {
        'x_prompt': nrm(ks[0], (BATCH, SEQ, D_MODEL), 1.0),
        'x_sample': nrm(ks[1], (DEC_BATCH, DEC_SEQ, D_MODEL), 1.0),
        'state_sc_conv': nrm(ks[2], (DEPTH, DEC_BATCH, SC_CONV - 1, SC_WIDTH), 1.0),
        'state_ssm_conv': nrm(ks[3], (DEPTH, DEC_BATCH, SSM_CONV - 1, SSM_CONV_DIM), 1.0),
        'state_ssm': nrm(ks[4], (DEPTH, DEC_BATCH, SSM_HEADS, SSM_HEAD_DIM, SSM_STATE), 0.1),
        'cache_kv_w128': nrm(ks[5], kv_shape(ATT_GROUPS[0][0]), 1.0),
        'cache_kv_w512': nrm(ks[6], kv_shape(ATT_GROUPS[1][0]), 1.0),
        'cache_kv_w2048': nrm(ks[7], kv_shape(ATT_GROUPS[2][0]), 1.0),
        'norm1_g': gain(ks[8], (DEPTH, D_MODEL)),
        'w_in': nrm(ks[9], (DEPTH, D_MODEL, N_IN), D_MODEL ** -0.5),
        'sc_conv_w': nrm(ks[10], (DEPTH, SC_CONV, SC_WIDTH), SC_CONV ** -0.5),
        'ssm_conv_w': nrm(ks[11], (DEPTH, SSM_CONV, SSM_CONV_DIM), SSM_CONV ** -0.5),
        'ssm_conv_b': nrm(ks[12], (DEPTH, SSM_CONV_DIM), 0.02),
        'ssm_dt_bias': dt + jnp.log(-jnp.expm1(-dt)),
        'ssm_A_log': jnp.log(jax.random.uniform(ks[14], (DEPTH, SSM_HEADS), f32, 1.0, 16.0)),
        'ssm_D': 1.0 + 0.1 * jax.random.normal(ks[15], (DEPTH, SSM_HEADS), f32),
        'ssm_norm_g': gain(ks[16], (DEPTH, SSM_D_INNER)),
        'q_norm_g': gain(ks[17], (DEPTH, ATT_HEAD_DIM)),
        'k_norm_g': gain(ks[18], (DEPTH, ATT_HEAD_DIM)),
        'rel_bias': nrm(ks[19], (N_BUCKETS, ATT_HEADS), 0.5),
        'w_br_sc': nrm(ks[20], (DEPTH, SC_WIDTH, D_MODEL), SC_WIDTH ** -0.5),
        'w_br_ssm': nrm(ks[21], (DEPTH, SSM_D_INNER, D_MODEL), SSM_D_INNER ** -0.5),
        'w_br_att': nrm(ks[22], (DEPTH, ATT_OUT_WIDTH, D_MODEL), ATT_OUT_WIDTH ** -0.5),
        'w_out': nrm(ks[23], (DEPTH, D_MODEL, D_MODEL), D_MODEL ** -0.5),
        'norm2_g': gain(ks[24], (DEPTH, D_MODEL)),
        'w_up': nrm(ks[25], (DEPTH, D_MODEL, D_FF), D_MODEL ** -0.5),
        'w_down': nrm(ks[26], (DEPTH, D_FF, D_MODEL), D_FF ** -0.5),
    }


def reference(x_prompt, x_sample, state_sc_conv, state_ssm_conv, state_ssm, cache_kv_w128, cache_kv_w512,
              cache_kv_w2048, norm1_g, w_in, sc_conv_w, ssm_conv_w, ssm_conv_b, ssm_dt_bias, ssm_A_log, ssm_D,
              ssm_norm_g, q_norm_g, k_norm_g, rel_bias, w_br_sc, w_br_ssm, w_br_att, w_out, norm2_g, w_up, w_down):
    split_pts = _in_split_points()

    def block(x, l, sc_buf, ssm_buf, ssm_h0, kv_bufs):
        b, L, _ = x.shape
        h = rmsnorm(x, norm1_g[l])
        proj = h @ w_in[l]
        gate_r, sc_b, sc_c, sc_x, z, xbc, dt_raw, q, k, v = jnp.split(proj, split_pts, axis=-1)
        gates = jax.nn.sigmoid(gate_r).reshape(b, L, N_BRANCH, D_MODEL)
        conv_u, sc_new = causal_dwconv(sc_c * sc_x, sc_buf, sc_conv_w[l])
        y_sc = sc_b * conv_u
        y_ssm, ssm_conv_new, ssm_hT = mamba_mixer(z, xbc, dt_raw, ssm_buf, ssm_h0, ssm_conv_w[l], ssm_conv_b[l],
                                                  ssm_dt_bias[l], ssm_A_log[l], ssm_D[l], ssm_norm_g[l])
        q = rmsnorm(q.reshape(b, L, ATT_HEADS, ATT_HEAD_DIM), q_norm_g[l])
        k = rmsnorm(k.reshape(b, L, ATT_HEADS, ATT_HEAD_DIM), k_norm_g[l])
        v = v.reshape(b, L, ATT_HEADS, ATT_HEAD_DIM)
        outs = []
        kv_new = []
        for gi, (window, dil) in enumerate(ATT_GROUPS):
            sl = slice(gi * ATT_HPG, (gi + 1) * ATT_HPG)
            if kv_bufs is None:
                outs.append(dilated_group_prompt(q[:, :, sl], k[:, :, sl], v[:, :, sl], rel_bias[:, sl], window, dil))
                keep = min(window, L)
                kv_new.append(jnp.stack([k[:, L - keep:, sl], v[:, L - keep:, sl]], axis=2))
            else:
                res, rows = dilated_group_sample(q[:, :, sl], k[:, :, sl], v[:, :, sl], kv_bufs[gi],
                                                 rel_bias[:, sl], window, dil)
                outs.append(res)
                kv_new.append(rows)
        y_att = merge_groups(outs).reshape(b, L, ATT_OUT_WIDTH).astype(x.dtype)
        merged = (gates[:, :, 0] * (y_sc @ w_br_sc[l]) + gates[:, :, 1] * (y_ssm @ w_br_ssm[l])
                  + gates[:, :, 2] * (y_att @ w_br_att[l]))
        x = x + merged @ w_out[l]
        h2 = rmsnorm(x, norm2_g[l])
        x = x + jnp.square(jax.nn.relu(h2 @ w_up[l])) @ w_down[l]
        return x, (sc_new, ssm_conv_new, ssm_hT, kv_new[0], kv_new[1], kv_new[2])

    xp = x_prompt
    xs = x_sample
    p_new = []
    s_new = []
    for l in range(DEPTH):
        zeros_sc = jnp.zeros((BATCH, SC_CONV - 1, SC_WIDTH), xp.dtype)
        zeros_conv = jnp.zeros((BATCH, SSM_CONV - 1, SSM_CONV_DIM), xp.dtype)
        zeros_h = jnp.zeros((BATCH, SSM_HEADS, SSM_HEAD_DIM, SSM_STATE), jnp.float32)
        xp, st = block(xp, l, zeros_sc, zeros_conv, zeros_h, None)
        p_new.append(st)
        xs, st = block(xs, l, state_sc_conv[l], state_ssm_conv[l], state_ssm[l],
                       (cache_kv_w128[l], cache_kv_w512[l], cache_kv_w2048[l]))
        s_new.append(st)
    p_sc, p_ssm_conv, p_ssm, p_kv128, p_kv512, p_kv2048 = [jnp.stack(a) for a in zip(*p_new)]
    s_sc, s_ssm_conv, s_ssm, s_kv128, s_kv512, s_kv2048 = [jnp.stack(a) for a in zip(*s_new)]
    return (xp, xs, p_sc, p_ssm_conv, p_ssm, p_kv128, p_kv512, p_kv2048,
            s_sc, s_ssm_conv, s_ssm, s_kv128, s_kv512, s_kv2048)
```

```python
import functools
import math

import numpy as np
import jax
import jax.numpy as jnp
from jax import lax
from jax.experimental import pallas as pl
from jax.experimental.pallas import tpu as pltpu

F32 = jnp.float32
BF16 = jnp.bfloat16

N_BRANCH = 3
SC_CONV = 3
SSM_HEAD_DIM = 64
SSM_GROUPS = 8
SSM_STATE = 128
SSM_CONV = 4
SSM_CHUNK = 128
ATT_HEAD_DIM = 128
ATT_GROUPS = ((128, 1), (512, 4), (2048, 16))
ATT_HPG = 4
ATT_BLOCK = 128
ATT_SCALE = 1.0 / math.sqrt(ATT_HEAD_DIM)
N_BUCKETS = 32
MAX_DISTANCE = 2048
EPS = 1e-6

LANES = 128
SUBLANES = 8
VMEM_LIMIT_BYTES = 56 * 1024 * 1024

DEC_TILE = 8
NEG_INF = float("-inf")


def _cparams(*sem):
    return pltpu.CompilerParams(dimension_semantics=sem, vmem_limit_bytes=VMEM_LIMIT_BYTES)


def _sigmoid(v):
    return 1.0 / (1.0 + jnp.exp(-v))


def _silu(v):
    return v * _sigmoid(v)


def _bdot(a, b):
    return jnp.dot(a.astype(BF16), b.astype(BF16), preferred_element_type=F32)


def _bdot_nt(a, b):
    return lax.dot_general(a.astype(BF16), b.astype(BF16), (((1,), (1,)), ((), ())),
                           preferred_element_type=F32)


def _rmsnorm_kernel(x_ref, g_ref, o_ref):
    x = x_ref[...]
    ms = jnp.mean(x * x, axis=-1, keepdims=True)
    o_ref[...] = (x * lax.rsqrt(ms + EPS) * g_ref[...]).astype(o_ref.dtype)


def _rmsnorm(x2d, g, tm):
    m, d = x2d.shape
    return pl.pallas_call(
        _rmsnorm_kernel,
        grid=(m // tm,),
        in_specs=[pl.BlockSpec((tm, d), lambda i: (i, 0)),
                  pl.BlockSpec((1, d), lambda i: (0, 0))],
        out_specs=pl.BlockSpec((tm, d), lambda i: (i, 0)),
        out_shape=jax.ShapeDtypeStruct((m, d), BF16),
        compiler_params=_cparams("parallel"),
        name="rmsnorm",
    )(x2d, g.reshape(1, d))


def _mm_kernel(a_ref, w_ref, o_ref):
    o_ref[...] = jnp.dot(a_ref[...], w_ref[...], preferred_element_type=F32).astype(o_ref.dtype)


def _matmul(a, w, tm, tn, out_dtype=F32, name="matmul"):
    m, k = a.shape
    n = w.shape[1]
    return pl.pallas_call(
        _mm_kernel,
        grid=(m // tm, n // tn),
        in_specs=[pl.BlockSpec((tm, k), lambda i, j: (i, 0)),
                  pl.BlockSpec((k, tn), lambda i, j: (0, j))],
        out_specs=pl.BlockSpec((tm, tn), lambda i, j: (i, j)),
        out_shape=jax.ShapeDtypeStruct((m, n), out_dtype),
        compiler_params=_cparams("parallel", "arbitrary"),
        name=name,
    )(a, w)


def _mm_qkv_kernel(a_ref, w_ref, g_ref, o_ref, *, n_norm_blocks, heads_per_block):
    acc = jnp.dot(a_ref[...], w_ref[...], preferred_element_type=F32)
    j = pl.program_id(1)

    @pl.when(j < n_norm_blocks)
    def _():
        for h in range(heads_per_block):
            sl = slice(h * ATT_HEAD_DIM, (h + 1) * ATT_HEAD_DIM)
            t = acc[:, sl]
            ms = jnp.mean(t * t, axis=-1, keepdims=True)
            o_ref[:, sl] = t * lax.rsqrt(ms + EPS) * g_ref[:, sl]

    @pl.when(j >= n_norm_blocks)
    def _():
        o_ref[...] = acc


def _matmul_qkv(a, w, gains, tm, tn):
    m, k = a.shape
    n = w.shape[1]
    att_width = n // 3
    return pl.pallas_call(
        functools.partial(_mm_qkv_kernel, n_norm_blocks=2 * att_width // tn,
                          heads_per_block=tn // ATT_HEAD_DIM),
        grid=(m // tm, n // tn),
        in_specs=[pl.BlockSpec((tm, k), lambda i, j: (i, 0)),
                  pl.BlockSpec((k, tn), lambda i, j: (0, j)),
                  pl.BlockSpec((1, tn), lambda i, j: (0, j))],
        out_specs=pl.BlockSpec((tm, tn), lambda i, j: (i, j)),
        out_shape=jax.ShapeDtypeStruct((m, n), F32),
        compiler_params=_cparams("parallel", "arbitrary"),
        name="matmul_qkv",
    )(a, w, gains)


def _sc_prompt_kernel(b_ref, c_ref, x_ref, hc_ref, hx_ref, w_ref, y_ref, tail_ref, ext_ref, *, ts):
    t = pl.program_id(2)
    u = c_ref[...] * x_ref[...]
    hu = hc_ref[...] * hx_ref[...]
    ext_ref[0:SUBLANES, :] = jnp.where(t > 0, hu, 0.0)
    ext_ref[SUBLANES:SUBLANES + ts, :] = u
    u2 = ext_ref[pl.ds(SUBLANES - 2, ts), :]
    u1 = ext_ref[pl.ds(SUBLANES - 1, ts), :]
    y = u2 * w_ref[0:1, :] + u1 * w_ref[1:2, :] + u * w_ref[2:3, :]
    y_ref[...] = (b_ref[...] * y).astype(y_ref.dtype)
    tail_ref[...] = ext_ref[pl.ds(ts, SUBLANES), :]


def _sc_prompt(p3, conv_w, d, ts, tc):
    bsz, s, _ = p3.shape
    ncb = d // tc
    off_b, off_c, off_x = 3 * ncb, 4 * ncb, 5 * ncb
    hb = ts // SUBLANES

    def halo(off):
        return pl.BlockSpec((None, SUBLANES, tc),
                            lambda b, c, t: (b, jnp.maximum(t * hb - 1, 0), off + c))

    def main(off):
        return pl.BlockSpec((None, ts, tc), lambda b, c, t: (b, t, off + c))

    y, tail = pl.pallas_call(
        functools.partial(_sc_prompt_kernel, ts=ts),
        grid=(bsz, ncb, s // ts),
        in_specs=[main(off_b), main(off_c), main(off_x), halo(off_c), halo(off_x),
                  pl.BlockSpec((SC_CONV, tc), lambda b, c, t: (0, c))],
        out_specs=[pl.BlockSpec((None, ts, tc), lambda b, c, t: (b, t, c)),
                   pl.BlockSpec((None, SUBLANES, tc), lambda b, c, t: (b, 0, c))],
        out_shape=[jax.ShapeDtypeStruct((bsz, s, d), BF16),
                   jax.ShapeDtypeStruct((bsz, SUBLANES, d), F32)],
        scratch_shapes=[pltpu.VMEM((ts + SUBLANES, tc), F32)],
        compiler_params=_cparams("parallel", "parallel", "arbitrary"),
        name="short_conv_prompt",
    )(p3, p3, p3, p3, p3, conv_w)
    return y, tail[:, SUBLANES - (SC_CONV - 1):, :]


def _sc_sample_kernel(b_ref, c_ref, x_ref, s0_ref, s1_ref, w_ref, y_ref, u_ref):
    u = c_ref[...] * x_ref[...]
    y = s0_ref[...] * w_ref[0:1, :] + s1_ref[...] * w_ref[1:2, :] + u * w_ref[2:3, :]
    y_ref[...] = b_ref[...] * y
    u_ref[...] = u


def _sc_sample(p2, state2d, conv_w, d, tc):
    db = p2.shape[0]
    ncb = d // tc

    def col(off):
        return pl.BlockSpec((db, tc), lambda c: (0, off + c))

    return pl.pallas_call(
        _sc_sample_kernel,
        grid=(ncb,),
        in_specs=[col(3 * ncb), col(4 * ncb), col(5 * ncb), col(0), col(ncb),
                  pl.BlockSpec((SC_CONV, tc), lambda c: (0, c))],
        out_specs=[col(0), col(0)],
        out_shape=[jax.ShapeDtypeStruct((db, d), F32), jax.ShapeDtypeStruct((db, d), F32)],
        compiler_params=_cparams("parallel"),
        name="short_conv_sample",
    )(p2, p2, p2, state2d, state2d, conv_w)


def _softplus(v):
    return jnp.maximum(v, 0.0) + jnp.log1p(jnp.exp(-jnp.abs(v)))


def _gated_group_norm(y, xs, z, dexp, ng):
    val = (y + dexp * xs) * _silu(z)
    ms = jnp.mean(val * val, axis=-1, keepdims=True)
    return val * lax.rsqrt(ms + EPS) * ng


def _ssd_prompt_kernel(x_ref, bc_ref, z_ref, dtr_ref, cwx_ref, cwbc_ref, cbx_ref, cbbc_ref, dtb_ref,
                       alog_ref, dexp_ref, ng_ref, y_ref, ht_ref, extx, extbc, xs_s, bc_s, state):
    q = SSM_CHUNK
    c = pl.program_id(1)
    pad = SUBLANES
    taps = SSM_CONV

    @pl.when(c == 0)
    def _():
        extx[0:pad, :] = jnp.zeros((pad, extx.shape[1]), F32)
        extbc[0:pad, :] = jnp.zeros((pad, extbc.shape[1]), F32)
        state[...] = jnp.zeros(state.shape, F32)

    extx[pad:pad + q, :] = x_ref[...]
    extbc[pad:pad + q, :] = bc_ref[...]

    def conv(ext, w_ref, b_ref):
        acc = ext[pl.ds(pad - taps + 1, q), :] * w_ref[0:1, :]
        for i in range(1, taps):
            acc = acc + ext[pl.ds(pad - taps + 1 + i, q), :] * w_ref[i:i + 1, :]
        return _silu(acc + b_ref[...])

    xs_s[...] = conv(extx, cwx_ref, cbx_ref)
    bc_s[...] = conv(extbc, cwbc_ref, cbbc_ref)
    extx[0:pad, :] = extx[q:q + pad, :]
    extbc[0:pad, :] = extbc[q:q + pad, :]

    dt = _softplus(dtr_ref[...] + dtb_ref[...])
    a_neg = -jnp.exp(alog_ref[...])
    ii = lax.broadcasted_iota(jnp.int32, (q, q), 0)
    jj = lax.broadcasted_iota(jnp.int32, (q, q), 1)
    causal = ii >= jj
    acum = jnp.dot(causal.astype(F32), dt * a_neg, precision=lax.Precision.HIGHEST,
                   preferred_element_type=F32)
    acum_t = acum.T
    dt_t = dt.T
    lo_half = jj < SSM_HEAD_DIM
    gs = SSM_STATE
    hd = SSM_HEAD_DIM
    hpg = xs_s.shape[1] // (SSM_GROUPS * hd)
    gw = hpg * hd
    n_bc = SSM_GROUPS * gs

    for g in range(SSM_GROUPS):
        bg = bc_s[:, g * gs:(g + 1) * gs].astype(BF16)
        cg = bc_s[:, n_bc + g * gs:n_bc + (g + 1) * gs].astype(BF16)
        cb = _bdot_nt(cg, bg)
        ys = []
        for hp in range(hpg // 2):
            e0 = g * hpg + 2 * hp
            rows = slice(e0 * hd, (e0 + 2) * hd)
            xp = xs_s[:, rows]
            xp16 = xp.astype(BF16)
            cols_a, cols_dt, parts = [], [], []
            for k in range(2):
                e = e0 + k
                col_a = jnp.broadcast_to(acum[:, e:e + 1], (q, q))
                cols_a.append(col_a)
                cols_dt.append(jnp.broadcast_to(dt[:, e:e + 1], (q, q)))
                dec = jnp.exp(jnp.where(causal, col_a - acum_t[e:e + 1, :], NEG_INF))
                wm = cb * dec * dt_t[e:e + 1, :]
                parts.append(jnp.dot(wm.astype(BF16), xp16, preferred_element_type=F32))
            y_diag = jnp.where(lo_half, parts[0], parts[1])
            sel_a = jnp.where(lo_half, cols_a[0], cols_a[1])
            sel_dt = jnp.where(lo_half, cols_dt[0], cols_dt[1])
            hpair = state[rows, :]
            y_off = _bdot_nt(cg, hpair) * jnp.exp(sel_a)
            ys.append(y_diag + y_off)
            last_a = sel_a[q - 1:q, :]
            xw = xp * jnp.exp(last_a - sel_a) * sel_dt
            st = jnp.dot(xw.T.astype(BF16), bg, preferred_element_type=F32)
            cd = jnp.concatenate([jnp.broadcast_to(cols_a[0][q - 1:q, :], (hd, gs)),
                                  jnp.broadcast_to(cols_a[1][q - 1:q, :], (hd, gs))], axis=0)
            state[rows, :] = hpair * jnp.exp(cd) + st
        gsl = slice(g * gw, (g + 1) * gw)
        yg = jnp.concatenate(ys, axis=1)
        y_ref[:, gsl] = _gated_group_norm(yg, xs_s[:, gsl], z_ref[:, gsl], dexp_ref[:, gsl],
                                          ng_ref[:, gsl]).astype(y_ref.dtype)

    @pl.when(c == pl.num_programs(1) - 1)
    def _():
        ht_ref[...] = state[...]


def _ssd_prompt(p3, dtr3, cw, cb, dtb, alog, dexp, ng, d):
    bsz, s, _ = p3.shape
    q = SSM_CHUNK
    heads = d // SSM_HEAD_DIM
    const = lambda shape: pl.BlockSpec(shape, lambda b, c: (0, 0))
    y, ht = pl.pallas_call(
        _ssd_prompt_kernel,
        grid=(bsz, s // q),
        in_specs=[pl.BlockSpec((None, q, d), lambda b, c: (b, c, 7)),
                  pl.BlockSpec((None, q, d), lambda b, c: (b, c, 8)),
                  pl.BlockSpec((None, q, d), lambda b, c: (b, c, 6)),
                  pl.BlockSpec((None, q, LANES), lambda b, c: (b, c, 0)),
                  const((SSM_CONV, d)), const((SSM_CONV, d)), const((1, d)), const((1, d)),
                  const((1, LANES)), const((1, LANES)), const((1, d)), const((1, d))],
        out_specs=[pl.BlockSpec((None, q, d), lambda b, c: (b, c, 0)),
                   pl.BlockSpec((None, heads * SSM_HEAD_DIM, SSM_STATE), lambda b, c: (b, 0, 0))],
        out_shape=[jax.ShapeDtypeStruct((bsz, s, d), BF16),
                   jax.ShapeDtypeStruct((bsz, heads * SSM_HEAD_DIM, SSM_STATE), F32)],
        scratch_shapes=[pltpu.VMEM((q + SUBLANES, d), F32), pltpu.VMEM((q + SUBLANES, d), F32),
                        pltpu.VMEM((q, d), F32), pltpu.VMEM((q, d), F32),
                        pltpu.VMEM((heads * SSM_HEAD_DIM, SSM_STATE), F32)],
        compiler_params=_cparams("parallel", "arbitrary"),
        name="ssd_prompt",
    )(p3, p3, p3, dtr3, cw[:, :d], cw[:, d:], cb[:, :d], cb[:, d:], dtb, alog, dexp, ng)
    return y, ht


def _ssd_sample_kernel(x_ref, bc_ref, z_ref, dtr_ref, s0x, s0bc, s1x, s1bc, s2x, s2bc, h0_ref,
                       cwx_ref, cwbc_ref, cbx_ref, cbbc_ref, dtb_ref, alog_ref, dexp_ref, ng_ref,
                       y_ref, ht_ref, padx, padd, pada, xs_s, bc_s, y_s):
    bt = DEC_TILE
    hd = SSM_HEAD_DIM
    gs = SSM_STATE
    d = x_ref.shape[1]
    n_chunk = d // LANES
    hpg = d // (SSM_GROUPS * hd)
    gw = hpg * hd
    n_bc = SSM_GROUPS * gs

    @pl.when(pl.program_id(0) == 0)
    def _():
        padx[...] = jnp.zeros(padx.shape, F32)
        padd[...] = jnp.zeros(padd.shape, F32)
        pada[...] = jnp.zeros(pada.shape, F32)

    def conv(s0, s1, s2, u, w_ref, b_ref):
        acc = s0[...] * w_ref[0:1, :] + s1[...] * w_ref[1:2, :] + s2[...] * w_ref[2:3, :] + u[...] * w_ref[3:4, :]
        return _silu(acc + b_ref[...])

    xs = conv(s0x, s1x, s2x, x_ref, cwx_ref, cbx_ref)
    xs_s[...] = xs
    bc_s[...] = conv(s0bc, s1bc, s2bc, bc_ref, cwbc_ref, cbbc_ref)
    dt = _softplus(dtr_ref[...] + dtb_ref[...])
    da = jnp.exp(dt * (-jnp.exp(alog_ref[...])))
    padx[0:bt, :] = xs
    padd[0:bt, :] = dt
    pada[0:bt, :] = da
    dt_t = padd[...].T
    da_t = pada[...].T
    lane = lax.broadcasted_iota(jnp.int32, (LANES, LANES), 1)

    for ci in range(n_chunk):
        g = (2 * ci) // hpg
        csl = slice(ci * LANES, (ci + 1) * LANES)
        x_t = padx[:, csl].T
        dt_rows = jnp.concatenate([jnp.broadcast_to(dt_t[2 * ci:2 * ci + 1, :], (hd, LANES)),
                                   jnp.broadcast_to(dt_t[2 * ci + 1:2 * ci + 2, :], (hd, LANES))], axis=0)
        dtx_t = x_t * dt_rows
        y_t = jnp.zeros((LANES, LANES), F32)
        for k in range(bt):
            b_row = bc_s[k:k + 1, g * gs:(g + 1) * gs]
            c_row = bc_s[k:k + 1, n_bc + g * gs:n_bc + (g + 1) * gs]
            col = jnp.broadcast_to(dtx_t[:, k:k + 1], (LANES, gs))
            da_rows = jnp.concatenate(
                [jnp.broadcast_to(da_t[2 * ci:2 * ci + 1, k:k + 1], (hd, gs)),
                 jnp.broadcast_to(da_t[2 * ci + 1:2 * ci + 2, k:k + 1], (hd, gs))], axis=0)
            hn = h0_ref[k, csl, :] * da_rows + col * b_row
            ht_ref[k, csl, :] = hn
            ycol = jnp.sum(hn * c_row, axis=-1, keepdims=True)
            y_t = jnp.where(lane == k, ycol, y_t)
        y_s[:, csl] = y_t.T[0:bt, :]

    for g in range(SSM_GROUPS):
        gsl = slice(g * gw, (g + 1) * gw)
        y_ref[:, gsl] = _gated_group_norm(y_s[:, gsl], xs_s[:, gsl], z_ref[:, gsl], dexp_ref[:, gsl],
                                          ng_ref[:, gsl])


def _ssd_sample(p2, dtr2, conv_state2d, h0, cw, cb, dtb, alog, dexp, ng, d):
    db = p2.shape[0]
    bt = DEC_TILE
    hp = h0.shape[1]
    row = lambda off: pl.BlockSpec((bt, d), lambda i: (i, off))
    const = lambda shape: pl.BlockSpec(shape, lambda i: (0, 0))
    y, ht = pl.pallas_call(
        _ssd_sample_kernel,
        grid=(db // bt,),
        in_specs=[row(7), row(8), row(6), pl.BlockSpec((bt, LANES), lambda i: (i, 0)),
                  row(0), row(1), row(2), row(3), row(4), row(5),
                  pl.BlockSpec((bt, hp, SSM_STATE), lambda i: (i, 0, 0)),
                  const((SSM_CONV, d)), const((SSM_CONV, d)), const((1, d)), const((1, d)),
                  const((1, LANES)), const((1, LANES)), const((1, d)), const((1, d))],
        out_specs=[pl.BlockSpec((bt, d), lambda i: (i, 0)),
                   pl.BlockSpec((bt, hp, SSM_STATE), lambda i: (i, 0, 0))],
        out_shape=[jax.ShapeDtypeStruct((db, d), F32),
                   jax.ShapeDtypeStruct((db, hp, SSM_STATE), F32)],
        scratch_shapes=[pltpu.VMEM((LANES, d), F32), pltpu.VMEM((LANES, LANES), F32),
                        pltpu.VMEM((LANES, LANES), F32), pltpu.VMEM((bt, d), F32),
                        pltpu.VMEM((bt, d), F32), pltpu.VMEM((bt, d), F32)],
        compiler_params=_cparams("arbitrary"),
        name="ssd_sample",
    )(p2, p2, p2, dtr2, conv_state2d, conv_state2d, conv_state2d, conv_state2d, conv_state2d,
      conv_state2d, h0, cw[:, :d], cw[:, d:], cb[:, :d], cb[:, d:], dtb, alog, dexp, ng)
    return y, ht


def _t5_bucket_np(dist):
    d = np.asarray(dist, np.int32)
    max_exact = N_BUCKETS // 2
    df = np.maximum(d, 1).astype(np.float32)
    large = max_exact + (np.log(df / np.float32(max_exact)) / np.float32(math.log(MAX_DISTANCE / max_exact))
                         * np.float32(N_BUCKETS - max_exact)).astype(np.int32)
    large = np.minimum(large, N_BUCKETS - 1)
    return np.where(d < max_exact, d, large).astype(np.int32)


def _bias_index_tables():
    blk = ATT_BLOCK
    qi = np.arange(blk)[:, None]
    kj = np.arange(2 * blk)[None, :]
    sdist = qi + blk - kj
    idx_p, idx_d = [], []
    for window, dil in ATT_GROUPS:
        nk = window // dil
        band = (sdist >= 0) & (sdist <= nk)
        idx_p.append(np.where(band, _t5_bucket_np(np.clip(sdist, 0, nk) * dil), -1))
        steps = nk - np.arange(nk)
        idx_d.append(np.broadcast_to(_t5_bucket_np(steps * dil)[:, None], (nk, LANES)))
    return np.stack(idx_p).astype(np.int32), np.stack(idx_d).astype(np.int32)


def _bias_kernel(tab_ref, idxp_ref, idxd_ref, bp_ref, bd_ref):
    h = pl.program_id(0)
    ip = idxp_ref[...]
    idd = idxd_ref[...]
    accp = jnp.full(ip.shape, NEG_INF, F32)
    accd = jnp.full(idd.shape, NEG_INF, F32)
    for b in range(N_BUCKETS):
        v = tab_ref[b, h]
        accp = jnp.where(ip == b, v, accp)
        accd = jnp.where(idd == b, v, accd)
    bp_ref[...] = accp
    bd_ref[...] = accd


def _build_bias(rel_bias):
    idx_p, idx_d = _bias_index_tables()
    n_heads = rel_bias.shape[1]
    blk = ATT_BLOCK
    nk = idx_d.shape[1]
    return pl.pallas_call(
        _bias_kernel,
        grid=(n_heads,),
        in_specs=[pl.BlockSpec(memory_space=pltpu.SMEM),
                  pl.BlockSpec((None, blk, 2 * blk), lambda h: (h // ATT_HPG, 0, 0)),
                  pl.BlockSpec((None, nk, LANES), lambda h: (h // ATT_HPG, 0, 0))],
        out_specs=[pl.BlockSpec((None, blk, 2 * blk), lambda h: (h, 0, 0)),
                   pl.BlockSpec((None, nk, LANES), lambda h: (h, 0, 0))],
        out_shape=[jax.ShapeDtypeStruct((n_heads, blk, 2 * blk), F32),
                   jax.ShapeDtypeStruct((n_heads, nk, LANES), F32)],
        compiler_params=_cparams("arbitrary"),
        name="t5_bias",
    )(rel_bias, jnp.asarray(idx_p), jnp.asarray(idx_d))


def _attn_prompt_kernel(q_ref, kc_ref, kp_ref, vc_ref, vp_ref, bias_ref, o_ref, m_ref, l_ref):
    blk = ATT_BLOCK
    hdim = ATT_HEAD_DIM
    has_prev = pl.program_id(2) > 0
    for h in range(ATT_HPG):
        sl = slice(h * hdim, (h + 1) * hdim)
        qh = q_ref[:, sl]
        bias = bias_ref[h]
        s_prev = _bdot_nt(qh, kp_ref[:, sl]) * ATT_SCALE + bias[:, :blk]
        s_prev = jnp.where(has_prev, s_prev, NEG_INF)
        s_cur = _bdot_nt(qh, kc_ref[:, sl]) * ATT_SCALE + bias[:, blk:]
        m = jnp.maximum(jnp.max(s_prev, axis=-1, keepdims=True), jnp.max(s_cur, axis=-1, keepdims=True))
        p_prev = jnp.exp(s_prev - m)
        p_cur = jnp.exp(s_cur - m)
        l = jnp.sum(p_prev, axis=-1, keepdims=True) + jnp.sum(p_cur, axis=-1, keepdims=True)
        o_ref[:, sl] = _bdot(p_prev, vp_ref[:, sl]) + _bdot(p_cur, vc_ref[:, sl])
        m_ref[:, sl] = jnp.broadcast_to(m, (blk, hdim))
        l_ref[:, sl] = jnp.broadcast_to(l, (blk, hdim))


def _attn_prompt_group(qkv3, bias_g, gi, dil, bsz, s):
    blk = ATT_BLOCK
    gw = ATT_HPG * ATT_HEAD_DIM
    aw = qkv3.shape[2] // 3
    ncol = 3 * aw // gw
    ln = s // dil
    nb = ln // blk
    view = qkv3.reshape(bsz, ln, dil * 3 * aw)
    qoff, koff, voff = gi, aw // gw + gi, 2 * aw // gw + gi

    def cur(off):
        return pl.BlockSpec((None, blk, gw), lambda b, r, n: (b, n, r * ncol + off))

    def prev(off):
        return pl.BlockSpec((None, blk, gw), lambda b, r, n: (b, jnp.maximum(n - 1, 0), r * ncol + off))

    out_spec = pl.BlockSpec((None, blk, gw), lambda b, r, n: (b, n, r))
    out_sds = jax.ShapeDtypeStruct((bsz, ln, dil * gw), F32)
    o, m, l = pl.pallas_call(
        _attn_prompt_kernel,
        grid=(bsz, dil, nb),
        in_specs=[cur(qoff), cur(koff), prev(koff), cur(voff), prev(voff),
                  pl.BlockSpec((ATT_HPG, blk, 2 * blk), lambda b, r, n: (0, 0, 0))],
        out_specs=[out_spec, out_spec, out_spec],
        out_shape=[out_sds, out_sds, out_sds],
        compiler_params=_cparams("parallel", "parallel", "arbitrary"),
        name=f"attn_prompt_g{gi}",
    )(view, view, view, view, view, bias_g)
    return [t.reshape(bsz * s, gw) for t in (o, m, l)]


def _merge_kernel(o0, m0, l0, o1, m1, l1, o2, m2, l2, y_ref):
    ms = [m0[...], m1[...], m2[...]]
    m_all = jnp.maximum(jnp.maximum(ms[0], ms[1]), ms[2])
    os_ = [o0, o1, o2]
    ls = [l0, l1, l2]
    w = jnp.exp(ms[0] - m_all)
    num = w * os_[0][...]
    den = w * ls[0][...]
    for g in range(1, 3):
        w = jnp.exp(ms[g] - m_all)
        num = num + w * os_[g][...]
        den = den + w * ls[g][...]
    y_ref[...] = (num / den).astype(y_ref.dtype)


def _merge_groups(parts, tm):
    m, gw = parts[0][0].shape
    spec = pl.BlockSpec((tm, gw), lambda i: (i, 0))
    flat = [t for grp in parts for t in grp]
    return pl.pallas_call(
        _merge_kernel,
        grid=(m // tm,),
        in_specs=[spec] * 9,
        out_specs=spec,
        out_shape=jax.ShapeDtypeStruct((m, gw), BF16),
        compiler_params=_cparams("parallel"),
        name="attn_merge",
    )(*flat)


def _attn_sample_kernel(qkv_ref, c0_ref, c1_ref, c2_ref, bias_ref, bias0_ref, y_ref):
    hdim = ATT_HEAD_DIM
    gw = ATT_HPG * hdim
    aw = qkv_ref.shape[2] // 3
    caches = (c0_ref, c1_ref, c2_ref)

    def body(b, carry):
        row = qkv_ref[b]
        for h in range(ATT_HPG):
            res = []
            for gi in range(len(ATT_GROUPS)):
                col = gi * gw + h * hdim
                qh = row[:, col:col + hdim]
                k_new = row[:, aw + col:aw + col + hdim]
                v_new = row[:, 2 * aw + col:2 * aw + col + hdim]
                kc = caches[gi][b, :, h * hdim:(h + 1) * hdim]
                vc = caches[gi][b, :, gw + h * hdim:gw + (h + 1) * hdim]
                s_past = jnp.sum(kc * qh, axis=-1, keepdims=True) * ATT_SCALE + bias_ref[gi * ATT_HPG + h]
                s_self = (jnp.sum(k_new * qh, axis=-1, keepdims=True) * ATT_SCALE
                          + bias0_ref[:, col:col + hdim])
                m = jnp.maximum(jnp.max(s_past, axis=0, keepdims=True), s_self)
                p_past = jnp.exp(s_past - m)
                p_self = jnp.exp(s_self - m)
                l = jnp.sum(p_past, axis=0, keepdims=True) + p_self
                o = jnp.sum(p_past * vc, axis=0, keepdims=True) + p_self * v_new
                res.append((o, m, l))
            m_all = jnp.maximum(jnp.maximum(res[0][1], res[1][1]), res[2][1])
            w = jnp.exp(res[0][1] - m_all)
            num = w * res[0][0]
            den = w * res[0][2]
            for gi in range(1, len(ATT_GROUPS)):
                w = jnp.exp(res[gi][1] - m_all)
                num = num + w * res[gi][0]
                den = den + w * res[gi][2]
            y_ref[b, :, h * hdim:(h + 1) * hdim] = num / den
        return carry

    lax.fori_loop(0, DEC_TILE, body, 0)


def _attn_sample(qkv2, caches, bias_d, bias0):
    db = qkv2.shape[0]
    bt = DEC_TILE
    gw = ATT_HPG * ATT_HEAD_DIM
    views, specs = [], []
    for (window, dil), cache in zip(ATT_GROUPS, caches):
        nk = window // dil
        wb = cache.shape[1]
        assert wb == window, "cached window must equal the group's window"
        views.append(cache.reshape(db, nk, dil * 2 * gw))
        specs.append(pl.BlockSpec((bt, nk, 2 * gw), lambda i: (i, 0, 0)))
    n_heads = bias_d.shape[0]
    nq = qkv2.shape[1]
    y = pl.pallas_call(
        _attn_sample_kernel,
        grid=(db // bt,),
        in_specs=[pl.BlockSpec((bt, 1, nq), lambda i: (i, 0, 0))] + specs +
                 [pl.BlockSpec((n_heads,) + bias_d.shape[1:], lambda i: (0, 0, 0)),
                  pl.BlockSpec((1, bias0.shape[1]), lambda i: (0, 0))],
        out_specs=pl.BlockSpec((bt, 1, gw), lambda i: (i, 0, 0)),
        out_shape=jax.ShapeDtypeStruct((db, 1, gw), F32),
        compiler_params=_cparams("parallel"),
        name="attn_sample",
    )(qkv2.reshape(db, 1, nq), *views, bias_d, bias0)
    return y.reshape(db, gw)


def _branch_kernel(ysc_ref, yssm_ref, yatt_ref, wsc_ref, wssm_ref, watt_ref, g0_ref, g1_ref, g2_ref, o_ref):
    a = jnp.dot(ysc_ref[...], wsc_ref[...], preferred_element_type=F32)
    b = jnp.dot(yssm_ref[...], wssm_ref[...], preferred_element_type=F32)
    c = jnp.dot(yatt_ref[...], watt_ref[...], preferred_element_type=F32)
    o = _sigmoid(g0_ref[...]) * a + _sigmoid(g1_ref[...]) * b + _sigmoid(g2_ref[...]) * c
    o_ref[...] = o.astype(o_ref.dtype)


def _branch_merge(ysc, yssm, yatt, wsc, wssm, watt, p2, tm, tn):
    m, d = ysc.shape
    nj = d // tn
    rows = lambda width: pl.BlockSpec((tm, width), lambda i, j: (i, 0))
    wcol = lambda k: pl.BlockSpec((k, tn), lambda i, j: (0, j))
    gate = lambda br: pl.BlockSpec((tm, tn), lambda i, j: (i, br * nj + j))
    return pl.pallas_call(
        _branch_kernel,
        grid=(m // tm, nj),
        in_specs=[rows(d), rows(d), rows(yatt.shape[1]), wcol(d), wcol(d), wcol(yatt.shape[1]),
                  gate(0), gate(1), gate(2)],
        out_specs=pl.BlockSpec((tm, tn), lambda i, j: (i, j)),
        out_shape=jax.ShapeDtypeStruct((m, d), BF16),
        compiler_params=_cparams("parallel", "arbitrary"),
        name="branch_merge",
    )(ysc, yssm, yatt, wsc, wssm, watt, p2, p2, p2)


def _outproj_kernel(x_ref, mg_ref, w_ref, g_ref, x1_ref, h2_ref):
    x1 = x_ref[...] + jnp.dot(mg_ref[...], w_ref[...], preferred_element_type=F32)
    x1_ref[...] = x1
    ms = jnp.mean(x1 * x1, axis=-1, keepdims=True)
    h2_ref[...] = (x1 * lax.rsqrt(ms + EPS) * g_ref[...]).astype(h2_ref.dtype)


def _outproj(x2d, merged, w_out, g2, tm):
    m, d = x2d.shape
    rows = pl.BlockSpec((tm, d), lambda i: (i, 0))
    return pl.pallas_call(
        _outproj_kernel,
        grid=(m // tm,),
        in_specs=[rows, rows, pl.BlockSpec((d, d), lambda i: (0, 0)), pl.BlockSpec((1, d), lambda i: (0, 0))],
        out_specs=[rows, rows],
        out_shape=[jax.ShapeDtypeStruct((m, d), F32), jax.ShapeDtypeStruct((m, d), BF16)],
        compiler_params=_cparams("parallel"),
        name="outproj_norm",
    )(x2d, merged, w_out, g2.reshape(1, d))


def _mlp_kernel(x1_ref, h2_ref, wu_ref, wd_ref, o_ref):
    @pl.when(pl.program_id(1) == 0)
    def _():
        o_ref[...] = x1_ref[...]

    u = jnp.dot(h2_ref[...], wu_ref[...], preferred_element_type=F32)
    u = jnp.square(jnp.maximum(u, 0.0)).astype(BF16)
    o_ref[...] += jnp.dot(u, wd_ref[...], preferred_element_type=F32)


def _mlp(x1, h2, w_up, w_down, tm, tf):
    m, d = x1.shape
    f = w_up.shape[1]
    rows = pl.BlockSpec((tm, d), lambda i, j: (i, 0))
    return pl.pallas_call(
        _mlp_kernel,
        grid=(m // tm, f // tf),
        in_specs=[rows, rows, pl.BlockSpec((d, tf), lambda i, j: (0, j)),
                  pl.BlockSpec((tf, d), lambda i, j: (j, 0))],
        out_specs=rows,
        out_shape=jax.ShapeDtypeStruct((m, d), F32),
        compiler_params=_cparams("parallel", "arbitrary"),
        name="mlp",
    )(x1, h2, w_up, w_down)


def _row_tile(m, pref):
    return pref if m % pref == 0 else m


def _pad_lanes(v):
    return jnp.pad(v.astype(F32), (0, LANES - v.shape[0])).reshape(1, LANES)


def _prep_layer(l, norm1_g, w_in, sc_conv_w, ssm_conv_w, ssm_conv_b, ssm_dt_bias, ssm_A_log, ssm_D,
                ssm_norm_g, q_norm_g, k_norm_g, w_br_sc, w_br_ssm, w_br_att, w_out, norm2_g, w_up, w_down):
    d = w_in.shape[1]
    heads = d // SSM_HEAD_DIM
    aw = ATT_HPG * len(ATT_GROUPS) * ATT_HEAD_DIM
    n_main = N_BRANCH * d + 3 * d + d + (d + 2 * SSM_GROUPS * SSM_STATE)
    wl = w_in[l]
    w_dt = jnp.pad(wl[:, n_main:n_main + heads], ((0, 0), (0, LANES - heads)))
    n_att_heads = aw // ATT_HEAD_DIM
    gains = jnp.concatenate([jnp.tile(q_norm_g[l], n_att_heads), jnp.tile(k_norm_g[l], n_att_heads),
                             jnp.ones((aw,), F32)]).reshape(1, 3 * aw)
    return dict(
        norm1_g=norm1_g[l], norm2_g=norm2_g[l],
        w_main=wl[:, :n_main].astype(BF16), w_dt=w_dt.astype(BF16),
        w_qkv=wl[:, n_main + heads:].astype(BF16), qk_gains=gains,
        sc_conv_w=sc_conv_w[l], ssm_conv_w=ssm_conv_w[l], ssm_conv_b=ssm_conv_b[l].reshape(1, -1),
        dt_bias=_pad_lanes(ssm_dt_bias[l]), a_log=_pad_lanes(ssm_A_log[l]),
        d_exp=jnp.repeat(ssm_D[l].astype(F32), SSM_HEAD_DIM).reshape(1, d),
        ssm_norm_g=ssm_norm_g[l].reshape(1, d),
        w_br_sc=w_br_sc[l].astype(BF16), w_br_ssm=w_br_ssm[l].astype(BF16), w_br_att=w_br_att[l].astype(BF16),
        w_out=w_out[l].astype(BF16), w_up=w_up[l].astype(BF16), w_down=w_down[l].astype(BF16),
    )


def _in_proj(x2d, w):
    m = x2d.shape[0]
    tm = _row_tile(m, 1024)
    h = _rmsnorm(x2d, w["norm1_g"], _row_tile(m, 512))
    p = _matmul(h, w["w_main"], tm, 1024, name="in_proj_main")
    dtr = _matmul(h, w["w_dt"], tm, LANES, name="in_proj_dt")
    qkv = _matmul_qkv(h, w["w_qkv"], w["qk_gains"], tm, ATT_HPG * ATT_HEAD_DIM)
    return p, dtr, qkv


def _tail(x2d, p, ysc, yssm, yatt, w):
    m = x2d.shape[0]
    merged = _branch_merge(ysc, yssm, yatt, w["w_br_sc"], w["w_br_ssm"], w["w_br_att"], p,
                           _row_tile(m, 1024), 512)
    x1, h2 = _outproj(x2d, merged, w["w_out"], w["norm2_g"], _row_tile(m, 512))
    return _mlp(x1, h2, w["w_up"], w["w_down"], _row_tile(m, 512), 1024)


def _kv_rows(qkv3, gi, keep):
    aw = qkv3.shape[2] // 3
    gw = ATT_HPG * ATT_HEAD_DIM
    bsz, ln, _ = qkv3.shape
    k = qkv3[:, ln - keep:, aw + gi * gw:aw + (gi + 1) * gw].reshape(bsz, keep, ATT_HPG, ATT_HEAD_DIM)
    v = qkv3[:, ln - keep:, 2 * aw + gi * gw:2 * aw + (gi + 1) * gw].reshape(bsz, keep, ATT_HPG, ATT_HEAD_DIM)
    return jnp.stack([k, v], axis=2)


def _prompt_block(x3, w, bias_p):
    bsz, s, d = x3.shape
    x2d = x3.reshape(bsz * s, d)
    p, dtr, qkv = _in_proj(x2d, w)
    p3 = p.reshape(bsz, s, -1)
    qkv3 = qkv.reshape(bsz, s, -1)
    ysc, sc_new = _sc_prompt(p3, w["sc_conv_w"], d, 512, 512)
    yssm, ht = _ssd_prompt(p3, dtr.reshape(bsz, s, LANES), w["ssm_conv_w"], w["ssm_conv_b"], w["dt_bias"],
                           w["a_log"], w["d_exp"], w["ssm_norm_g"], d)
    xbc_off = N_BRANCH * d + 3 * d + d
    ssm_conv_new = p3[:, s - (SSM_CONV - 1):, xbc_off:]
    parts, kv_new = [], []
    for gi, (window, dil) in enumerate(ATT_GROUPS):
        parts.append(_attn_prompt_group(qkv3, bias_p[gi * ATT_HPG:(gi + 1) * ATT_HPG], gi, dil, bsz, s))
        kv_new.append(_kv_rows(qkv3, gi, min(window, s)))
    yatt = _merge_groups(parts, 1024)
    x_out = _tail(x2d, p, ysc.reshape(bsz * s, d), yssm.reshape(bsz * s, d), yatt, w)
    heads = d // SSM_HEAD_DIM
    st = (sc_new, ssm_conv_new, ht.reshape(bsz, heads, SSM_HEAD_DIM, SSM_STATE), kv_new[0], kv_new[1], kv_new[2])
    return x_out.reshape(bsz, s, d), st


def _sample_block(x3, w, bias_d, bias0, sc_buf, ssm_buf, ssm_h0, kv_bufs):
    db, t, d = x3.shape
    assert t == 1, "decode kernels handle one new token per sample"
    x2d = x3.reshape(db, d)
    p, dtr, qkv = _in_proj(x2d, w)
    ysc, u_new = _sc_sample(p, sc_buf.reshape(db, -1), w["sc_conv_w"], d, 512)
    sc_new = jnp.concatenate([sc_buf[:, 1:], u_new[:, None, :]], axis=1)
    heads = d // SSM_HEAD_DIM
    yssm, ht = _ssd_sample(p, dtr, ssm_buf.reshape(db, -1), ssm_h0.reshape(db, heads * SSM_HEAD_DIM, SSM_STATE),
                           w["ssm_conv_w"], w["ssm_conv_b"], w["dt_bias"], w["a_log"], w["d_exp"],
                           w["ssm_norm_g"], d)
    xbc_off = N_BRANCH * d + 3 * d + d
    ssm_conv_new = jnp.concatenate([ssm_buf[:, 1:], p[:, None, xbc_off:]], axis=1)
    yatt = _attn_sample(qkv, kv_bufs, bias_d, bias0)
    qkv3 = qkv.reshape(db, 1, -1)
    kv_new = [_kv_rows(qkv3, gi, 1) for gi in range(len(ATT_GROUPS))]
    x_out = _tail(x2d, p, ysc.astype(BF16), yssm.astype(BF16), yatt.astype(BF16), w)
    st = (sc_new, ssm_conv_new, ht.reshape(db, heads, SSM_HEAD_DIM, SSM_STATE), kv_new[0], kv_new[1], kv_new[2])
    return x_out.reshape(db, 1, d), st


def kernel(x_prompt, x_sample, state_sc_conv, state_ssm_conv, state_ssm, cache_kv_w128, cache_kv_w512,
           cache_kv_w2048, norm1_g, w_in, sc_conv_w, ssm_conv_w, ssm_conv_b, ssm_dt_bias, ssm_A_log, ssm_D,
           ssm_norm_g, q_norm_g, k_norm_g, rel_bias, w_br_sc, w_br_ssm, w_br_att, w_out, norm2_g, w_up, w_down):
    depth = w_in.shape[0]
    bias_p, bias_d = _build_bias(rel_bias)
    bias0 = jnp.repeat(rel_bias[0], ATT_HEAD_DIM).reshape(1, -1)
    xp, xs = x_prompt, x_sample
    p_new, s_new = [], []
    for l in range(depth):
        w = _prep_layer(l, norm1_g, w_in, sc_conv_w, ssm_conv_w, ssm_conv_b, ssm_dt_bias, ssm_A_log, ssm_D,
                        ssm_norm_g, q_norm_g, k_norm_g, w_br_sc, w_br_ssm, w_br_att, w_out, norm2_g, w_up,
                        w_down)
        xp, st = _prompt_block(xp, w, bias_p)
        p_new.append(st)
        xs, st = _sample_block(xs, w, bias_d, bias0, state_sc_conv[l], state_ssm_conv[l], state_ssm[l],
                               (cache_kv_w128[l], cache_kv_w512[l], cache_kv_w2048[l]))
        s_new.append(st)
    p_out = [jnp.stack(a) for a in zip(*p_new)]
    s_out = [jnp.stack(a) for a in zip(*s_new)]
    return (xp, xs, *p_out, *s_out)
```

```python
import functools
import math

import numpy as np
import jax
import jax.numpy as jnp
from jax import lax
from jax.experimental import pallas as pl
from jax.experimental.pallas import tpu as pltpu

F32 = jnp.float32
BF16 = jnp.bfloat16

N_BRANCH = 3
SC_CONV = 3
SSM_HEAD_DIM = 64
SSM_GROUPS = 8
SSM_STATE = 128
SSM_CONV = 4
SSM_CHUNK = 128
ATT_HEAD_DIM = 128
ATT_GROUPS = ((128, 1), (512, 4), (2048, 16))
ATT_HPG = 4
ATT_BLOCK = 128
ATT_SCALE = 1.0 / math.sqrt(ATT_HEAD_DIM)
N_BUCKETS = 32
MAX_DISTANCE = 2048
EPS = 1e-6

LANES = 128
SUBLANES = 8
VMEM_LIMIT_BYTES = 56 * 1024 * 1024

DEC_TILE = 8
NEG_INF = float("-inf")


def _cparams(*sem):
    return pltpu.CompilerParams(dimension_semantics=sem, vmem_limit_bytes=VMEM_LIMIT_BYTES)


def _sigmoid(v):
    return 1.0 / (1.0 + jnp.exp(-v))


def _silu(v):
    return v * _sigmoid(v)


def _bdot(a, b):
    return jnp.dot(a.astype(BF16), b.astype(BF16), preferred_element_type=F32)


def _bdot_nt(a, b):
    return lax.dot_general(a.astype(BF16), b.astype(BF16), (((1,), (1,)), ((), ())),
                           preferred_element_type=F32)


def _rmsnorm_kernel(x_ref, g_ref, o_ref):
    x = x_ref[...]
    ms = jnp.mean(x * x, axis=-1, keepdims=True)
    o_ref[...] = (x * lax.rsqrt(ms + EPS) * g_ref[...]).astype(o_ref.dtype)


def _rmsnorm(x2d, g, tm):
    m, d = x2d.shape
    return pl.pallas_call(
        _rmsnorm_kernel,
        grid=(m // tm,),
        in_specs=[pl.BlockSpec((tm, d), lambda i: (i, 0)),
                  pl.BlockSpec((1, d), lambda i: (0, 0))],
        out_specs=pl.BlockSpec((tm, d), lambda i: (i, 0)),
        out_shape=jax.ShapeDtypeStruct((m, d), BF16),
        compiler_params=_cparams("parallel"),
        name="rmsnorm",
    )(x2d, g.reshape(1, d))


def _mm_wcast_kernel(a_ref, w_ref, o_ref, wb_ref):
    @pl.when(pl.program_id(1) == 0)
    def _():
        wb_ref[...] = w_ref[...].astype(BF16)

    o_ref[...] = jnp.dot(a_ref[...], wb_ref[...], preferred_element_type=F32)


def _matmul_wcast(a, w_all, layer, n_cols, tm, tn, name):
    m, k = a.shape
    return pl.pallas_call(
        _mm_wcast_kernel,
        grid=(n_cols // tn, m // tm),
        in_specs=[pl.BlockSpec((tm, k), lambda j, i: (i, 0)),
                  pl.BlockSpec((None, k, tn), lambda j, i: (layer, 0, j))],
        out_specs=pl.BlockSpec((tm, tn), lambda j, i: (i, j)),
        out_shape=jax.ShapeDtypeStruct((m, n_cols), F32),
        scratch_shapes=[pltpu.VMEM((k, tn), BF16)],
        compiler_params=_cparams("parallel", "arbitrary"),
        name=name,
    )(a, w_all)


def _mm_qkv_kernel(a_ref, w_ref, g_ref, o_ref, wb_ref, *, n_norm_blocks, heads_per_block):
    @pl.when(pl.program_id(1) == 0)
    def _():
        wb_ref[...] = w_ref[...].astype(BF16)

    acc = jnp.dot(a_ref[...], wb_ref[...], preferred_element_type=F32)
    j = pl.program_id(0)

    @pl.when(j < n_norm_blocks)
    def _():
        for h in range(heads_per_block):
            sl = slice(h * ATT_HEAD_DIM, (h + 1) * ATT_HEAD_DIM)
            t = acc[:, sl]
            ms = jnp.mean(t * t, axis=-1, keepdims=True)
            o_ref[:, sl] = t * lax.rsqrt(ms + EPS) * g_ref[:, sl]

    @pl.when(j >= n_norm_blocks)
    def _():
        o_ref[...] = acc


def _matmul_qkv(a, w, gains, tm, tn):
    m, k = a.shape
    n = w.shape[1]
    att_width = n // 3
    return pl.pallas_call(
        functools.partial(_mm_qkv_kernel, n_norm_blocks=2 * att_width // tn,
                          heads_per_block=tn // ATT_HEAD_DIM),
        grid=(n // tn, m // tm),
        in_specs=[pl.BlockSpec((tm, k), lambda j, i: (i, 0)),
                  pl.BlockSpec((k, tn), lambda j, i: (0, j)),
                  pl.BlockSpec((1, tn), lambda j, i: (0, j))],
        out_specs=pl.BlockSpec((tm, tn), lambda j, i: (i, j)),
        out_shape=jax.ShapeDtypeStruct((m, n), F32),
        scratch_shapes=[pltpu.VMEM((k, tn), BF16)],
        compiler_params=_cparams("parallel", "arbitrary"),
        name="matmul_qkv",
    )(a, w, gains)


def _sc_prompt_kernel(b_ref, c_ref, x_ref, hc_ref, hx_ref, w_ref, y_ref, tail_ref, ext_ref, *, ts):
    t = pl.program_id(2)
    u = c_ref[...] * x_ref[...]
    hu = hc_ref[...] * hx_ref[...]
    ext_ref[0:SUBLANES, :] = jnp.where(t > 0, hu, 0.0)
    ext_ref[SUBLANES:SUBLANES + ts, :] = u
    u2 = ext_ref[pl.ds(SUBLANES - 2, ts), :]
    u1 = ext_ref[pl.ds(SUBLANES - 1, ts), :]
    y = u2 * w_ref[0:1, :] + u1 * w_ref[1:2, :] + u * w_ref[2:3, :]
    y_ref[...] = (b_ref[...] * y).astype(y_ref.dtype)
    tail_ref[...] = ext_ref[pl.ds(ts, SUBLANES), :]


def _sc_prompt(p3, conv_w, d, ts, tc):
    bsz, s, _ = p3.shape
    ncb = d // tc
    off_b, off_c, off_x = 3 * ncb, 4 * ncb, 5 * ncb
    hb = ts // SUBLANES

    def halo(off):
        return pl.BlockSpec((None, SUBLANES, tc),
                            lambda b, c, t: (b, jnp.maximum(t * hb - 1, 0), off + c))

    def main(off):
        return pl.BlockSpec((None, ts, tc), lambda b, c, t: (b, t, off + c))

    y, tail = pl.pallas_call(
        functools.partial(_sc_prompt_kernel, ts=ts),
        grid=(bsz, ncb, s // ts),
        in_specs=[main(off_b), main(off_c), main(off_x), halo(off_c), halo(off_x),
                  pl.BlockSpec((SC_CONV, tc), lambda b, c, t: (0, c))],
        out_specs=[pl.BlockSpec((None, ts, tc), lambda b, c, t: (b, t, c)),
                   pl.BlockSpec((None, SUBLANES, tc), lambda b, c, t: (b, 0, c))],
        out_shape=[jax.ShapeDtypeStruct((bsz, s, d), BF16),
                   jax.ShapeDtypeStruct((bsz, SUBLANES, d), F32)],
        scratch_shapes=[pltpu.VMEM((ts + SUBLANES, tc), F32)],
        compiler_params=_cparams("parallel", "parallel", "arbitrary"),
        name="short_conv_prompt",
    )(p3, p3, p3, p3, p3, conv_w)
    return y, tail[:, SUBLANES - (SC_CONV - 1):, :]


def _sc_sample_kernel(b_ref, c_ref, x_ref, s0_ref, s1_ref, w_ref, y_ref, u_ref):
    u = c_ref[...] * x_ref[...]
    y = s0_ref[...] * w_ref[0:1, :] + s1_ref[...] * w_ref[1:2, :] + u * w_ref[2:3, :]
    y_ref[...] = b_ref[...] * y
    u_ref[...] = u


def _sc_sample(p2, state2d, conv_w, d, tc):
    db = p2.shape[0]
    ncb = d // tc

    def col(off):
        return pl.BlockSpec((db, tc), lambda c: (0, off + c))

    return pl.pallas_call(
        _sc_sample_kernel,
        grid=(ncb,),
        in_specs=[col(3 * ncb), col(4 * ncb), col(5 * ncb), col(0), col(ncb),
                  pl.BlockSpec((SC_CONV, tc), lambda c: (0, c))],
        out_specs=[col(0), col(0)],
        out_shape=[jax.ShapeDtypeStruct((db, d), F32), jax.ShapeDtypeStruct((db, d), F32)],
        compiler_params=_cparams("parallel"),
        name="short_conv_sample",
    )(p2, p2, p2, state2d, state2d, conv_w)


def _softplus(v):
    return jnp.maximum(v, 0.0) + jnp.log1p(jnp.exp(-jnp.abs(v)))


def _gated_group_norm(y, xs, z, dexp, ng):
    val = (y + dexp * xs) * _silu(z)
    ms = jnp.mean(val * val, axis=-1, keepdims=True)
    return val * lax.rsqrt(ms + EPS) * ng


def _ssd_prompt_kernel(x_ref, bc_ref, z_ref, dtr_ref, cwx_ref, cwbc_ref, cbx_ref, cbbc_ref, dtb_ref,
                       alog_ref, dexp_ref, ng_ref, y_ref, ht_ref, extx, extbc, xs_s, bc_s, state):
    q = SSM_CHUNK
    c = pl.program_id(1)
    pad = SUBLANES
    taps = SSM_CONV

    @pl.when(c == 0)
    def _():
        extx[0:pad, :] = jnp.zeros((pad, extx.shape[1]), F32)
        extbc[0:pad, :] = jnp.zeros((pad, extbc.shape[1]), F32)
        state[...] = jnp.zeros(state.shape, F32)

    extx[pad:pad + q, :] = x_ref[...]
    extbc[pad:pad + q, :] = bc_ref[...]

    def conv(ext, w_ref, b_ref):
        acc = ext[pl.ds(pad - taps + 1, q), :] * w_ref[0:1, :]
        for i in range(1, taps):
            acc = acc + ext[pl.ds(pad - taps + 1 + i, q), :] * w_ref[i:i + 1, :]
        return _silu(acc + b_ref[...])

    xs_s[...] = conv(extx, cwx_ref, cbx_ref)
    bc_s[...] = conv(extbc, cwbc_ref, cbbc_ref)
    extx[0:pad, :] = extx[q:q + pad, :]
    extbc[0:pad, :] = extbc[q:q + pad, :]

    dt = _softplus(dtr_ref[...] + dtb_ref[...])
    a_neg = -jnp.exp(alog_ref[...])
    ii = lax.broadcasted_iota(jnp.int32, (q, q), 0)
    jj = lax.broadcasted_iota(jnp.int32, (q, q), 1)
    causal = ii >= jj
    acum = jnp.dot(causal.astype(F32), dt * a_neg, precision=lax.Precision.HIGHEST,
                   preferred_element_type=F32)
    acum_t = acum.T
    dt_t = dt.T
    lo_half = jj < SSM_HEAD_DIM
    gs = SSM_STATE
    hd = SSM_HEAD_DIM
    hpg = xs_s.shape[1] // (SSM_GROUPS * hd)
    gw = hpg * hd
    n_bc = SSM_GROUPS * gs

    for g in range(SSM_GROUPS):
        bg = bc_s[:, g * gs:(g + 1) * gs].astype(BF16)
        cg = bc_s[:, n_bc + g * gs:n_bc + (g + 1) * gs].astype(BF16)
        cb = _bdot_nt(cg, bg)
        ys = []
        for hp in range(hpg // 2):
            e0 = g * hpg + 2 * hp
            rows = slice(e0 * hd, (e0 + 2) * hd)
            xp = xs_s[:, rows]
            xp16 = xp.astype(BF16)
            cols_a, cols_dt, parts = [], [], []
            for k in range(2):
                e = e0 + k
                col_a = jnp.broadcast_to(acum[:, e:e + 1], (q, q))
                cols_a.append(col_a)
                cols_dt.append(jnp.broadcast_to(dt[:, e:e + 1], (q, q)))
                dec = jnp.exp(jnp.where(causal, col_a - acum_t[e:e + 1, :], NEG_INF))
                wm = cb * dec * dt_t[e:e + 1, :]
                parts.append(jnp.dot(wm.astype(BF16), xp16, preferred_element_type=F32))
            y_diag = jnp.where(lo_half, parts[0], parts[1])
            sel_a = jnp.where(lo_half, cols_a[0], cols_a[1])
            sel_dt = jnp.where(lo_half, cols_dt[0], cols_dt[1])
            hpair = state[rows, :]
            y_off = _bdot_nt(cg, hpair) * jnp.exp(sel_a)
            ys.append(y_diag + y_off)
            last_a = sel_a[q - 1:q, :]
            xw = xp * jnp.exp(last_a - sel_a) * sel_dt
            st = jnp.dot(xw.T.astype(BF16), bg, preferred_element_type=F32)
            cd = jnp.concatenate([jnp.broadcast_to(cols_a[0][q - 1:q, :], (hd, gs)),
                                  jnp.broadcast_to(cols_a[1][q - 1:q, :], (hd, gs))], axis=0)
            state[rows, :] = hpair * jnp.exp(cd) + st
        gsl = slice(g * gw, (g + 1) * gw)
        yg = jnp.concatenate(ys, axis=1)
        y_ref[:, gsl] = _gated_group_norm(yg, xs_s[:, gsl], z_ref[:, gsl], dexp_ref[:, gsl],
                                          ng_ref[:, gsl]).astype(y_ref.dtype)

    @pl.when(c == pl.num_programs(1) - 1)
    def _():
        ht_ref[...] = state[...]


def _ssd_prompt(p3, dtr3, cw, cb, dtb, alog, dexp, ng, d):
    bsz, s, _ = p3.shape
    q = SSM_CHUNK
    heads = d // SSM_HEAD_DIM
    const = lambda shape: pl.BlockSpec(shape, lambda b, c: (0, 0))
    y, ht = pl.pallas_call(
        _ssd_prompt_kernel,
        grid=(bsz, s // q),
        in_specs=[pl.BlockSpec((None, q, d), lambda b, c: (b, c, 7)),
                  pl.BlockSpec((None, q, d), lambda b, c: (b, c, 8)),
                  pl.BlockSpec((None, q, d), lambda b, c: (b, c, 6)),
                  pl.BlockSpec((None, q, LANES), lambda b, c: (b, c, 0)),
                  const((SSM_CONV, d)), const((SSM_CONV, d)), const((1, d)), const((1, d)),
                  const((1, LANES)), const((1, LANES)), const((1, d)), const((1, d))],
        out_specs=[pl.BlockSpec((None, q, d), lambda b, c: (b, c, 0)),
                   pl.BlockSpec((None, heads * SSM_HEAD_DIM, SSM_STATE), lambda b, c: (b, 0, 0))],
        out_shape=[jax.ShapeDtypeStruct((bsz, s, d), BF16),
                   jax.ShapeDtypeStruct((bsz, heads * SSM_HEAD_DIM, SSM_STATE), F32)],
        scratch_shapes=[pltpu.VMEM((q + SUBLANES, d), F32), pltpu.VMEM((q + SUBLANES, d), F32),
                        pltpu.VMEM((q, d), F32), pltpu.VMEM((q, d), F32),
                        pltpu.VMEM((heads * SSM_HEAD_DIM, SSM_STATE), F32)],
        compiler_params=_cparams("parallel", "arbitrary"),
        name="ssd_prompt",
    )(p3, p3, p3, dtr3, cw[:, :d], cw[:, d:], cb[:, :d], cb[:, d:], dtb, alog, dexp, ng)
    return y, ht


def _ssd_sample_kernel(x_ref, bc_ref, z_ref, dtr_ref, s0x, s0bc, s1x, s1bc, s2x, s2bc, h0_ref, cwx_ref,
                       cwbc_ref, cbx_ref, cbbc_ref, dtb_ref, alog_ref, dexp_ref, ng_ref,
                       y_ref, ht_ref, padx, padbc, padd, pada, y_s):
    bt = DEC_TILE
    hd = SSM_HEAD_DIM
    gs = SSM_STATE
    d = x_ref.shape[1]
    n_chunk = d // LANES
    hpg = d // (SSM_GROUPS * hd)
    gw = hpg * hd
    n_bc = SSM_GROUPS * gs

    @pl.when(pl.program_id(0) == 0)
    def _():
        padx[...] = jnp.zeros(padx.shape, F32)
        padbc[...] = jnp.zeros(padbc.shape, F32)
        padd[...] = jnp.zeros(padd.shape, F32)
        pada[...] = jnp.zeros(pada.shape, F32)

    def conv(s0, s1, s2, u, w_ref, b_ref):
        acc = s0[...] * w_ref[0:1, :] + s1[...] * w_ref[1:2, :] + s2[...] * w_ref[2:3, :] + u[...] * w_ref[3:4, :]
        return _silu(acc + b_ref[...])

    padx[0:bt, :] = conv(s0x, s1x, s2x, x_ref, cwx_ref, cbx_ref)
    padbc[0:bt, :] = conv(s0bc, s1bc, s2bc, bc_ref, cwbc_ref, cbbc_ref)
    dt = _softplus(dtr_ref[...] + dtb_ref[...])
    padd[0:bt, :] = dt
    pada[0:bt, :] = jnp.exp(dt * (-jnp.exp(alog_ref[...])))
    dt_t = padd[...].T
    da_t = pada[...].T
    wide = bt * gs
    row_w = lax.broadcasted_iota(jnp.int32, (LANES, wide), 0)
    blk_w = lax.broadcasted_iota(jnp.int32, (LANES, wide), 1) // gs
    row_t = lax.broadcasted_iota(jnp.int32, (wide, LANES), 0) // gs
    lane_t = lax.broadcasted_iota(jnp.int32, (wide, LANES), 1)

    bbig = cbig_t = None
    for ci in range(n_chunk):
        g = (2 * ci) // hpg
        if (2 * ci) % hpg == 0:
            bg = padbc[:, g * gs:(g + 1) * gs]
            bbig = jnp.where(row_w == blk_w, jnp.concatenate([bg] * bt, axis=1), 0.0).astype(BF16)
            cg_t = padbc[:, n_bc + g * gs:n_bc + (g + 1) * gs].T
            cbig_t = jnp.where(row_t == lane_t, jnp.concatenate([cg_t] * bt, axis=0), 0.0).astype(BF16)
        csl = slice(ci * LANES, (ci + 1) * LANES)
        x_t = padx[:, csl].T
        dt_rows = jnp.concatenate([jnp.broadcast_to(dt_t[2 * ci:2 * ci + 1, :], (hd, LANES)),
                                   jnp.broadcast_to(dt_t[2 * ci + 1:2 * ci + 2, :], (hd, LANES))], axis=0)
        inc = jnp.dot((x_t * dt_rows).astype(BF16), bbig, preferred_element_type=F32)
        hs = []
        for k in range(bt):
            da_rows = jnp.concatenate(
                [jnp.broadcast_to(da_t[2 * ci:2 * ci + 1, k:k + 1], (hd, gs)),
                 jnp.broadcast_to(da_t[2 * ci + 1:2 * ci + 2, k:k + 1], (hd, gs))], axis=0)
            hn = h0_ref[k, csl, :] * da_rows + inc[:, k * gs:(k + 1) * gs]
            ht_ref[k, csl, :] = hn
            hs.append(hn.astype(BF16))
        y_t = jnp.dot(jnp.concatenate(hs, axis=1), cbig_t, preferred_element_type=F32)
        y_s[:, csl] = y_t.T[0:bt, :]

    for g in range(SSM_GROUPS):
        gsl = slice(g * gw, (g + 1) * gw)
        y_ref[:, gsl] = _gated_group_norm(y_s[:, gsl], padx[0:bt, gsl], z_ref[:, gsl], dexp_ref[:, gsl],
                                          ng_ref[:, gsl])


def _ssd_sample(p2, dtr2, conv_state2d, h0_all, layer, cw, cb, dtb, alog, dexp, ng, d):
    db = p2.shape[0]
    bt = DEC_TILE
    hp = h0_all.shape[2]
    row = lambda off: pl.BlockSpec((bt, d), lambda i: (i, off))
    const = lambda shape: pl.BlockSpec(shape, lambda i: (0, 0))
    y, ht = pl.pallas_call(
        _ssd_sample_kernel,
        grid=(db // bt,),
        in_specs=[row(7), row(8), row(6), pl.BlockSpec((bt, LANES), lambda i: (i, 0)),
                  row(0), row(1), row(2), row(3), row(4), row(5),
                  pl.BlockSpec((None, bt, hp, SSM_STATE), lambda i: (layer, i, 0, 0)),
                  const((SSM_CONV, d)), const((SSM_CONV, d)), const((1, d)), const((1, d)),
                  const((1, LANES)), const((1, LANES)), const((1, d)), const((1, d))],
        out_specs=[pl.BlockSpec((bt, d), lambda i: (i, 0)),
                   pl.BlockSpec((bt, hp, SSM_STATE), lambda i: (i, 0, 0))],
        out_shape=[jax.ShapeDtypeStruct((db, d), F32),
                   jax.ShapeDtypeStruct((db, hp, SSM_STATE), F32)],
        scratch_shapes=[pltpu.VMEM((LANES, d), F32), pltpu.VMEM((LANES, d), F32),
                        pltpu.VMEM((LANES, LANES), F32), pltpu.VMEM((LANES, LANES), F32),
                        pltpu.VMEM((bt, d), F32)],
        compiler_params=_cparams("arbitrary"),
        name="ssd_sample",
    )(p2, p2, p2, dtr2, conv_state2d, conv_state2d, conv_state2d, conv_state2d, conv_state2d,
      conv_state2d, h0_all, cw[:, :d], cw[:, d:], cb[:, :d], cb[:, d:], dtb, alog, dexp, ng)
    return y, ht


def _t5_bucket_np(dist):
    d = np.asarray(dist, np.int32)
    max_exact = N_BUCKETS // 2
    df = np.maximum(d, 1).astype(np.float32)
    large = max_exact + (np.log(df / np.float32(max_exact)) / np.float32(math.log(MAX_DISTANCE / max_exact))
                         * np.float32(N_BUCKETS - max_exact)).astype(np.int32)
    large = np.minimum(large, N_BUCKETS - 1)
    return np.where(d < max_exact, d, large).astype(np.int32)


def _bias_index_tables():
    blk = ATT_BLOCK
    qi = np.arange(blk)[:, None]
    kj = np.arange(2 * blk)[None, :]
    sdist = qi + blk - kj
    idx_p, idx_d = [], []
    for window, dil in ATT_GROUPS:
        nk = window // dil
        band = (sdist >= 0) & (sdist <= nk)
        idx_p.append(np.where(band, _t5_bucket_np(np.clip(sdist, 0, nk) * dil), -1))
        steps = nk - np.arange(nk)
        idx_d.append(np.broadcast_to(_t5_bucket_np(steps * dil)[:, None], (nk, LANES)))
    return np.stack(idx_p).astype(np.int32), np.stack(idx_d).astype(np.int32)


def _bias_kernel(tab_ref, idxp_ref, idxd_ref, bp_ref, bd_ref):
    h = pl.program_id(0)
    ip = idxp_ref[...]
    idd = idxd_ref[...]
    accp = jnp.full(ip.shape, NEG_INF, F32)
    accd = jnp.full(idd.shape, NEG_INF, F32)
    for b in range(N_BUCKETS):
        v = tab_ref[b, h]
        accp = jnp.where(ip == b, v, accp)
        accd = jnp.where(idd == b, v, accd)
    bp_ref[...] = accp
    bd_ref[...] = accd


def _build_bias(rel_bias):
    idx_p, idx_d = _bias_index_tables()
    n_heads = rel_bias.shape[1]
    blk = ATT_BLOCK
    nk = idx_d.shape[1]
    return pl.pallas_call(
        _bias_kernel,
        grid=(n_heads,),
        in_specs=[pl.BlockSpec(memory_space=pltpu.SMEM),
                  pl.BlockSpec((None, blk, 2 * blk), lambda h: (h // ATT_HPG, 0, 0)),
                  pl.BlockSpec((None, nk, LANES), lambda h: (h // ATT_HPG, 0, 0))],
        out_specs=[pl.BlockSpec((None, blk, 2 * blk), lambda h: (h, 0, 0)),
                   pl.BlockSpec((None, nk, LANES), lambda h: (h, 0, 0))],
        out_shape=[jax.ShapeDtypeStruct((n_heads, blk, 2 * blk), F32),
                   jax.ShapeDtypeStruct((n_heads, nk, LANES), F32)],
        compiler_params=_cparams("arbitrary"),
        name="t5_bias",
    )(rel_bias, jnp.asarray(idx_p), jnp.asarray(idx_d))


ATT_TILE = ATT_BLOCK * max(dil for _, dil in ATT_GROUPS)
MERGE_ROWS = 256


def _merge_three(parts):
    m_all = jnp.maximum(jnp.maximum(parts[0][1], parts[1][1]), parts[2][1])
    w = jnp.exp(parts[0][1] - m_all)
    num = w * parts[0][0]
    den = w * parts[0][2]
    for o, m, l in parts[1:]:
        w = jnp.exp(m - m_all)
        num = num + w * o
        den = den + w * l
    return num / den


def _attn_prompt_kernel(*refs):
    ng = len(ATT_GROUPS)
    blk = ATT_BLOCK
    bias_ref = refs[5 * ng]
    y_ref = refs[5 * ng + 1]
    scr = refs[5 * ng + 2:]
    first_tile = pl.program_id(1) == 0

    def rows_of(start, dil):
        return pl.ds(start, blk) if dil == 1 else pl.ds(start, blk, stride=dil)

    for gi, (window, dil) in enumerate(ATT_GROUPS):
        q_ref, kc_ref, vc_ref, kp_ref, vp_ref = refs[5 * gi:5 * gi + 5]
        o_s, m_s, l_s = scr[3 * gi:3 * gi + 3]
        span = blk * dil
        bias = bias_ref[gi]
        for j in range(ATT_TILE // span):
            for r in range(dil):
                rows = rows_of(j * span + r, dil)
                q = q_ref[rows, :]
                if j == 0:
                    prows = rows_of(r, dil)
                    s_prev = _bdot_nt(q, kp_ref[prows, :]) * ATT_SCALE + bias[:, :blk]
                    s_prev = jnp.where(first_tile, NEG_INF, s_prev)
                    v_prev = vp_ref[prows, :]
                else:
                    prows = rows_of((j - 1) * span + r, dil)
                    s_prev = _bdot_nt(q, kc_ref[prows, :]) * ATT_SCALE + bias[:, :blk]
                    v_prev = vc_ref[prows, :]
                s_cur = _bdot_nt(q, kc_ref[rows, :]) * ATT_SCALE + bias[:, blk:]
                m = jnp.maximum(jnp.max(s_prev, axis=-1, keepdims=True), jnp.max(s_cur, axis=-1, keepdims=True))
                p_prev = jnp.exp(s_prev - m)
                p_cur = jnp.exp(s_cur - m)
                l = jnp.sum(p_prev, axis=-1, keepdims=True) + jnp.sum(p_cur, axis=-1, keepdims=True)
                o_s[rows, :] = _bdot(p_prev, v_prev) + _bdot(p_cur, vc_ref[rows, :])
                m_s[rows, :] = jnp.broadcast_to(m, (blk, ATT_HEAD_DIM))
                l_s[rows, :] = jnp.broadcast_to(l, (blk, ATT_HEAD_DIM))

    for c in range(ATT_TILE // MERGE_ROWS):
        rs = slice(c * MERGE_ROWS, (c + 1) * MERGE_ROWS)
        parts = [(scr[3 * gi][rs, :], scr[3 * gi + 1][rs, :], scr[3 * gi + 2][rs, :]) for gi in range(ng)]
        y_ref[rs, :] = _merge_three(parts).astype(y_ref.dtype)


def _attn_prompt(qkv, bias_p, bsz, s):
    hdim = ATT_HEAD_DIM
    ng = len(ATT_GROUPS)
    n_heads = qkv.shape[1] // (3 * hdim)
    tile = ATT_TILE
    assert s % tile == 0
    nt = s // tile
    in_specs, args = [], []
    for gi, (window, dil) in enumerate(ATT_GROUPS):
        span = ATT_BLOCK * dil
        per = tile // span

        def cur(part, gi=gi):
            return pl.BlockSpec((tile, hdim), lambda b, i, h: (b * nt + i, part * n_heads + gi * ATT_HPG + h))

        def prev(part, gi=gi, per=per):
            return pl.BlockSpec((span, hdim),
                                lambda b, i, h: (jnp.maximum((b * nt + i) * per - 1, 0),
                                                 part * n_heads + gi * ATT_HPG + h))

        in_specs += [cur(0), cur(1), cur(2), prev(1), prev(2)]
        args += [qkv] * 5
    in_specs.append(pl.BlockSpec((ng, None, ATT_BLOCK, 2 * ATT_BLOCK), lambda b, i, h: (0, h, 0, 0)))
    args.append(bias_p)
    return pl.pallas_call(
        _attn_prompt_kernel,
        grid=(bsz, nt, ATT_HPG),
        in_specs=in_specs,
        out_specs=pl.BlockSpec((tile, hdim), lambda b, i, h: (b * nt + i, h)),
        out_shape=jax.ShapeDtypeStruct((bsz * s, ATT_HPG * hdim), BF16),
        scratch_shapes=[pltpu.VMEM((tile, hdim), F32)] * (3 * ng),
        compiler_params=_cparams("parallel", "parallel", "arbitrary"),
        name="attn_prompt",
    )(*args)


def _attn_sample_kernel(qkv_ref, c0_ref, c1_ref, c2_ref, bias_ref, bias0_ref, y_ref):
    ng = len(ATT_GROUPS)
    caches = (c0_ref, c1_ref, c2_ref)

    def body(b, carry):
        parts = []
        for gi in range(ng):
            q = qkv_ref[b, gi]
            k_new = qkv_ref[b, ng + gi]
            v_new = qkv_ref[b, 2 * ng + gi]
            kc = caches[gi][b, :, 0]
            vc = caches[gi][b, :, 1]
            s_past = jnp.sum(kc * q[None], axis=-1, keepdims=True) * ATT_SCALE + bias_ref[gi]
            s_self = jnp.sum(k_new * q, axis=-1, keepdims=True) * ATT_SCALE + bias0_ref[gi]
            m = jnp.maximum(jnp.max(s_past, axis=0), s_self)
            p_past = jnp.exp(s_past - m[None])
            p_self = jnp.exp(s_self - m)
            l = jnp.sum(p_past, axis=0) + p_self
            o = jnp.sum(p_past * vc, axis=0) + p_self * v_new
            parts.append((o, m, l))
        y_ref[b] = _merge_three(parts)
        return carry

    lax.fori_loop(0, DEC_TILE, body, 0)


def _attn_sample(qkv2, caches, layer, bias_d, bias0):
    db = qkv2.shape[0]
    bt = DEC_TILE
    hdim = ATT_HEAD_DIM
    ng = len(ATT_GROUPS)
    views, specs = [], []
    for (window, dil), cache in zip(ATT_GROUPS, caches):
        nk = window // dil
        depth, _, wb = cache.shape[:3]
        assert wb == window, "cached window must equal the group's window"
        views.append(cache.reshape(depth, db, nk, dil, 2, ATT_HPG, hdim))
        specs.append(pl.BlockSpec((None, bt, nk, None, 2, ATT_HPG, hdim),
                                  lambda i: (layer, i, 0, 0, 0, 0, 0)))
    y = pl.pallas_call(
        _attn_sample_kernel,
        grid=(db // bt,),
        in_specs=[pl.BlockSpec((bt, 3 * ng, ATT_HPG, hdim), lambda i: (i, 0, 0, 0))] + specs +
                 [pl.BlockSpec(bias_d.shape, lambda i: (0, 0, 0, 0)),
                  pl.BlockSpec(bias0.shape, lambda i: (0, 0, 0))],
        out_specs=pl.BlockSpec((bt, ATT_HPG, hdim), lambda i: (i, 0, 0)),
        out_shape=jax.ShapeDtypeStruct((db, ATT_HPG, hdim), F32),
        compiler_params=_cparams("parallel"),
        name="attn_sample",
    )(qkv2.reshape(db, 3 * ng, ATT_HPG, hdim), *views, bias_d, bias0)
    return y.reshape(db, ATT_HPG * hdim)


def _branch_kernel(ysc_ref, yssm_ref, yatt_ref, wsc_ref, wssm_ref, watt_ref, g0_ref, g1_ref, g2_ref, o_ref):
    a = jnp.dot(ysc_ref[...], wsc_ref[...], preferred_element_type=F32)
    b = jnp.dot(yssm_ref[...], wssm_ref[...], preferred_element_type=F32)
    c = jnp.dot(yatt_ref[...], watt_ref[...], preferred_element_type=F32)
    o = _sigmoid(g0_ref[...]) * a + _sigmoid(g1_ref[...]) * b + _sigmoid(g2_ref[...]) * c
    o_ref[...] = o.astype(o_ref.dtype)


def _branch_merge(ysc, yssm, yatt, wsc, wssm, watt, p2, tm, tn):
    m, d = ysc.shape
    nj = d // tn
    rows = lambda width: pl.BlockSpec((tm, width), lambda i, j: (i, 0))
    wcol = lambda k: pl.BlockSpec((k, tn), lambda i, j: (0, j))
    gate = lambda br: pl.BlockSpec((tm, tn), lambda i, j: (i, br * nj + j))
    return pl.pallas_call(
        _branch_kernel,
        grid=(m // tm, nj),
        in_specs=[rows(d), rows(d), rows(yatt.shape[1]), wcol(d), wcol(d), wcol(yatt.shape[1]),
                  gate(0), gate(1), gate(2)],
        out_specs=pl.BlockSpec((tm, tn), lambda i, j: (i, j)),
        out_shape=jax.ShapeDtypeStruct((m, d), BF16),
        compiler_params=_cparams("parallel", "arbitrary"),
        name="branch_merge",
    )(ysc, yssm, yatt, wsc, wssm, watt, p2, p2, p2)


def _outproj_kernel(x_ref, mg_ref, w_ref, g_ref, x1_ref, h2_ref):
    x1 = x_ref[...] + jnp.dot(mg_ref[...], w_ref[...], preferred_element_type=F32)
    x1_ref[...] = x1
    ms = jnp.mean(x1 * x1, axis=-1, keepdims=True)
    h2_ref[...] = (x1 * lax.rsqrt(ms + EPS) * g_ref[...]).astype(h2_ref.dtype)


def _outproj(x2d, merged, w_out, g2, tm):
    m, d = x2d.shape
    rows = pl.BlockSpec((tm, d), lambda i: (i, 0))
    return pl.pallas_call(
        _outproj_kernel,
        grid=(m // tm,),
        in_specs=[rows, rows, pl.BlockSpec((d, d), lambda i: (0, 0)), pl.BlockSpec((1, d), lambda i: (0, 0))],
        out_specs=[rows, rows],
        out_shape=[jax.ShapeDtypeStruct((m, d), F32), jax.ShapeDtypeStruct((m, d), BF16)],
        compiler_params=_cparams("parallel"),
        name="outproj_norm",
    )(x2d, merged, w_out, g2.reshape(1, d))


def _mlp_kernel(x1_ref, h2_ref, wu_ref, wd_ref, o_ref):
    @pl.when(pl.program_id(1) == 0)
    def _():
        o_ref[...] = x1_ref[...]

    u = jnp.dot(h2_ref[...], wu_ref[...], preferred_element_type=F32)
    u = jnp.square(jnp.maximum(u, 0.0)).astype(BF16)
    o_ref[...] += jnp.dot(u, wd_ref[...], preferred_element_type=F32)


def _mlp(x1, h2, w_up, w_down, tm, tf):
    m, d = x1.shape
    f = w_up.shape[1]
    rows = pl.BlockSpec((tm, d), lambda i, j: (i, 0))
    return pl.pallas_call(
        _mlp_kernel,
        grid=(m // tm, f // tf),
        in_specs=[rows, rows, pl.BlockSpec((d, tf), lambda i, j: (0, j)),
                  pl.BlockSpec((tf, d), lambda i, j: (j, 0))],
        out_specs=rows,
        out_shape=jax.ShapeDtypeStruct((m, d), F32),
        compiler_params=_cparams("parallel", "arbitrary"),
        name="mlp",
    )(x1, h2, w_up, w_down)


def _row_tile(m, pref):
    return pref if m % pref == 0 else m


def _pad_lanes(v):
    return jnp.pad(v.astype(F32), (0, LANES - v.shape[0])).reshape(1, LANES)


def _prep_layer(l, norm1_g, w_in, sc_conv_w, ssm_conv_w, ssm_conv_b, ssm_dt_bias, ssm_A_log, ssm_D,
                ssm_norm_g, q_norm_g, k_norm_g, w_br_sc, w_br_ssm, w_br_att, w_out, norm2_g, w_up, w_down):
    d = w_in.shape[1]
    heads = d // SSM_HEAD_DIM
    aw = ATT_HPG * len(ATT_GROUPS) * ATT_HEAD_DIM
    n_main = N_BRANCH * d + 3 * d + d + (d + 2 * SSM_GROUPS * SSM_STATE)
    wl = w_in[l]
    w_dt = jnp.pad(wl[:, n_main:n_main + heads], ((0, 0), (0, LANES - heads)))
    n_att_heads = aw // ATT_HEAD_DIM
    gains = jnp.concatenate([jnp.tile(q_norm_g[l], n_att_heads), jnp.tile(k_norm_g[l], n_att_heads),
                             jnp.ones((aw,), F32)]).reshape(1, 3 * aw)
    return dict(
        layer=l, w_in_all=w_in, n_main=n_main,
        norm1_g=norm1_g[l], norm2_g=norm2_g[l], w_dt=w_dt[None],
        w_qkv=wl[:, n_main + heads:], qk_gains=gains,
        sc_conv_w=sc_conv_w[l], ssm_conv_w=ssm_conv_w[l], ssm_conv_b=ssm_conv_b[l].reshape(1, -1),
        dt_bias=_pad_lanes(ssm_dt_bias[l]), a_log=_pad_lanes(ssm_A_log[l]),
        d_exp=jnp.repeat(ssm_D[l].astype(F32), SSM_HEAD_DIM).reshape(1, d),
        ssm_norm_g=ssm_norm_g[l].reshape(1, d),
        w_br_sc=w_br_sc[l].astype(BF16), w_br_ssm=w_br_ssm[l].astype(BF16), w_br_att=w_br_att[l].astype(BF16),
        w_out=w_out[l].astype(BF16), w_up=w_up[l].astype(BF16), w_down=w_down[l].astype(BF16),
    )


def _in_proj(x2d, w):
    m = x2d.shape[0]
    tm = _row_tile(m, 1024)
    h = _rmsnorm(x2d, w["norm1_g"], _row_tile(m, 512))
    p = _matmul_wcast(h, w["w_in_all"], w["layer"], w["n_main"], tm, 1024, name="in_proj_main")
    dtr = _matmul_wcast(h, w["w_dt"], 0, LANES, tm, LANES, name="in_proj_dt")
    qkv = _matmul_qkv(h, w["w_qkv"], w["qk_gains"], tm, ATT_HPG * ATT_HEAD_DIM)
    return p, dtr, qkv


def _tail(x2d, p, ysc, yssm, yatt, w):
    m = x2d.shape[0]
    merged = _branch_merge(ysc, yssm, yatt, w["w_br_sc"], w["w_br_ssm"], w["w_br_att"], p,
                           _row_tile(m, 1024), 512)
    x1, h2 = _outproj(x2d, merged, w["w_out"], w["norm2_g"], _row_tile(m, 512))
    return _mlp(x1, h2, w["w_up"], w["w_down"], _row_tile(m, 512), 1024)


def _kv_rows(qkv3, gi, keep):
    aw = qkv3.shape[2] // 3
    gw = ATT_HPG * ATT_HEAD_DIM
    bsz, ln, _ = qkv3.shape
    k = qkv3[:, ln - keep:, aw + gi * gw:aw + (gi + 1) * gw].reshape(bsz, keep, ATT_HPG, ATT_HEAD_DIM)
    v = qkv3[:, ln - keep:, 2 * aw + gi * gw:2 * aw + (gi + 1) * gw].reshape(bsz, keep, ATT_HPG, ATT_HEAD_DIM)
    return jnp.stack([k, v], axis=2)


def _prompt_block(x3, w, bias_p):
    bsz, s, d = x3.shape
    x2d = x3.reshape(bsz * s, d)
    p, dtr, qkv = _in_proj(x2d, w)
    p3 = p.reshape(bsz, s, -1)
    qkv3 = qkv.reshape(bsz, s, -1)
    ysc, sc_new = _sc_prompt(p3, w["sc_conv_w"], d, 512, 512)
    yssm, ht = _ssd_prompt(p3, dtr.reshape(bsz, s, LANES), w["ssm_conv_w"], w["ssm_conv_b"], w["dt_bias"],
                           w["a_log"], w["d_exp"], w["ssm_norm_g"], d)
    xbc_off = N_BRANCH * d + 3 * d + d
    ssm_conv_new = p3[:, s - (SSM_CONV - 1):, xbc_off:]
    kv_new = [_kv_rows(qkv3, gi, min(window, s)) for gi, (window, _) in enumerate(ATT_GROUPS)]
    yatt = _attn_prompt(qkv, bias_p, bsz, s)
    x_out = _tail(x2d, p, ysc.reshape(bsz * s, d), yssm.reshape(bsz * s, d), yatt, w)
    heads = d // SSM_HEAD_DIM
    st = (sc_new, ssm_conv_new, ht.reshape(bsz, heads, SSM_HEAD_DIM, SSM_STATE), kv_new[0], kv_new[1], kv_new[2])
    return x_out.reshape(bsz, s, d), st


def _sample_block(x3, w, l, bias_d, bias0, sc_buf, ssm_buf, ssm_h0_all, kv_caches):
    db, t, d = x3.shape
    assert t == 1, "decode kernels handle one new token per sample"
    x2d = x3.reshape(db, d)
    p, dtr, qkv = _in_proj(x2d, w)
    ysc, u_new = _sc_sample(p, sc_buf.reshape(db, -1), w["sc_conv_w"], d, 512)
    sc_new = jnp.concatenate([sc_buf[:, 1:], u_new[:, None, :]], axis=1)
    heads = d // SSM_HEAD_DIM
    depth = ssm_h0_all.shape[0]
    yssm, ht = _ssd_sample(p, dtr, ssm_buf.reshape(db, -1),
                           ssm_h0_all.reshape(depth, db, heads * SSM_HEAD_DIM, SSM_STATE), l,
                           w["ssm_conv_w"], w["ssm_conv_b"], w["dt_bias"], w["a_log"], w["d_exp"],
                           w["ssm_norm_g"], d)
    xbc_off = N_BRANCH * d + 3 * d + d
    ssm_conv_new = jnp.concatenate([ssm_buf[:, 1:], p[:, None, xbc_off:]], axis=1)
    yatt = _attn_sample(qkv, kv_caches, l, bias_d, bias0)
    qkv3 = qkv.reshape(db, 1, -1)
    kv_new = [_kv_rows(qkv3, gi, 1) for gi in range(len(ATT_GROUPS))]
    x_out = _tail(x2d, p, ysc.astype(BF16), yssm.astype(BF16), yatt.astype(BF16), w)
    st = (sc_new, ssm_conv_new, ht.reshape(db, heads, SSM_HEAD_DIM, SSM_STATE), kv_new[0], kv_new[1], kv_new[2])
    return x_out.reshape(db, 1, d), st


def kernel(x_prompt, x_sample, state_sc_conv, state_ssm_conv, state_ssm, cache_kv_w128, cache_kv_w512,
           cache_kv_w2048, norm1_g, w_in, sc_conv_w, ssm_conv_w, ssm_conv_b, ssm_dt_bias, ssm_A_log, ssm_D,
           ssm_norm_g, q_norm_g, k_norm_g, rel_bias, w_br_sc, w_br_ssm, w_br_att, w_out, norm2_g, w_up, w_down):
    depth = w_in.shape[0]
    ng = len(ATT_GROUPS)
    bias_p, bias_d = _build_bias(rel_bias)
    bias_p = bias_p.reshape((ng, ATT_HPG) + bias_p.shape[1:])
    bias_d = bias_d.reshape((ng, ATT_HPG) + bias_d.shape[1:]).transpose(0, 2, 1, 3)
    bias0 = jnp.broadcast_to(rel_bias[0].reshape(ng, ATT_HPG, 1), (ng, ATT_HPG, LANES))
    xp, xs = x_prompt, x_sample
    p_new, s_new = [], []
    for l in range(depth):
        w = _prep_layer(l, norm1_g, w_in, sc_conv_w, ssm_conv_w, ssm_conv_b, ssm_dt_bias, ssm_A_log, ssm_D,
                        ssm_norm_g, q_norm_g, k_norm_g, w_br_sc, w_br_ssm, w_br_att, w_out, norm2_g, w_up,
                        w_down)
        xp, st = _prompt_block(xp, w, bias_p)
        p_new.append(st)
        xs, st = _sample_block(xs, w, l, bias_d, bias0, state_sc_conv[l], state_ssm_conv[l], state_ssm,
                               (cache_kv_w128, cache_kv_w512, cache_kv_w2048))
        s_new.append(st)
    p_out = [jnp.stack(a) for a in zip(*p_new)]
    s_out = [jnp.stack(a) for a in zip(*s_new)]
    return (xp, xs, *p_out, *s_out)
```

```python
import functools
import math

import numpy as np
import jax
import jax.numpy as jnp
from jax import lax
from jax.experimental import pallas as pl
from jax.experimental.pallas import tpu as pltpu

F32 = jnp.float32
BF16 = jnp.bfloat16

N_BRANCH = 3
SC_CONV = 3
SSM_HEAD_DIM = 64
SSM_GROUPS = 8
SSM_STATE = 128
SSM_CONV = 4
SSM_CHUNK = 128
ATT_HEAD_DIM = 128
ATT_GROUPS = ((128, 1), (512, 4), (2048, 16))
ATT_HPG = 4
ATT_BLOCK = 128
ATT_SCALE = 1.0 / math.sqrt(ATT_HEAD_DIM)
N_BUCKETS = 32
MAX_DISTANCE = 2048
EPS = 1e-6

LANES = 128
SUBLANES = 8
VMEM_LIMIT_BYTES = 56 * 1024 * 1024

DEC_TILE = 8
NEG_INF = float("-inf")


def _cparams(*sem):
    return pltpu.CompilerParams(dimension_semantics=sem, vmem_limit_bytes=VMEM_LIMIT_BYTES)


def _sigmoid(v):
    return 1.0 / (1.0 + jnp.exp(-v))


def _silu(v):
    return v * _sigmoid(v)


def _bdot(a, b):
    return jnp.dot(a.astype(BF16), b.astype(BF16), preferred_element_type=F32)


def _bdot_nt(a, b):
    return lax.dot_general(a.astype(BF16), b.astype(BF16), (((1,), (1,)), ((), ())),
                           preferred_element_type=F32)


def _rmsnorm_kernel(x_ref, g_ref, o_ref):
    x = x_ref[...]
    ms = jnp.mean(x * x, axis=-1, keepdims=True)
    o_ref[...] = (x * lax.rsqrt(ms + EPS) * g_ref[...]).astype(o_ref.dtype)


def _rmsnorm(x2d, g, tm):
    m, d = x2d.shape
    return pl.pallas_call(
        _rmsnorm_kernel,
        grid=(m // tm,),
        in_specs=[pl.BlockSpec((tm, d), lambda i: (i, 0)),
                  pl.BlockSpec((1, d), lambda i: (0, 0))],
        out_specs=pl.BlockSpec((tm, d), lambda i: (i, 0)),
        out_shape=jax.ShapeDtypeStruct((m, d), BF16),
        compiler_params=_cparams("parallel"),
        name="rmsnorm",
    )(x2d, g.reshape(1, d))


def _mm_wcast_kernel(a_ref, w_ref, o_ref, wb_ref):
    @pl.when(pl.program_id(1) == 0)
    def _():
        wb_ref[...] = w_ref[...].astype(BF16)

    o_ref[...] = _bdot_nt(a_ref[...], wb_ref[...])


def _matmul_wcast(a, wt_all, layer, first_block, n_cols, tm, tn, name):
    m, k = a.shape
    return pl.pallas_call(
        _mm_wcast_kernel,
        grid=(n_cols // tn, m // tm),
        in_specs=[pl.BlockSpec((tm, k), lambda j, i: (i, 0)),
                  pl.BlockSpec((None, tn, k), lambda j, i: (layer, first_block + j, 0))],
        out_specs=pl.BlockSpec((tm, tn), lambda j, i: (i, j)),
        out_shape=jax.ShapeDtypeStruct((m, n_cols), F32),
        scratch_shapes=[pltpu.VMEM((tn, k), BF16)],
        compiler_params=_cparams("parallel", "arbitrary"),
        name=name,
    )(a, wt_all)


def _mm_qkv_kernel(a_ref, w_ref, g_ref, o_ref, wb_ref, *, n_norm_blocks, heads_per_block):
    @pl.when(pl.program_id(1) == 0)
    def _():
        wb_ref[...] = w_ref[...].astype(BF16)

    acc = _bdot_nt(a_ref[...], wb_ref[...])
    j = pl.program_id(0)

    @pl.when(j < n_norm_blocks)
    def _():
        for h in range(heads_per_block):
            sl = slice(h * ATT_HEAD_DIM, (h + 1) * ATT_HEAD_DIM)
            t = acc[:, sl]
            ms = jnp.mean(t * t, axis=-1, keepdims=True)
            o_ref[:, sl] = t * lax.rsqrt(ms + EPS) * g_ref[:, sl]

    @pl.when(j >= n_norm_blocks)
    def _():
        o_ref[...] = acc


def _matmul_qkv(a, wt, gains, tm, tn):
    m, k = a.shape
    n = wt.shape[0]
    att_width = n // 3
    return pl.pallas_call(
        functools.partial(_mm_qkv_kernel, n_norm_blocks=2 * att_width // tn,
                          heads_per_block=tn // ATT_HEAD_DIM),
        grid=(n // tn, m // tm),
        in_specs=[pl.BlockSpec((tm, k), lambda j, i: (i, 0)),
                  pl.BlockSpec((tn, k), lambda j, i: (j, 0)),
                  pl.BlockSpec((1, tn), lambda j, i: (0, j))],
        out_specs=pl.BlockSpec((tm, tn), lambda j, i: (i, j)),
        out_shape=jax.ShapeDtypeStruct((m, n), F32),
        scratch_shapes=[pltpu.VMEM((tn, k), BF16)],
        compiler_params=_cparams("parallel", "arbitrary"),
        name="matmul_qkv",
    )(a, wt, gains)


def _sc_prompt_kernel(b_ref, c_ref, x_ref, hc_ref, hx_ref, w_ref, y_ref, tail_ref, ext_ref, *, ts):
    t = pl.program_id(2)
    u = c_ref[...] * x_ref[...]
    hu = hc_ref[...] * hx_ref[...]
    ext_ref[0:SUBLANES, :] = jnp.where(t > 0, hu, 0.0)
    ext_ref[SUBLANES:SUBLANES + ts, :] = u
    u2 = ext_ref[pl.ds(SUBLANES - 2, ts), :]
    u1 = ext_ref[pl.ds(SUBLANES - 1, ts), :]
    y = u2 * w_ref[0:1, :] + u1 * w_ref[1:2, :] + u * w_ref[2:3, :]
    y_ref[...] = (b_ref[...] * y).astype(y_ref.dtype)
    tail_ref[...] = ext_ref[pl.ds(ts, SUBLANES), :]


def _sc_prompt(p3, conv_w, d, ts, tc):
    bsz, s, _ = p3.shape
    ncb = d // tc
    off_b, off_c, off_x = 3 * ncb, 4 * ncb, 5 * ncb
    hb = ts // SUBLANES

    def halo(off):
        return pl.BlockSpec((None, SUBLANES, tc),
                            lambda b, c, t: (b, jnp.maximum(t * hb - 1, 0), off + c))

    def main(off):
        return pl.BlockSpec((None, ts, tc), lambda b, c, t: (b, t, off + c))

    y, tail = pl.pallas_call(
        functools.partial(_sc_prompt_kernel, ts=ts),
        grid=(bsz, ncb, s // ts),
        in_specs=[main(off_b), main(off_c), main(off_x), halo(off_c), halo(off_x),
                  pl.BlockSpec((SC_CONV, tc), lambda b, c, t: (0, c))],
        out_specs=[pl.BlockSpec((None, ts, tc), lambda b, c, t: (b, t, c)),
                   pl.BlockSpec((None, SUBLANES, tc), lambda b, c, t: (b, 0, c))],
        out_shape=[jax.ShapeDtypeStruct((bsz, s, d), BF16),
                   jax.ShapeDtypeStruct((bsz, SUBLANES, d), F32)],
        scratch_shapes=[pltpu.VMEM((ts + SUBLANES, tc), F32)],
        compiler_params=_cparams("parallel", "parallel", "arbitrary"),
        name="short_conv_prompt",
    )(p3, p3, p3, p3, p3, conv_w)
    return y, tail[:, SUBLANES - (SC_CONV - 1):, :]


def _sc_sample_kernel(b_ref, c_ref, x_ref, s0_ref, s1_ref, w_ref, y_ref, u_ref):
    u = c_ref[...] * x_ref[...]
    y = s0_ref[...] * w_ref[0:1, :] + s1_ref[...] * w_ref[1:2, :] + u * w_ref[2:3, :]
    y_ref[...] = b_ref[...] * y
    u_ref[...] = u


def _sc_sample(p2, state2d, conv_w, d, tc):
    db = p2.shape[0]
    ncb = d // tc

    def col(off):
        return pl.BlockSpec((db, tc), lambda c: (0, off + c))

    return pl.pallas_call(
        _sc_sample_kernel,
        grid=(ncb,),
        in_specs=[col(3 * ncb), col(4 * ncb), col(5 * ncb), col(0), col(ncb),
                  pl.BlockSpec((SC_CONV, tc), lambda c: (0, c))],
        out_specs=[col(0), col(0)],
        out_shape=[jax.ShapeDtypeStruct((db, d), F32), jax.ShapeDtypeStruct((db, d), F32)],
        compiler_params=_cparams("parallel"),
        name="short_conv_sample",
    )(p2, p2, p2, state2d, state2d, conv_w)


def _softplus(v):
    return jnp.maximum(v, 0.0) + jnp.log1p(jnp.exp(-jnp.abs(v)))


def _gated_group_norm(y, xs, z, dexp, ng):
    val = (y + dexp * xs) * _silu(z)
    ms = jnp.mean(val * val, axis=-1, keepdims=True)
    return val * lax.rsqrt(ms + EPS) * ng


def _ssd_prompt_kernel(x_ref, bc_ref, z_ref, dtr_ref, cwx_ref, cwbc_ref, cbx_ref, cbbc_ref, dtb_ref,
                       alog_ref, dexp_ref, ng_ref, y_ref, ht_ref, extx, extbc, xs_s, bc_s, state):
    q = SSM_CHUNK
    c = pl.program_id(1)
    pad = SUBLANES
    taps = SSM_CONV

    @pl.when(c == 0)
    def _():
        extx[0:pad, :] = jnp.zeros((pad, extx.shape[1]), F32)
        extbc[0:pad, :] = jnp.zeros((pad, extbc.shape[1]), F32)
        state[...] = jnp.zeros(state.shape, F32)

    extx[pad:pad + q, :] = x_ref[...]
    extbc[pad:pad + q, :] = bc_ref[...]

    def conv(ext, w_ref, b_ref):
        acc = ext[pl.ds(pad - taps + 1, q), :] * w_ref[0:1, :]
        for i in range(1, taps):
            acc = acc + ext[pl.ds(pad - taps + 1 + i, q), :] * w_ref[i:i + 1, :]
        return _silu(acc + b_ref[...])

    xs_s[...] = conv(extx, cwx_ref, cbx_ref)
    bc_s[...] = conv(extbc, cwbc_ref, cbbc_ref)
    extx[0:pad, :] = extx[q:q + pad, :]
    extbc[0:pad, :] = extbc[q:q + pad, :]

    dt = _softplus(dtr_ref[...] + dtb_ref[...])
    a_neg = -jnp.exp(alog_ref[...])
    ii = lax.broadcasted_iota(jnp.int32, (q, q), 0)
    jj = lax.broadcasted_iota(jnp.int32, (q, q), 1)
    causal = ii >= jj
    acum = jnp.dot(causal.astype(F32), dt * a_neg, precision=lax.Precision.HIGHEST,
                   preferred_element_type=F32)
    acum_t = acum.T
    dt_t = dt.T
    lo_half = jj < SSM_HEAD_DIM
    gs = SSM_STATE
    hd = SSM_HEAD_DIM
    hpg = xs_s.shape[1] // (SSM_GROUPS * hd)
    gw = hpg * hd
    n_bc = SSM_GROUPS * gs

    for g in range(SSM_GROUPS):
        bg = bc_s[:, g * gs:(g + 1) * gs].astype(BF16)
        cg = bc_s[:, n_bc + g * gs:n_bc + (g + 1) * gs].astype(BF16)
        cb = _bdot_nt(cg, bg)
        ys = []
        for hp in range(hpg // 2):
            e0 = g * hpg + 2 * hp
            rows = slice(e0 * hd, (e0 + 2) * hd)
            xp = xs_s[:, rows]
            xp16 = xp.astype(BF16)
            cols_a, cols_dt, parts = [], [], []
            for k in range(2):
                e = e0 + k
                col_a = jnp.broadcast_to(acum[:, e:e + 1], (q, q))
                cols_a.append(col_a)
                cols_dt.append(jnp.broadcast_to(dt[:, e:e + 1], (q, q)))
                dec = jnp.exp(jnp.where(causal, col_a - acum_t[e:e + 1, :], NEG_INF))
                wm = cb * dec * dt_t[e:e + 1, :]
                parts.append(jnp.dot(wm.astype(BF16), xp16, preferred_element_type=F32))
            y_diag = jnp.where(lo_half, parts[0], parts[1])
            sel_a = jnp.where(lo_half, cols_a[0], cols_a[1])
            sel_dt = jnp.where(lo_half, cols_dt[0], cols_dt[1])
            hpair = state[rows, :]
            y_off = _bdot_nt(cg, hpair) * jnp.exp(sel_a)
            ys.append(y_diag + y_off)
            last_a = sel_a[q - 1:q, :]
            xw = xp * jnp.exp(last_a - sel_a) * sel_dt
            st = jnp.dot(xw.T.astype(BF16), bg, preferred_element_type=F32)
            cd = jnp.concatenate([jnp.broadcast_to(cols_a[0][q - 1:q, :], (hd, gs)),
                                  jnp.broadcast_to(cols_a[1][q - 1:q, :], (hd, gs))], axis=0)
            state[rows, :] = hpair * jnp.exp(cd) + st
        gsl = slice(g * gw, (g + 1) * gw)
        yg = jnp.concatenate(ys, axis=1)
        y_ref[:, gsl] = _gated_group_norm(yg, xs_s[:, gsl], z_ref[:, gsl], dexp_ref[:, gsl],
                                          ng_ref[:, gsl]).astype(y_ref.dtype)

    @pl.when(c == pl.num_programs(1) - 1)
    def _():
        ht_ref[...] = state[...]


def _ssd_prompt(p3, dtr3, cw, cb, dtb, alog, dexp, ng, d):
    bsz, s, _ = p3.shape
    q = SSM_CHUNK
    heads = d // SSM_HEAD_DIM
    const = lambda shape: pl.BlockSpec(shape, lambda b, c: (0, 0))
    y, ht = pl.pallas_call(
        _ssd_prompt_kernel,
        grid=(bsz, s // q),
        in_specs=[pl.BlockSpec((None, q, d), lambda b, c: (b, c, 7)),
                  pl.BlockSpec((None, q, d), lambda b, c: (b, c, 8)),
                  pl.BlockSpec((None, q, d), lambda b, c: (b, c, 6)),
                  pl.BlockSpec((None, q, LANES), lambda b, c: (b, c, 0)),
                  const((SSM_CONV, d)), const((SSM_CONV, d)), const((1, d)), const((1, d)),
                  const((1, LANES)), const((1, LANES)), const((1, d)), const((1, d))],
        out_specs=[pl.BlockSpec((None, q, d), lambda b, c: (b, c, 0)),
                   pl.BlockSpec((None, heads * SSM_HEAD_DIM, SSM_STATE), lambda b, c: (b, 0, 0))],
        out_shape=[jax.ShapeDtypeStruct((bsz, s, d), BF16),
                   jax.ShapeDtypeStruct((bsz, heads * SSM_HEAD_DIM, SSM_STATE), F32)],
        scratch_shapes=[pltpu.VMEM((q + SUBLANES, d), F32), pltpu.VMEM((q + SUBLANES, d), F32),
                        pltpu.VMEM((q, d), F32), pltpu.VMEM((q, d), F32),
                        pltpu.VMEM((heads * SSM_HEAD_DIM, SSM_STATE), F32)],
        compiler_params=_cparams("parallel", "arbitrary"),
        name="ssd_prompt",
    )(p3, p3, p3, dtr3, cw[:, :d], cw[:, d:], cb[:, :d], cb[:, d:], dtb, alog, dexp, ng)
    return y, ht


def _ssd_sample_kernel(x_ref, bc_ref, z_ref, dtr_ref, s0x, s0bc, s1x, s1bc, s2x, s2bc, h0_ref, cwx_ref,
                       cwbc_ref, cbx_ref, cbbc_ref, dtb_ref, alog_ref, dexp_ref, ng_ref,
                       y_ref, ht_ref, padx, padbc, padd, pada, y_s):
    bt = DEC_TILE
    hd = SSM_HEAD_DIM
    gs = SSM_STATE
    d = x_ref.shape[1]
    n_chunk = d // LANES
    hpg = d // (SSM_GROUPS * hd)
    gw = hpg * hd
    n_bc = SSM_GROUPS * gs

    @pl.when(pl.program_id(0) == 0)
    def _():
        padx[...] = jnp.zeros(padx.shape, F32)
        padbc[...] = jnp.zeros(padbc.shape, F32)
        padd[...] = jnp.zeros(padd.shape, F32)
        pada[...] = jnp.zeros(pada.shape, F32)

    def conv(s0, s1, s2, u, w_ref, b_ref):
        acc = s0[...] * w_ref[0:1, :] + s1[...] * w_ref[1:2, :] + s2[...] * w_ref[2:3, :] + u[...] * w_ref[3:4, :]
        return _silu(acc + b_ref[...])

    padx[0:bt, :] = conv(s0x, s1x, s2x, x_ref, cwx_ref, cbx_ref)
    padbc[0:bt, :] = conv(s0bc, s1bc, s2bc, bc_ref, cwbc_ref, cbbc_ref)
    dt = _softplus(dtr_ref[...] + dtb_ref[...])
    padd[0:bt, :] = dt
    pada[0:bt, :] = jnp.exp(dt * (-jnp.exp(alog_ref[...])))
    dt_t = padd[...].T
    da_t = pada[...].T
    wide = bt * gs
    row_w = lax.broadcasted_iota(jnp.int32, (LANES, wide), 0)
    blk_w = lax.broadcasted_iota(jnp.int32, (LANES, wide), 1) // gs
    row_t = lax.broadcasted_iota(jnp.int32, (wide, LANES), 0) // gs
    lane_t = lax.broadcasted_iota(jnp.int32, (wide, LANES), 1)

    bbig = cbig_t = None
    for ci in range(n_chunk):
        g = (2 * ci) // hpg
        if (2 * ci) % hpg == 0:
            bg = padbc[:, g * gs:(g + 1) * gs]
            bbig = jnp.where(row_w == blk_w, jnp.concatenate([bg] * bt, axis=1), 0.0).astype(BF16)
            cg_t = padbc[:, n_bc + g * gs:n_bc + (g + 1) * gs].T
            cbig_t = jnp.where(row_t == lane_t, jnp.concatenate([cg_t] * bt, axis=0), 0.0).astype(BF16)
        csl = slice(ci * LANES, (ci + 1) * LANES)
        x_t = padx[:, csl].T
        dt_rows = jnp.concatenate([jnp.broadcast_to(dt_t[2 * ci:2 * ci + 1, :], (hd, LANES)),
                                   jnp.broadcast_to(dt_t[2 * ci + 1:2 * ci + 2, :], (hd, LANES))], axis=0)
        inc = jnp.dot((x_t * dt_rows).astype(BF16), bbig, preferred_element_type=F32)
        hs = []
        for k in range(bt):
            da_rows = jnp.concatenate(
                [jnp.broadcast_to(da_t[2 * ci:2 * ci + 1, k:k + 1], (hd, gs)),
                 jnp.broadcast_to(da_t[2 * ci + 1:2 * ci + 2, k:k + 1], (hd, gs))], axis=0)
            hn = h0_ref[k, csl, :] * da_rows + inc[:, k * gs:(k + 1) * gs]
            ht_ref[k, csl, :] = hn
            hs.append(hn.astype(BF16))
        y_t = jnp.dot(jnp.concatenate(hs, axis=1), cbig_t, preferred_element_type=F32)
        y_s[:, csl] = y_t.T[0:bt, :]

    for g in range(SSM_GROUPS):
        gsl = slice(g * gw, (g + 1) * gw)
        y_ref[:, gsl] = _gated_group_norm(y_s[:, gsl], padx[0:bt, gsl], z_ref[:, gsl], dexp_ref[:, gsl],
                                          ng_ref[:, gsl])


def _ssd_sample(p2, dtr2, conv_state2d, h0_all, layer, cw, cb, dtb, alog, dexp, ng, d):
    db = p2.shape[0]
    bt = DEC_TILE
    hp = h0_all.shape[2]
    row = lambda off: pl.BlockSpec((bt, d), lambda i: (i, off))
    const = lambda shape: pl.BlockSpec(shape, lambda i: (0, 0))
    y, ht = pl.pallas_call(
        _ssd_sample_kernel,
        grid=(db // bt,),
        in_specs=[row(7), row(8), row(6), pl.BlockSpec((bt, LANES), lambda i: (i, 0)),
                  row(0), row(1), row(2), row(3), row(4), row(5),
                  pl.BlockSpec((None, bt, hp, SSM_STATE), lambda i: (layer, i, 0, 0)),
                  const((SSM_CONV, d)), const((SSM_CONV, d)), const((1, d)), const((1, d)),
                  const((1, LANES)), const((1, LANES)), const((1, d)), const((1, d))],
        out_specs=[pl.BlockSpec((bt, d), lambda i: (i, 0)),
                   pl.BlockSpec((bt, hp, SSM_STATE), lambda i: (i, 0, 0))],
        out_shape=[jax.ShapeDtypeStruct((db, d), F32),
                   jax.ShapeDtypeStruct((db, hp, SSM_STATE), F32)],
        scratch_shapes=[pltpu.VMEM((LANES, d), F32), pltpu.VMEM((LANES, d), F32),
                        pltpu.VMEM((LANES, LANES), F32), pltpu.VMEM((LANES, LANES), F32),
                        pltpu.VMEM((bt, d), F32)],
        compiler_params=_cparams("arbitrary"),
        name="ssd_sample",
    )(p2, p2, p2, dtr2, conv_state2d, conv_state2d, conv_state2d, conv_state2d, conv_state2d,
      conv_state2d, h0_all, cw[:, :d], cw[:, d:], cb[:, :d], cb[:, d:], dtb, alog, dexp, ng)
    return y, ht


def _t5_bucket_np(dist):
    d = np.asarray(dist, np.int32)
    max_exact = N_BUCKETS // 2
    df = np.maximum(d, 1).astype(np.float32)
    large = max_exact + (np.log(df / np.float32(max_exact)) / np.float32(math.log(MAX_DISTANCE / max_exact))
                         * np.float32(N_BUCKETS - max_exact)).astype(np.int32)
    large = np.minimum(large, N_BUCKETS - 1)
    return np.where(d < max_exact, d, large).astype(np.int32)


def _bias_index_tables():
    blk = ATT_BLOCK
    qi = np.arange(blk)[:, None]
    kj = np.arange(2 * blk)[None, :]
    sdist = qi + blk - kj
    idx_p, idx_d = [], []
    for window, dil in ATT_GROUPS:
        nk = window // dil
        band = (sdist >= 0) & (sdist <= nk)
        idx_p.append(np.where(band, _t5_bucket_np(np.clip(sdist, 0, nk) * dil), -1))
        steps = nk - np.arange(nk)
        idx_d.append(np.broadcast_to(_t5_bucket_np(steps * dil)[:, None], (nk, LANES)))
    return np.stack(idx_p).astype(np.int32), np.stack(idx_d).astype(np.int32)


def _bias_kernel(tab_ref, idxp_ref, idxd_ref, bp_ref, bd_ref):
    h = pl.program_id(0)
    ip = idxp_ref[...]
    idd = idxd_ref[...]
    accp = jnp.full(ip.shape, NEG_INF, F32)
    accd = jnp.full(idd.shape, NEG_INF, F32)
    for b in range(N_BUCKETS):
        v = tab_ref[b, h]
        accp = jnp.where(ip == b, v, accp)
        accd = jnp.where(idd == b, v, accd)
    bp_ref[...] = accp
    bd_ref[...] = accd


def _build_bias(rel_bias):
    idx_p, idx_d = _bias_index_tables()
    n_heads = rel_bias.shape[1]
    blk = ATT_BLOCK
    nk = idx_d.shape[1]
    return pl.pallas_call(
        _bias_kernel,
        grid=(n_heads,),
        in_specs=[pl.BlockSpec(memory_space=pltpu.SMEM),
                  pl.BlockSpec((None, blk, 2 * blk), lambda h: (h // ATT_HPG, 0, 0)),
                  pl.BlockSpec((None, nk, LANES), lambda h: (h // ATT_HPG, 0, 0))],
        out_specs=[pl.BlockSpec((None, blk, 2 * blk), lambda h: (h, 0, 0)),
                   pl.BlockSpec((None, nk, LANES), lambda h: (h, 0, 0))],
        out_shape=[jax.ShapeDtypeStruct((n_heads, blk, 2 * blk), F32),
                   jax.ShapeDtypeStruct((n_heads, nk, LANES), F32)],
        compiler_params=_cparams("arbitrary"),
        name="t5_bias",
    )(rel_bias, jnp.asarray(idx_p), jnp.asarray(idx_d))


ATT_TILE = ATT_BLOCK * max(dil for _, dil in ATT_GROUPS)
MERGE_ROWS = 256


def _merge_three(parts):
    m_all = jnp.maximum(jnp.maximum(parts[0][1], parts[1][1]), parts[2][1])
    w = jnp.exp(parts[0][1] - m_all)
    num = w * parts[0][0]
    den = w * parts[0][2]
    for o, m, l in parts[1:]:
        w = jnp.exp(m - m_all)
        num = num + w * o
        den = den + w * l
    return num / den


def _attn_prompt_kernel(*refs):
    ng = len(ATT_GROUPS)
    blk = ATT_BLOCK
    bias_ref = refs[5 * ng]
    y_ref = refs[5 * ng + 1]
    scr = refs[5 * ng + 2:]
    first_tile = pl.program_id(1) == 0

    def rows_of(start, dil):
        return pl.ds(start, blk) if dil == 1 else pl.ds(start, blk, stride=dil)

    for gi, (window, dil) in enumerate(ATT_GROUPS):
        q_ref, kc_ref, vc_ref, kp_ref, vp_ref = refs[5 * gi:5 * gi + 5]
        o_s, m_s, l_s = scr[3 * gi:3 * gi + 3]
        span = blk * dil
        bias = bias_ref[gi]
        for j in range(ATT_TILE // span):
            for r in range(dil):
                rows = rows_of(j * span + r, dil)
                q = q_ref[rows, :]
                if j == 0:
                    prows = rows_of(r, dil)
                    s_prev = _bdot_nt(q, kp_ref[prows, :]) * ATT_SCALE + bias[:, :blk]
                    s_prev = jnp.where(first_tile, NEG_INF, s_prev)
                    v_prev = vp_ref[prows, :]
                else:
                    prows = rows_of((j - 1) * span + r, dil)
                    s_prev = _bdot_nt(q, kc_ref[prows, :]) * ATT_SCALE + bias[:, :blk]
                    v_prev = vc_ref[prows, :]
                s_cur = _bdot_nt(q, kc_ref[rows, :]) * ATT_SCALE + bias[:, blk:]
                m = jnp.maximum(jnp.max(s_prev, axis=-1, keepdims=True), jnp.max(s_cur, axis=-1, keepdims=True))
                p_prev = jnp.exp(s_prev - m)
                p_cur = jnp.exp(s_cur - m)
                l = jnp.sum(p_prev, axis=-1, keepdims=True) + jnp.sum(p_cur, axis=-1, keepdims=True)
                o_s[rows, :] = _bdot(p_prev, v_prev) + _bdot(p_cur, vc_ref[rows, :])
                m_s[rows, :] = jnp.broadcast_to(m, (blk, ATT_HEAD_DIM))
                l_s[rows, :] = jnp.broadcast_to(l, (blk, ATT_HEAD_DIM))

    for c in range(ATT_TILE // MERGE_ROWS):
        rs = slice(c * MERGE_ROWS, (c + 1) * MERGE_ROWS)
        parts = [(scr[3 * gi][rs, :], scr[3 * gi + 1][rs, :], scr[3 * gi + 2][rs, :]) for gi in range(ng)]
        y_ref[rs, :] = _merge_three(parts).astype(y_ref.dtype)


def _attn_prompt(qkv, bias_p, bsz, s):
    hdim = ATT_HEAD_DIM
    ng = len(ATT_GROUPS)
    n_heads = qkv.shape[1] // (3 * hdim)
    tile = ATT_TILE
    assert s % tile == 0
    nt = s // tile
    in_specs, args = [], []
    for gi, (window, dil) in enumerate(ATT_GROUPS):
        span = ATT_BLOCK * dil
        per = tile // span

        def cur(part, gi=gi):
            return pl.BlockSpec((tile, hdim), lambda b, i, h: (b * nt + i, part * n_heads + gi * ATT_HPG + h))

        def prev(part, gi=gi, per=per):
            return pl.BlockSpec((span, hdim),
                                lambda b, i, h: (jnp.maximum((b * nt + i) * per - 1, 0),
                                                 part * n_heads + gi * ATT_HPG + h))

        in_specs += [cur(0), cur(1), cur(2), prev(1), prev(2)]
        args += [qkv] * 5
    in_specs.append(pl.BlockSpec((ng, None, ATT_BLOCK, 2 * ATT_BLOCK), lambda b, i, h: (0, h, 0, 0)))
    args.append(bias_p)
    return pl.pallas_call(
        _attn_prompt_kernel,
        grid=(bsz, nt, ATT_HPG),
        in_specs=in_specs,
        out_specs=pl.BlockSpec((tile, hdim), lambda b, i, h: (b * nt + i, h)),
        out_shape=jax.ShapeDtypeStruct((bsz * s, ATT_HPG * hdim), BF16),
        scratch_shapes=[pltpu.VMEM((tile, hdim), F32)] * (3 * ng),
        compiler_params=_cparams("parallel", "parallel", "arbitrary"),
        name="attn_prompt",
    )(*args)


def _attn_sample_kernel(qkv_ref, c0_ref, c1_ref, c2_ref, bias_ref, bias0_ref, y_ref):
    ng = len(ATT_GROUPS)
    hp = ATT_HPG
    caches = (c0_ref, c1_ref, c2_ref)
    zeros = jnp.zeros((hp, ATT_HEAD_DIM), F32)

    def body(b, carry):
        parts = []
        for gi in range(ng):
            q = qkv_ref[b, gi]
            k_new = qkv_ref[b, ng + gi]
            v_new = qkv_ref[b, 2 * ng + gi]
            kv = caches[gi][b]
            q2 = jnp.concatenate([q * ATT_SCALE, zeros], axis=0)
            s_past = jnp.sum(kv * q2[None], axis=-1, keepdims=True) + bias_ref[gi]
            s_self = jnp.sum(k_new * q, axis=-1, keepdims=True) * ATT_SCALE + bias0_ref[gi]
            m2 = jnp.maximum(jnp.max(s_past, axis=0), jnp.concatenate([s_self, zeros], axis=0))
            p_past = jnp.exp(s_past - m2[None])
            m = m2[:hp]
            p_self = jnp.exp(s_self - m)
            l = jnp.sum(p_past, axis=0)[:hp] + p_self
            p_on_v = pltpu.roll(p_past, hp, 1)
            o = jnp.sum(p_on_v * kv, axis=0)[hp:] + p_self * v_new
            parts.append((o, m, l))
        y_ref[b] = _merge_three(parts)
        return carry

    lax.fori_loop(0, DEC_TILE, body, 0)


def _attn_sample(qkv2, caches, layer, bias_d, bias0):
    db = qkv2.shape[0]
    bt = DEC_TILE
    hdim = ATT_HEAD_DIM
    ng = len(ATT_GROUPS)
    views, specs = [], []
    for (window, dil), cache in zip(ATT_GROUPS, caches):
        nk = window // dil
        depth, _, wb = cache.shape[:3]
        assert wb == window, "cached window must equal the group's window"
        views.append(cache.reshape(depth, db, nk, dil, 2 * ATT_HPG, hdim))
        specs.append(pl.BlockSpec((None, bt, nk, None, 2 * ATT_HPG, hdim),
                                  lambda i: (layer, i, 0, 0, 0, 0)))
    y = pl.pallas_call(
        _attn_sample_kernel,
        grid=(db // bt,),
        in_specs=[pl.BlockSpec((bt, 3 * ng, ATT_HPG, hdim), lambda i: (i, 0, 0, 0))] + specs +
                 [pl.BlockSpec(bias_d.shape, lambda i: (0, 0, 0, 0)),
                  pl.BlockSpec(bias0.shape, lambda i: (0, 0, 0))],
        out_specs=pl.BlockSpec((bt, ATT_HPG, hdim), lambda i: (i, 0, 0)),
        out_shape=jax.ShapeDtypeStruct((db, ATT_HPG, hdim), F32),
        compiler_params=_cparams("parallel"),
        name="attn_sample",
    )(qkv2.reshape(db, 3 * ng, ATT_HPG, hdim), *views, bias_d, bias0)
    return y.reshape(db, ATT_HPG * hdim)


def _branch_kernel(ysc_ref, yssm_ref, yatt_ref, wsc_ref, wssm_ref, watt_ref, g0_ref, g1_ref, g2_ref, o_ref):
    a = jnp.dot(ysc_ref[...], wsc_ref[...], preferred_element_type=F32)
    b = jnp.dot(yssm_ref[...], wssm_ref[...], preferred_element_type=F32)
    c = jnp.dot(yatt_ref[...], watt_ref[...], preferred_element_type=F32)
    o = _sigmoid(g0_ref[...]) * a + _sigmoid(g1_ref[...]) * b + _sigmoid(g2_ref[...]) * c
    o_ref[...] = o.astype(o_ref.dtype)


def _branch_merge(ysc, yssm, yatt, wsc, wssm, watt, layer, p2, tm, tn):
    m, d = ysc.shape
    nj = d // tn
    rows = lambda width: pl.BlockSpec((tm, width), lambda i, j: (i, 0))
    wcol = lambda k: pl.BlockSpec((None, k, tn), lambda i, j: (layer, 0, j))
    gate = lambda br: pl.BlockSpec((tm, tn), lambda i, j: (i, br * nj + j))
    return pl.pallas_call(
        _branch_kernel,
        grid=(m // tm, nj),
        in_specs=[rows(d), rows(d), rows(yatt.shape[1]), wcol(d), wcol(d), wcol(yatt.shape[1]),
                  gate(0), gate(1), gate(2)],
        out_specs=pl.BlockSpec((tm, tn), lambda i, j: (i, j)),
        out_shape=jax.ShapeDtypeStruct((m, d), BF16),
        compiler_params=_cparams("parallel", "arbitrary"),
        name="branch_merge",
    )(ysc, yssm, yatt, wsc, wssm, watt, p2, p2, p2)


def _outproj_kernel(x_ref, mg_ref, w_ref, g_ref, x1_ref, h2_ref):
    x1 = x_ref[...] + jnp.dot(mg_ref[...], w_ref[...], preferred_element_type=F32)
    x1_ref[...] = x1
    ms = jnp.mean(x1 * x1, axis=-1, keepdims=True)
    h2_ref[...] = (x1 * lax.rsqrt(ms + EPS) * g_ref[...]).astype(h2_ref.dtype)


def _outproj(x2d, merged, w_out, layer, g2, tm):
    m, d = x2d.shape
    rows = pl.BlockSpec((tm, d), lambda i: (i, 0))
    return pl.pallas_call(
        _outproj_kernel,
        grid=(m // tm,),
        in_specs=[rows, rows, pl.BlockSpec((None, d, d), lambda i: (layer, 0, 0)),
                  pl.BlockSpec((1, d), lambda i: (0, 0))],
        out_specs=[rows, rows],
        out_shape=[jax.ShapeDtypeStruct((m, d), F32), jax.ShapeDtypeStruct((m, d), BF16)],
        compiler_params=_cparams("parallel"),
        name="outproj_norm",
    )(x2d, merged, w_out, g2.reshape(1, d))


def _mlp_kernel(x1_ref, h2_ref, wu_ref, wd_ref, o_ref):
    @pl.when(pl.program_id(1) == 0)
    def _():
        o_ref[...] = x1_ref[...]

    u = jnp.dot(h2_ref[...], wu_ref[...], preferred_element_type=F32)
    u = jnp.square(jnp.maximum(u, 0.0)).astype(BF16)
    o_ref[...] += jnp.dot(u, wd_ref[...], preferred_element_type=F32)


def _mlp(x1, h2, w_up, w_down, layer, tm, tf):
    m, d = x1.shape
    f = w_up.shape[2]
    rows = pl.BlockSpec((tm, d), lambda i, j: (i, 0))
    return pl.pallas_call(
        _mlp_kernel,
        grid=(m // tm, f // tf),
        in_specs=[rows, rows, pl.BlockSpec((None, d, tf), lambda i, j: (layer, 0, j)),
                  pl.BlockSpec((None, tf, d), lambda i, j: (layer, j, 0))],
        out_specs=rows,
        out_shape=jax.ShapeDtypeStruct((m, d), F32),
        compiler_params=_cparams("parallel", "arbitrary"),
        name="mlp",
    )(x1, h2, w_up, w_down)


def _row_tile(m, pref):
    return pref if m % pref == 0 else m


def _pad_lanes(v):
    return jnp.pad(v.astype(F32), (0, LANES - v.shape[0])).reshape(1, LANES)


def _prep_layer(l, shared, norm1_g, sc_conv_w, ssm_conv_w, ssm_conv_b, ssm_dt_bias, ssm_A_log, ssm_D,
                ssm_norm_g, q_norm_g, k_norm_g, norm2_g):
    wt_in = shared["wt_in"]
    d = wt_in.shape[2]
    heads = d // SSM_HEAD_DIM
    aw = ATT_HPG * len(ATT_GROUPS) * ATT_HEAD_DIM
    n_main = N_BRANCH * d + 3 * d + d + (d + 2 * SSM_GROUPS * SSM_STATE)
    assert n_main % LANES == 0 and heads <= LANES
    n_att_heads = aw // ATT_HEAD_DIM
    gains = jnp.concatenate([jnp.tile(q_norm_g[l], n_att_heads), jnp.tile(k_norm_g[l], n_att_heads),
                             jnp.ones((aw,), F32)]).reshape(1, 3 * aw)
    return dict(
        shared, layer=l, n_main=n_main,
        norm1_g=norm1_g[l], norm2_g=norm2_g[l],
        wt_qkv=wt_in[l, n_main + heads:, :], qk_gains=gains,
        sc_conv_w=sc_conv_w[l], ssm_conv_w=ssm_conv_w[l], ssm_conv_b=ssm_conv_b[l].reshape(1, -1),
        dt_bias=_pad_lanes(ssm_dt_bias[l]), a_log=_pad_lanes(ssm_A_log[l]),
        d_exp=jnp.repeat(ssm_D[l].astype(F32), SSM_HEAD_DIM).reshape(1, d),
        ssm_norm_g=ssm_norm_g[l].reshape(1, d),
    )


def _in_proj(x2d, w):
    m = x2d.shape[0]
    tm = _row_tile(m, 1024)
    h = _rmsnorm(x2d, w["norm1_g"], _row_tile(m, 512))
    p = _matmul_wcast(h, w["wt_in"], w["layer"], 0, w["n_main"], tm, 1024, name="in_proj_main")
    dtr = _matmul_wcast(h, w["wt_in"], w["layer"], w["n_main"] // LANES, LANES, tm, LANES, name="in_proj_dt")
    qkv = _matmul_qkv(h, w["wt_qkv"], w["qk_gains"], tm, ATT_HPG * ATT_HEAD_DIM)
    return p, dtr, qkv


def _tail(x2d, p, ysc, yssm, yatt, w):
    m = x2d.shape[0]
    l = w["layer"]
    merged = _branch_merge(ysc, yssm, yatt, w["w_br_sc"], w["w_br_ssm"], w["w_br_att"], l, p,
                           _row_tile(m, 1024), 512)
    x1, h2 = _outproj(x2d, merged, w["w_out"], l, w["norm2_g"], _row_tile(m, 512))
    return _mlp(x1, h2, w["w_up"], w["w_down"], l, _row_tile(m, 512), 1024)


def _kv_rows(qkv3, gi, keep):
    aw = qkv3.shape[2] // 3
    gw = ATT_HPG * ATT_HEAD_DIM
    bsz, ln, _ = qkv3.shape
    k = qkv3[:, ln - keep:, aw + gi * gw:aw + (gi + 1) * gw].reshape(bsz, keep, ATT_HPG, ATT_HEAD_DIM)
    v = qkv3[:, ln - keep:, 2 * aw + gi * gw:2 * aw + (gi + 1) * gw].reshape(bsz, keep, ATT_HPG, ATT_HEAD_DIM)
    return jnp.stack([k, v], axis=2)


def _prompt_block(x3, w, bias_p):
    bsz, s, d = x3.shape
    x2d = x3.reshape(bsz * s, d)
    p, dtr, qkv = _in_proj(x2d, w)
    p3 = p.reshape(bsz, s, -1)
    qkv3 = qkv.reshape(bsz, s, -1)
    ysc, sc_new = _sc_prompt(p3, w["sc_conv_w"], d, 1024, 1024)
    yssm, ht = _ssd_prompt(p3, dtr.reshape(bsz, s, LANES), w["ssm_conv_w"], w["ssm_conv_b"], w["dt_bias"],
                           w["a_log"], w["d_exp"], w["ssm_norm_g"], d)
    xbc_off = N_BRANCH * d + 3 * d + d
    ssm_conv_new = p3[:, s - (SSM_CONV - 1):, xbc_off:]
    kv_new = [_kv_rows(qkv3, gi, min(window, s)) for gi, (window, _) in enumerate(ATT_GROUPS)]
    yatt = _attn_prompt(qkv, bias_p, bsz, s)
    x_out = _tail(x2d, p, ysc.reshape(bsz * s, d), yssm.reshape(bsz * s, d), yatt, w)
    heads = d // SSM_HEAD_DIM
    st = (sc_new, ssm_conv_new, ht.reshape(bsz, heads, SSM_HEAD_DIM, SSM_STATE), kv_new[0], kv_new[1], kv_new[2])
    return x_out.reshape(bsz, s, d), st


def _sample_block(x3, w, l, bias_d, bias0, sc_buf, ssm_buf, ssm_h0_all, kv_caches):
    db, t, d = x3.shape
    assert t == 1, "decode kernels handle one new token per sample"
    x2d = x3.reshape(db, d)
    p, dtr, qkv = _in_proj(x2d, w)
    ysc, u_new = _sc_sample(p, sc_buf.reshape(db, -1), w["sc_conv_w"], d, 512)
    sc_new = jnp.concatenate([sc_buf[:, 1:], u_new[:, None, :]], axis=1)
    heads = d // SSM_HEAD_DIM
    depth = ssm_h0_all.shape[0]
    yssm, ht = _ssd_sample(p, dtr, ssm_buf.reshape(db, -1),
                           ssm_h0_all.reshape(depth, db, heads * SSM_HEAD_DIM, SSM_STATE), l,
                           w["ssm_conv_w"], w["ssm_conv_b"], w["dt_bias"], w["a_log"], w["d_exp"],
                           w["ssm_norm_g"], d)
    xbc_off = N_BRANCH * d + 3 * d + d
    ssm_conv_new = jnp.concatenate([ssm_buf[:, 1:], p[:, None, xbc_off:]], axis=1)
    yatt = _attn_sample(qkv, kv_caches, l, bias_d, bias0)
    qkv3 = qkv.reshape(db, 1, -1)
    kv_new = [_kv_rows(qkv3, gi, 1) for gi in range(len(ATT_GROUPS))]
    x_out = _tail(x2d, p, ysc.astype(BF16), yssm.astype(BF16), yatt.astype(BF16), w)
    st = (sc_new, ssm_conv_new, ht.reshape(db, heads, SSM_HEAD_DIM, SSM_STATE), kv_new[0], kv_new[1], kv_new[2])
    return x_out.reshape(db, 1, d), st


def kernel(x_prompt, x_sample, state_sc_conv, state_ssm_conv, state_ssm, cache_kv_w128, cache_kv_w512,
           cache_kv_w2048, norm1_g, w_in, sc_conv_w, ssm_conv_w, ssm_conv_b, ssm_dt_bias, ssm_A_log, ssm_D,
           ssm_norm_g, q_norm_g, k_norm_g, rel_bias, w_br_sc, w_br_ssm, w_br_att, w_out, norm2_g, w_up, w_down):
    depth = w_in.shape[0]
    ng = len(ATT_GROUPS)
    bias_p, bias_d = _build_bias(rel_bias)
    bias_p = bias_p.reshape((ng, ATT_HPG) + bias_p.shape[1:])
    bias_d = bias_d.reshape((ng, ATT_HPG) + bias_d.shape[1:]).transpose(0, 2, 1, 3)
    bias_d = jnp.concatenate([bias_d, jnp.zeros_like(bias_d)], axis=2)
    bias0 = jnp.broadcast_to(rel_bias[0].reshape(ng, ATT_HPG, 1), (ng, ATT_HPG, LANES))
    xp, xs = x_prompt, x_sample
    p_new, s_new = [], []
    shared = dict(wt_in=jnp.swapaxes(w_in, 1, 2),
                  w_br_sc=w_br_sc.astype(BF16), w_br_ssm=w_br_ssm.astype(BF16), w_br_att=w_br_att.astype(BF16),
                  w_out=w_out.astype(BF16), w_up=w_up.astype(BF16), w_down=w_down.astype(BF16))
    for l in range(depth):
        w = _prep_layer(l, shared, norm1_g, sc_conv_w, ssm_conv_w, ssm_conv_b, ssm_dt_bias, ssm_A_log, ssm_D,
                        ssm_norm_g, q_norm_g, k_norm_g, norm2_g)
        xp, st = _prompt_block(xp, w, bias_p)
        p_new.append(st)
        xs, st = _sample_block(xs, w, l, bias_d, bias0, state_sc_conv[l], state_ssm_conv[l], state_ssm,
                               (cache_kv_w128, cache_kv_w512, cache_kv_w2048))
        s_new.append(st)
    p_out = [jnp.stack(a) for a in zip(*p_new)]
    s_out = [jnp.stack(a) for a in zip(*s_new)]
    return (xp, xs, *p_out, *s_out)
```

```python
import functools
import math

import numpy as np
import jax
import jax.numpy as jnp
from jax import lax
from jax.experimental import pallas as pl
from jax.experimental.pallas import tpu as pltpu

F32 = jnp.float32
BF16 = jnp.bfloat16

N_BRANCH = 3
SC_CONV = 3
SSM_HEAD_DIM = 64
SSM_GROUPS = 8
SSM_STATE = 128
SSM_CONV = 4
SSM_CHUNK = 128
ATT_HEAD_DIM = 128
ATT_GROUPS = ((128, 1), (512, 4), (2048, 16))
ATT_HPG = 4
ATT_BLOCK = 128
ATT_SCALE = 1.0 / math.sqrt(ATT_HEAD_DIM)
N_BUCKETS = 32
MAX_DISTANCE = 2048
EPS = 1e-6

LANES = 128
SUBLANES = 8
MXU_COLS = 256
VMEM_LIMIT_BYTES = 56 * 1024 * 1024

DEC_TILE = 8
NEG_INF = float("-inf")


def _cparams(*sem):
    return pltpu.CompilerParams(dimension_semantics=sem, vmem_limit_bytes=VMEM_LIMIT_BYTES)


def _sigmoid(v):
    return 1.0 / (1.0 + jnp.exp(-v))


def _silu(v):
    return v * _sigmoid(v)


def _bdot(a, b):
    return jnp.dot(a.astype(BF16), b.astype(BF16), preferred_element_type=F32)


def _bdot_nt(a, b):
    return lax.dot_general(a.astype(BF16), b.astype(BF16), (((1,), (1,)), ((), ())),
                           preferred_element_type=F32)


def _rmsnorm_kernel(x_ref, g_ref, o_ref):
    x = x_ref[...]
    ms = jnp.mean(x * x, axis=-1, keepdims=True)
    o_ref[...] = (x * lax.rsqrt(ms + EPS) * g_ref[...]).astype(o_ref.dtype)


def _rmsnorm(x2d, g, tm):
    m, d = x2d.shape
    return pl.pallas_call(
        _rmsnorm_kernel,
        grid=(m // tm,),
        in_specs=[pl.BlockSpec((tm, d), lambda i: (i, 0)),
                  pl.BlockSpec((1, d), lambda i: (0, 0))],
        out_specs=pl.BlockSpec((tm, d), lambda i: (i, 0)),
        out_shape=jax.ShapeDtypeStruct((m, d), BF16),
        compiler_params=_cparams("parallel"),
        name="rmsnorm",
    )(x2d, g.reshape(1, d))


def _round_and_dot(a_ref, as_ref, w_ref, wb_ref, emit):
    tn = wb_ref.shape[0]
    cw = min(tn, MXU_COLS)

    @pl.when(pl.program_id(1) == 0)
    def _():
        for c in range(tn // cw):
            rs = slice(c * cw, (c + 1) * cw)
            wb_ref[rs, :] = w_ref[rs, :].astype(BF16)
            emit(0, rs, _bdot_nt(a_ref[...], wb_ref[rs, :]))
            emit(1, rs, _bdot_nt(as_ref[...], wb_ref[rs, :]))

    @pl.when(pl.program_id(1) != 0)
    def _():
        emit(0, slice(0, tn), _bdot_nt(a_ref[...], wb_ref[...]))


def _mm_wcast_kernel(a_ref, as_ref, w_ref, o_ref, os_ref, wb_ref):
    outs = (o_ref, os_ref)

    def emit(which, cols, acc):
        outs[which][:, cols] = acc

    _round_and_dot(a_ref, as_ref, w_ref, wb_ref, emit)


def _matmul_wcast(a, a_s, wt_all, layer, first_block, n_cols, tm, tn, name):
    m, k = a.shape
    ms = a_s.shape[0]
    return pl.pallas_call(
        _mm_wcast_kernel,
        grid=(n_cols // tn, m // tm),
        in_specs=[pl.BlockSpec((tm, k), lambda j, i: (i, 0)),
                  pl.BlockSpec((ms, k), lambda j, i: (0, 0)),
                  pl.BlockSpec((None, tn, k), lambda j, i: (layer, first_block + j, 0))],
        out_specs=[pl.BlockSpec((tm, tn), lambda j, i: (i, j)),
                   pl.BlockSpec((ms, tn), lambda j, i: (0, j))],
        out_shape=[jax.ShapeDtypeStruct((m, n_cols), F32), jax.ShapeDtypeStruct((ms, n_cols), F32)],
        scratch_shapes=[pltpu.VMEM((tn, k), BF16)],
        compiler_params=_cparams("parallel", "arbitrary"),
        name=name,
    )(a, a_s, wt_all)


def _mm_qkv_kernel(a_ref, as_ref, w_ref, g_ref, o_ref, os_ref, wb_ref, *, n_norm_blocks):
    is_qk = pl.program_id(0) < n_norm_blocks
    outs = (o_ref, os_ref)

    def emit(which, cols, acc):
        for h in range((cols.stop - cols.start) // ATT_HEAD_DIM):
            sl = slice(cols.start + h * ATT_HEAD_DIM, cols.start + (h + 1) * ATT_HEAD_DIM)
            t = acc[:, h * ATT_HEAD_DIM:(h + 1) * ATT_HEAD_DIM]
            ms = jnp.mean(t * t, axis=-1, keepdims=True)
            outs[which][:, sl] = t * jnp.where(is_qk, lax.rsqrt(ms + EPS), 1.0) * g_ref[:, sl]

    _round_and_dot(a_ref, as_ref, w_ref.at[0], wb_ref, emit)


def _matmul_qkv(a, a_s, wt_all, layer, first_row, n, gains, tm, tn):
    m, k = a.shape
    ms = a_s.shape[0]
    assert first_row % SUBLANES == 0 and tn % SUBLANES == 0
    att_width = n // 3
    return pl.pallas_call(
        functools.partial(_mm_qkv_kernel, n_norm_blocks=2 * att_width // tn),
        grid=(n // tn, m // tm),
        in_specs=[pl.BlockSpec((tm, k), lambda j, i: (i, 0)),
                  pl.BlockSpec((ms, k), lambda j, i: (0, 0)),
                  pl.BlockSpec((pl.Element(1), pl.Element(tn), pl.Element(k)),
                               lambda j, i: (layer, pl.multiple_of(first_row + j * tn, SUBLANES), 0)),
                  pl.BlockSpec((1, tn), lambda j, i: (0, j))],
        out_specs=[pl.BlockSpec((tm, tn), lambda j, i: (i, j)),
                   pl.BlockSpec((ms, tn), lambda j, i: (0, j))],
        out_shape=[jax.ShapeDtypeStruct((m, n), F32), jax.ShapeDtypeStruct((ms, n), F32)],
        scratch_shapes=[pltpu.VMEM((tn, k), BF16)],
        compiler_params=_cparams("parallel", "arbitrary"),
        name="matmul_qkv",
    )(a, a_s, wt_all, gains)


def _sc_prompt_kernel(b_ref, c_ref, x_ref, hc_ref, hx_ref, w_ref, y_ref, tail_ref, ext_ref, *, ts):
    t = pl.program_id(2)
    u = c_ref[...] * x_ref[...]
    hu = hc_ref[...] * hx_ref[...]
    ext_ref[0:SUBLANES, :] = jnp.where(t > 0, hu, 0.0)
    ext_ref[SUBLANES:SUBLANES + ts, :] = u
    u2 = ext_ref[pl.ds(SUBLANES - 2, ts), :]
    u1 = ext_ref[pl.ds(SUBLANES - 1, ts), :]
    y = u2 * w_ref[0:1, :] + u1 * w_ref[1:2, :] + u * w_ref[2:3, :]
    y_ref[...] = (b_ref[...] * y).astype(y_ref.dtype)
    tail_ref[...] = ext_ref[pl.ds(ts, SUBLANES), :]


def _sc_prompt(p3, conv_w, d, ts, tc):
    bsz, s, _ = p3.shape
    ncb = d // tc
    off_b, off_c, off_x = 3 * ncb, 4 * ncb, 5 * ncb
    hb = ts // SUBLANES

    def halo(off):
        return pl.BlockSpec((None, SUBLANES, tc),
                            lambda b, c, t: (b, jnp.maximum(t * hb - 1, 0), off + c))

    def main(off):
        return pl.BlockSpec((None, ts, tc), lambda b, c, t: (b, t, off + c))

    y, tail = pl.pallas_call(
        functools.partial(_sc_prompt_kernel, ts=ts),
        grid=(bsz, ncb, s // ts),
        in_specs=[main(off_b), main(off_c), main(off_x), halo(off_c), halo(off_x),
                  pl.BlockSpec((SC_CONV, tc), lambda b, c, t: (0, c))],
        out_specs=[pl.BlockSpec((None, ts, tc), lambda b, c, t: (b, t, c)),
                   pl.BlockSpec((None, SUBLANES, tc), lambda b, c, t: (b, 0, c))],
        out_shape=[jax.ShapeDtypeStruct((bsz, s, d), BF16),
                   jax.ShapeDtypeStruct((bsz, SUBLANES, d), F32)],
        scratch_shapes=[pltpu.VMEM((ts + SUBLANES, tc), F32)],
        compiler_params=_cparams("parallel", "parallel", "arbitrary"),
        name="short_conv_prompt",
    )(p3, p3, p3, p3, p3, conv_w)
    return y, tail[:, SUBLANES - (SC_CONV - 1):, :]


def _sc_sample_kernel(b_ref, c_ref, x_ref, s0_ref, s1_ref, w_ref, y_ref, u_ref):
    u = c_ref[...] * x_ref[...]
    y = s0_ref[...] * w_ref[0:1, :] + s1_ref[...] * w_ref[1:2, :] + u * w_ref[2:3, :]
    y_ref[...] = b_ref[...] * y
    u_ref[...] = u


def _sc_sample(p2, state2d, conv_w, d, tc):
    db = p2.shape[0]
    ncb = d // tc

    def col(off):
        return pl.BlockSpec((db, tc), lambda c: (0, off + c))

    return pl.pallas_call(
        _sc_sample_kernel,
        grid=(ncb,),
        in_specs=[col(3 * ncb), col(4 * ncb), col(5 * ncb), col(0), col(ncb),
                  pl.BlockSpec((SC_CONV, tc), lambda c: (0, c))],
        out_specs=[col(0), col(0)],
        out_shape=[jax.ShapeDtypeStruct((db, d), F32), jax.ShapeDtypeStruct((db, d), F32)],
        compiler_params=_cparams("parallel"),
        name="short_conv_sample",
    )(p2, p2, p2, state2d, state2d, conv_w)


def _softplus(v):
    return jnp.maximum(v, 0.0) + jnp.log1p(jnp.exp(-jnp.abs(v)))


def _gated_group_norm(y, xs, z, dexp, ng):
    val = (y + dexp * xs) * _silu(z)
    ms = jnp.mean(val * val, axis=-1, keepdims=True)
    return val * lax.rsqrt(ms + EPS) * ng


def _ssd_prompt_kernel(x_ref, bc_ref, z_ref, dtr_ref, cwx_ref, cwbc_ref, cbx_ref, cbbc_ref, dtb_ref,
                       alog_ref, dexp_ref, ng_ref, y_ref, ht_ref, extx, extbc, xs_s, bc_s, state):
    q = SSM_CHUNK
    c = pl.program_id(1)
    pad = SUBLANES
    taps = SSM_CONV

    @pl.when(c == 0)
    def _():
        extx[0:pad, :] = jnp.zeros((pad, extx.shape[1]), F32)
        extbc[0:pad, :] = jnp.zeros((pad, extbc.shape[1]), F32)
        state[...] = jnp.zeros(state.shape, F32)

    extx[pad:pad + q, :] = x_ref[...]
    extbc[pad:pad + q, :] = bc_ref[...]

    def conv(ext, w_ref, b_ref):
        acc = ext[pl.ds(pad - taps + 1, q), :] * w_ref[0:1, :]
        for i in range(1, taps):
            acc = acc + ext[pl.ds(pad - taps + 1 + i, q), :] * w_ref[i:i + 1, :]
        return _silu(acc + b_ref[...])

    xs_s[...] = conv(extx, cwx_ref, cbx_ref)
    bc_s[...] = conv(extbc, cwbc_ref, cbbc_ref)
    extx[0:pad, :] = extx[q:q + pad, :]
    extbc[0:pad, :] = extbc[q:q + pad, :]

    dt = _softplus(dtr_ref[...] + dtb_ref[...])
    a_neg = -jnp.exp(alog_ref[...])
    ii = lax.broadcasted_iota(jnp.int32, (q, q), 0)
    jj = lax.broadcasted_iota(jnp.int32, (q, q), 1)
    causal = ii >= jj
    acum = jnp.dot(causal.astype(F32), dt * a_neg, precision=lax.Precision.HIGHEST,
                   preferred_element_type=F32)
    acum_t = acum.T
    dt_t = dt.T
    lo_half = jj < SSM_HEAD_DIM
    gs = SSM_STATE
    hd = SSM_HEAD_DIM
    hpg = xs_s.shape[1] // (SSM_GROUPS * hd)
    gw = hpg * hd
    n_bc = SSM_GROUPS * gs

    for g in range(SSM_GROUPS):
        bg = bc_s[:, g * gs:(g + 1) * gs].astype(BF16)
        cg = bc_s[:, n_bc + g * gs:n_bc + (g + 1) * gs].astype(BF16)
        cb = _bdot_nt(cg, bg)
        ys = []
        for hp in range(hpg // 2):
            e0 = g * hpg + 2 * hp
            rows = slice(e0 * hd, (e0 + 2) * hd)
            xp = xs_s[:, rows]
            xp16 = xp.astype(BF16)
            cols_a, cols_dt, parts = [], [], []
            for k in range(2):
                e = e0 + k
                col_a = jnp.broadcast_to(acum[:, e:e + 1], (q, q))
                cols_a.append(col_a)
                cols_dt.append(jnp.broadcast_to(dt[:, e:e + 1], (q, q)))
                dec = jnp.exp(jnp.where(causal, col_a - acum_t[e:e + 1, :], NEG_INF))
                wm = cb * dec * dt_t[e:e + 1, :]
                parts.append(jnp.dot(wm.astype(BF16), xp16, preferred_element_type=F32))
            y_diag = jnp.where(lo_half, parts[0], parts[1])
            sel_a = jnp.where(lo_half, cols_a[0], cols_a[1])
            sel_dt = jnp.where(lo_half, cols_dt[0], cols_dt[1])
            hpair = state[rows, :]
            y_off = _bdot_nt(cg, hpair) * jnp.exp(sel_a)
            ys.append(y_diag + y_off)
            last_a = sel_a[q - 1:q, :]
            xw = xp * jnp.exp(last_a - sel_a) * sel_dt
            st = jnp.dot(xw.T.astype(BF16), bg, preferred_element_type=F32)
            cd = jnp.concatenate([jnp.broadcast_to(cols_a[0][q - 1:q, :], (hd, gs)),
                                  jnp.broadcast_to(cols_a[1][q - 1:q, :], (hd, gs))], axis=0)
            state[rows, :] = hpair * jnp.exp(cd) + st
        gsl = slice(g * gw, (g + 1) * gw)
        yg = jnp.concatenate(ys, axis=1)
        y_ref[:, gsl] = _gated_group_norm(yg, xs_s[:, gsl], z_ref[:, gsl], dexp_ref[:, gsl],
                                          ng_ref[:, gsl]).astype(y_ref.dtype)

    @pl.when(c == pl.num_programs(1) - 1)
    def _():
        ht_ref[...] = state[...]


def _ssd_prompt(p3, dtr3, cw, cb, dtb, alog, dexp, ng, d):
    bsz, s, _ = p3.shape
    q = SSM_CHUNK
    heads = d // SSM_HEAD_DIM
    const = lambda shape: pl.BlockSpec(shape, lambda b, c: (0, 0))
    y, ht = pl.pallas_call(
        _ssd_prompt_kernel,
        grid=(bsz, s // q),
        in_specs=[pl.BlockSpec((None, q, d), lambda b, c: (b, c, 7)),
                  pl.BlockSpec((None, q, d), lambda b, c: (b, c, 8)),
                  pl.BlockSpec((None, q, d), lambda b, c: (b, c, 6)),
                  pl.BlockSpec((None, q, LANES), lambda b, c: (b, c, 0)),
                  const((SSM_CONV, d)), const((SSM_CONV, d)), const((1, d)), const((1, d)),
                  const((1, LANES)), const((1, LANES)), const((1, d)), const((1, d))],
        out_specs=[pl.BlockSpec((None, q, d), lambda b, c: (b, c, 0)),
                   pl.BlockSpec((None, heads * SSM_HEAD_DIM, SSM_STATE), lambda b, c: (b, 0, 0))],
        out_shape=[jax.ShapeDtypeStruct((bsz, s, d), BF16),
                   jax.ShapeDtypeStruct((bsz, heads * SSM_HEAD_DIM, SSM_STATE), F32)],
        scratch_shapes=[pltpu.VMEM((q + SUBLANES, d), F32), pltpu.VMEM((q + SUBLANES, d), F32),
                        pltpu.VMEM((q, d), F32), pltpu.VMEM((q, d), F32),
                        pltpu.VMEM((heads * SSM_HEAD_DIM, SSM_STATE), F32)],
        compiler_params=_cparams("parallel", "arbitrary"),
        name="ssd_prompt",
    )(p3, p3, p3, dtr3, cw[:, :d], cw[:, d:], cb[:, :d], cb[:, d:], dtb, alog, dexp, ng)
    return y, ht


def _ssd_sample_kernel(x_ref, bc_ref, z_ref, dtr_ref, s0x, s0bc, s1x, s1bc, s2x, s2bc, h0_ref, cwx_ref,
                       cwbc_ref, cbx_ref, cbbc_ref, dtb_ref, alog_ref, dexp_ref, ng_ref,
                       y_ref, ht_ref, padx, padbc, padd, pada, y_s):
    bt = DEC_TILE
    hd = SSM_HEAD_DIM
    gs = SSM_STATE
    d = x_ref.shape[1]
    n_chunk = d // LANES
    hpg = d // (SSM_GROUPS * hd)
    gw = hpg * hd
    n_bc = SSM_GROUPS * gs

    @pl.when(pl.program_id(0) == 0)
    def _():
        padx[...] = jnp.zeros(padx.shape, F32)
        padbc[...] = jnp.zeros(padbc.shape, F32)
        padd[...] = jnp.zeros(padd.shape, F32)
        pada[...] = jnp.zeros(pada.shape, F32)

    def conv(s0, s1, s2, u, w_ref, b_ref):
        acc = s0[...] * w_ref[0:1, :] + s1[...] * w_ref[1:2, :] + s2[...] * w_ref[2:3, :] + u[...] * w_ref[3:4, :]
        return _silu(acc + b_ref[...])

    padx[0:bt, :] = conv(s0x, s1x, s2x, x_ref, cwx_ref, cbx_ref)
    padbc[0:bt, :] = conv(s0bc, s1bc, s2bc, bc_ref, cwbc_ref, cbbc_ref)
    dt = _softplus(dtr_ref[...] + dtb_ref[...])
    padd[0:bt, :] = dt
    pada[0:bt, :] = jnp.exp(dt * (-jnp.exp(alog_ref[...])))
    dt_t = padd[...].T
    da_t = pada[...].T
    wide = bt * gs
    row_w = lax.broadcasted_iota(jnp.int32, (LANES, wide), 0)
    blk_w = lax.broadcasted_iota(jnp.int32, (LANES, wide), 1) // gs
    row_t = lax.broadcasted_iota(jnp.int32, (wide, LANES), 0) // gs
    lane_t = lax.broadcasted_iota(jnp.int32, (wide, LANES), 1)

    bbig = cbig_t = None
    for ci in range(n_chunk):
        g = (2 * ci) // hpg
        if (2 * ci) % hpg == 0:
            bg = padbc[:, g * gs:(g + 1) * gs]
            bbig = jnp.where(row_w == blk_w, jnp.concatenate([bg] * bt, axis=1), 0.0).astype(BF16)
            cg_t = padbc[:, n_bc + g * gs:n_bc + (g + 1) * gs].T
            cbig_t = jnp.where(row_t == lane_t, jnp.concatenate([cg_t] * bt, axis=0), 0.0).astype(BF16)
        csl = slice(ci * LANES, (ci + 1) * LANES)
        x_t = padx[:, csl].T
        dt_rows = jnp.concatenate([jnp.broadcast_to(dt_t[2 * ci:2 * ci + 1, :], (hd, LANES)),
                                   jnp.broadcast_to(dt_t[2 * ci + 1:2 * ci + 2, :], (hd, LANES))], axis=0)
        inc = jnp.dot((x_t * dt_rows).astype(BF16), bbig, preferred_element_type=F32)
        hs = []
        for k in range(bt):
            da_rows = jnp.concatenate(
                [jnp.broadcast_to(da_t[2 * ci:2 * ci + 1, k:k + 1], (hd, gs)),
                 jnp.broadcast_to(da_t[2 * ci + 1:2 * ci + 2, k:k + 1], (hd, gs))], axis=0)
            hn = h0_ref[k, csl, :] * da_rows + inc[:, k * gs:(k + 1) * gs]
            ht_ref[k, csl, :] = hn
            hs.append(hn.astype(BF16))
        y_t = jnp.dot(jnp.concatenate(hs, axis=1), cbig_t, preferred_element_type=F32)
        y_s[:, csl] = y_t.T[0:bt, :]

    for g in range(SSM_GROUPS):
        gsl = slice(g * gw, (g + 1) * gw)
        y_ref[:, gsl] = _gated_group_norm(y_s[:, gsl], padx[0:bt, gsl], z_ref[:, gsl], dexp_ref[:, gsl],
                                          ng_ref[:, gsl])


def _ssd_sample(p2, dtr2, conv_state2d, h0_all, layer, cw, cb, dtb, alog, dexp, ng, d):
    db = p2.shape[0]
    bt = DEC_TILE
    hp = h0_all.shape[2]
    row = lambda off: pl.BlockSpec((bt, d), lambda i: (i, off))
    const = lambda shape: pl.BlockSpec(shape, lambda i: (0, 0))
    y, ht = pl.pallas_call(
        _ssd_sample_kernel,
        grid=(db // bt,),
        in_specs=[row(7), row(8), row(6), pl.BlockSpec((bt, LANES), lambda i: (i, 0)),
                  row(0), row(1), row(2), row(3), row(4), row(5),
                  pl.BlockSpec((None, bt, hp, SSM_STATE), lambda i: (layer, i, 0, 0)),
                  const((SSM_CONV, d)), const((SSM_CONV, d)), const((1, d)), const((1, d)),
                  const((1, LANES)), const((1, LANES)), const((1, d)), const((1, d))],
        out_specs=[pl.BlockSpec((bt, d), lambda i: (i, 0)),
                   pl.BlockSpec((bt, hp, SSM_STATE), lambda i: (i, 0, 0))],
        out_shape=[jax.ShapeDtypeStruct((db, d), F32),
                   jax.ShapeDtypeStruct((db, hp, SSM_STATE), F32)],
        scratch_shapes=[pltpu.VMEM((LANES, d), F32), pltpu.VMEM((LANES, d), F32),
                        pltpu.VMEM((LANES, LANES), F32), pltpu.VMEM((LANES, LANES), F32),
                        pltpu.VMEM((bt, d), F32)],
        compiler_params=_cparams("arbitrary"),
        name="ssd_sample",
    )(p2, p2, p2, dtr2, conv_state2d, conv_state2d, conv_state2d, conv_state2d, conv_state2d,
      conv_state2d, h0_all, cw[:, :d], cw[:, d:], cb[:, :d], cb[:, d:], dtb, alog, dexp, ng)
    return y, ht


def _t5_bucket_np(dist):
    d = np.asarray(dist, np.int32)
    max_exact = N_BUCKETS // 2
    df = np.maximum(d, 1).astype(np.float32)
    large = max_exact + (np.log(df / np.float32(max_exact)) / np.float32(math.log(MAX_DISTANCE / max_exact))
                         * np.float32(N_BUCKETS - max_exact)).astype(np.int32)
    large = np.minimum(large, N_BUCKETS - 1)
    return np.where(d < max_exact, d, large).astype(np.int32)


def _bias_index_tables():
    blk = ATT_BLOCK
    qi = np.arange(blk)[:, None]
    kj = np.arange(2 * blk)[None, :]
    sdist = qi + blk - kj
    idx_p, idx_d = [], []
    for window, dil in ATT_GROUPS:
        nk = window // dil
        band = (sdist >= 0) & (sdist <= nk)
        idx_p.append(np.where(band, _t5_bucket_np(np.clip(sdist, 0, nk) * dil), -1))
        steps = nk - np.arange(nk)
        idx_d.append(np.broadcast_to(_t5_bucket_np(steps * dil)[:, None], (nk, LANES)))
    return np.stack(idx_p).astype(np.int32), np.stack(idx_d).astype(np.int32)


def _bias_kernel(tab_ref, idxp_ref, idxd_ref, bp_ref, bd_ref):
    h = pl.program_id(0)
    ip = idxp_ref[...]
    idd = idxd_ref[...]
    accp = jnp.full(ip.shape, NEG_INF, F32)
    accd = jnp.full(idd.shape, NEG_INF, F32)
    for b in range(N_BUCKETS):
        v = tab_ref[b, h]
        accp = jnp.where(ip == b, v, accp)
        accd = jnp.where(idd == b, v, accd)
    bp_ref[...] = accp
    bd_ref[...] = accd


def _build_bias(rel_bias):
    idx_p, idx_d = _bias_index_tables()
    n_heads = rel_bias.shape[1]
    blk = ATT_BLOCK
    nk = idx_d.shape[1]
    return pl.pallas_call(
        _bias_kernel,
        grid=(n_heads,),
        in_specs=[pl.BlockSpec(memory_space=pltpu.SMEM),
                  pl.BlockSpec((None, blk, 2 * blk), lambda h: (h // ATT_HPG, 0, 0)),
                  pl.BlockSpec((None, nk, LANES), lambda h: (h // ATT_HPG, 0, 0))],
        out_specs=[pl.BlockSpec((None, blk, 2 * blk), lambda h: (h, 0, 0)),
                   pl.BlockSpec((None, nk, LANES), lambda h: (h, 0, 0))],
        out_shape=[jax.ShapeDtypeStruct((n_heads, blk, 2 * blk), F32),
                   jax.ShapeDtypeStruct((n_heads, nk, LANES), F32)],
        compiler_params=_cparams("arbitrary"),
        name="t5_bias",
    )(rel_bias, jnp.asarray(idx_p), jnp.asarray(idx_d))


ATT_TILE = ATT_BLOCK * max(dil for _, dil in ATT_GROUPS)
MERGE_ROWS = 256


def _merge_three(parts):
    m_all = jnp.maximum(jnp.maximum(parts[0][1], parts[1][1]), parts[2][1])
    w = jnp.exp(parts[0][1] - m_all)
    num = w * parts[0][0]
    den = w * parts[0][2]
    for o, m, l in parts[1:]:
        w = jnp.exp(m - m_all)
        num = num + w * o
        den = den + w * l
    return num / den


def _attn_prompt_kernel(*refs):
    ng = len(ATT_GROUPS)
    blk = ATT_BLOCK
    bias_ref = refs[5 * ng]
    y_ref = refs[5 * ng + 1]
    scr = refs[5 * ng + 2:]
    first_tile = pl.program_id(1) == 0

    def rows_of(start, dil):
        return pl.ds(start, blk) if dil == 1 else pl.ds(start, blk, stride=dil)

    for gi, (window, dil) in enumerate(ATT_GROUPS):
        q_ref, kc_ref, vc_ref, kp_ref, vp_ref = refs[5 * gi:5 * gi + 5]
        o_s, m_s, l_s = scr[3 * gi:3 * gi + 3]
        span = blk * dil
        bias = bias_ref[gi]
        for j in range(ATT_TILE // span):
            for r in range(dil):
                rows = rows_of(j * span + r, dil)
                q = q_ref[rows, :]
                if j == 0:
                    prows = rows_of(r, dil)
                    s_prev = _bdot_nt(q, kp_ref[prows, :]) * ATT_SCALE + bias[:, :blk]
                    s_prev = jnp.where(first_tile, NEG_INF, s_prev)
                    v_prev = vp_ref[prows, :]
                else:
                    prows = rows_of((j - 1) * span + r, dil)
                    s_prev = _bdot_nt(q, kc_ref[prows, :]) * ATT_SCALE + bias[:, :blk]
                    v_prev = vc_ref[prows, :]
                s_cur = _bdot_nt(q, kc_ref[rows, :]) * ATT_SCALE + bias[:, blk:]
                m = jnp.maximum(jnp.max(s_prev, axis=-1, keepdims=True), jnp.max(s_cur, axis=-1, keepdims=True))
                p_prev = jnp.exp(s_prev - m)
                p_cur = jnp.exp(s_cur - m)
                l = jnp.sum(p_prev, axis=-1, keepdims=True) + jnp.sum(p_cur, axis=-1, keepdims=True)
                o_s[rows, :] = _bdot(p_prev, v_prev) + _bdot(p_cur, vc_ref[rows, :])
                m_s[rows, :] = jnp.broadcast_to(m, (blk, ATT_HEAD_DIM))
                l_s[rows, :] = jnp.broadcast_to(l, (blk, ATT_HEAD_DIM))

    for c in range(ATT_TILE // MERGE_ROWS):
        rs = slice(c * MERGE_ROWS, (c + 1) * MERGE_ROWS)
        parts = [(scr[3 * gi][rs, :], scr[3 * gi + 1][rs, :], scr[3 * gi + 2][rs, :]) for gi in range(ng)]
        y_ref[rs, :] = _merge_three(parts).astype(y_ref.dtype)


def _attn_prompt(qkv, bias_p, bsz, s):
    hdim = ATT_HEAD_DIM
    ng = len(ATT_GROUPS)
    n_heads = qkv.shape[1] // (3 * hdim)
    tile = ATT_TILE
    assert s % tile == 0
    nt = s // tile
    in_specs, args = [], []
    for gi, (window, dil) in enumerate(ATT_GROUPS):
        span = ATT_BLOCK * dil
        per = tile // span

        def cur(part, gi=gi):
            return pl.BlockSpec((tile, hdim), lambda b, i, h: (b * nt + i, part * n_heads + gi * ATT_HPG + h))

        def prev(part, gi=gi, per=per):
            return pl.BlockSpec((span, hdim),
                                lambda b, i, h: (jnp.maximum((b * nt + i) * per - 1, 0),
                                                 part * n_heads + gi * ATT_HPG + h))

        in_specs += [cur(0), cur(1), cur(2), prev(1), prev(2)]
        args += [qkv] * 5
    in_specs.append(pl.BlockSpec((ng, None, ATT_BLOCK, 2 * ATT_BLOCK), lambda b, i, h: (0, h, 0, 0)))
    args.append(bias_p)
    return pl.pallas_call(
        _attn_prompt_kernel,
        grid=(bsz, nt, ATT_HPG),
        in_specs=in_specs,
        out_specs=pl.BlockSpec((tile, hdim), lambda b, i, h: (b * nt + i, h)),
        out_shape=jax.ShapeDtypeStruct((bsz * s, ATT_HPG * hdim), BF16),
        scratch_shapes=[pltpu.VMEM((tile, hdim), F32)] * (3 * ng),
        compiler_params=_cparams("parallel", "parallel", "arbitrary"),
        name="attn_prompt",
    )(*args)


REDUCE_WAYS = 4
DEC_KEYS = 32


def _reduce_rows(x, op):
    n = x.shape[0]
    part = op(x.reshape((REDUCE_WAYS, n // REDUCE_WAYS) + x.shape[1:]), axis=1)
    return op(part, axis=0)


def _attn_sample_kernel(qkv_ref, c0_ref, c1_ref, c2_ref, bias_ref, bias0_ref, y_ref):
    ng = len(ATT_GROUPS)
    hp = ATT_HPG
    caches = (c0_ref, c1_ref, c2_ref)
    zeros = jnp.zeros((hp, ATT_HEAD_DIM), F32)
    ones = jnp.ones((hp, ATT_HEAD_DIM), F32)

    def body(b, carry):
        parts = []
        for gi in range(ng):
            q = qkv_ref[b, gi]
            k_new = qkv_ref[b, ng + gi]
            v_new = qkv_ref[b, 2 * ng + gi]
            q2 = jnp.concatenate([q * ATT_SCALE, zeros], axis=0)
            s_self = jnp.sum(k_new * q, axis=-1, keepdims=True) * ATT_SCALE + bias0_ref[gi]
            m = jnp.concatenate([s_self, zeros], axis=0)
            l = jnp.concatenate([ones, ones], axis=0)
            o = jnp.concatenate([zeros, v_new], axis=0)
            for c in range(caches[gi].shape[1] // DEC_KEYS):
                ks = slice(c * DEC_KEYS, (c + 1) * DEC_KEYS)
                kv = caches[gi][b, ks]
                s = jnp.sum(kv * q2[None], axis=-1, keepdims=True) + bias_ref[gi, ks]
                m_new = jnp.maximum(m, _reduce_rows(s, jnp.max))
                alpha = jnp.exp(m - m_new)
                p = jnp.exp(s - m_new[None])
                l = alpha * l + _reduce_rows(p, jnp.sum)
                p_on_v = pltpu.roll(p, hp, 1)
                o = pltpu.roll(alpha, hp, 0) * o + _reduce_rows(p_on_v * kv, jnp.sum)
                m = m_new
            parts.append((o[hp:], m[:hp], l[:hp]))
        y_ref[b] = _merge_three(parts)
        return carry

    lax.fori_loop(0, DEC_TILE, body, 0)


def _attn_sample(qkv2, caches, layer, bias_d, bias0):
    db = qkv2.shape[0]
    bt = DEC_TILE
    hdim = ATT_HEAD_DIM
    ng = len(ATT_GROUPS)
    views, specs = [], []
    for (window, dil), cache in zip(ATT_GROUPS, caches):
        nk = window // dil
        depth, _, wb = cache.shape[:3]
        assert wb == window, "cached window must equal the group's window"
        views.append(cache.reshape(depth, db, nk, dil, 2 * ATT_HPG, hdim))
        specs.append(pl.BlockSpec((None, bt, nk, None, 2 * ATT_HPG, hdim),
                                  lambda i: (layer, i, 0, 0, 0, 0)))
    y = pl.pallas_call(
        _attn_sample_kernel,
        grid=(db // bt,),
        in_specs=[pl.BlockSpec((bt, 3 * ng, ATT_HPG, hdim), lambda i: (i, 0, 0, 0))] + specs +
                 [pl.BlockSpec(bias_d.shape, lambda i: (0, 0, 0, 0)),
                  pl.BlockSpec(bias0.shape, lambda i: (0, 0, 0))],
        out_specs=pl.BlockSpec((bt, ATT_HPG, hdim), lambda i: (i, 0, 0)),
        out_shape=jax.ShapeDtypeStruct((db, ATT_HPG, hdim), F32),
        compiler_params=_cparams("parallel"),
        name="attn_sample",
    )(qkv2.reshape(db, 3 * ng, ATT_HPG, hdim), *views, bias_d, bias0)
    return y.reshape(db, ATT_HPG * hdim)


def _branch_kernel(ysc_ref, yssm_ref, yatt_ref, ysc_s, yssm_s, yatt_s, wsc_ref, wssm_ref, watt_ref,
                   g0_ref, g1_ref, g2_ref, g0_s, g1_s, g2_s, o_ref, os_ref):
    def merged(ysc, yssm, yatt, g0, g1, g2):
        a = jnp.dot(ysc[...], wsc_ref[...], preferred_element_type=F32)
        b = jnp.dot(yssm[...], wssm_ref[...], preferred_element_type=F32)
        c = jnp.dot(yatt[...], watt_ref[...], preferred_element_type=F32)
        return (_sigmoid(g0[...]) * a + _sigmoid(g1[...]) * b + _sigmoid(g2[...]) * c).astype(o_ref.dtype)

    o_ref[...] = merged(ysc_ref, yssm_ref, yatt_ref, g0_ref, g1_ref, g2_ref)

    @pl.when(pl.program_id(1) == 0)
    def _():
        os_ref[...] = merged(ysc_s, yssm_s, yatt_s, g0_s, g1_s, g2_s)


def _branch_merge(ys, ys_s, wsc, wssm, watt, layer, p2, p2_s, tm, tn):
    m, d = ys[0].shape
    ms = ys_s[0].shape[0]
    da = ys[2].shape[1]
    nj = d // tn
    rows = lambda width: pl.BlockSpec((tm, width), lambda j, i: (i, 0))
    rows_s = lambda width: pl.BlockSpec((ms, width), lambda j, i: (0, 0))
    wcol = lambda k: pl.BlockSpec((None, k, tn), lambda j, i: (layer, 0, j))
    gate = lambda br: pl.BlockSpec((tm, tn), lambda j, i: (i, br * nj + j))
    gate_s = lambda br: pl.BlockSpec((ms, tn), lambda j, i: (0, br * nj + j))
    return pl.pallas_call(
        _branch_kernel,
        grid=(nj, m // tm),
        in_specs=[rows(d), rows(d), rows(da), rows_s(d), rows_s(d), rows_s(da), wcol(d), wcol(d), wcol(da),
                  gate(0), gate(1), gate(2), gate_s(0), gate_s(1), gate_s(2)],
        out_specs=[pl.BlockSpec((tm, tn), lambda j, i: (i, j)), pl.BlockSpec((ms, tn), lambda j, i: (0, j))],
        out_shape=[jax.ShapeDtypeStruct((m, d), BF16), jax.ShapeDtypeStruct((ms, d), BF16)],
        compiler_params=_cparams("parallel", "arbitrary"),
        name="branch_merge",
    )(*ys, *ys_s, wsc, wssm, watt, p2, p2, p2, p2_s, p2_s, p2_s)


def _outproj_kernel(x_ref, mg_ref, x_s, mg_s, w_ref, g_ref, x1_ref, h2_ref, x1_s, h2_s):
    def project(x, mg, x1_out, h2_out):
        x1 = x[...] + jnp.dot(mg[...], w_ref[...], preferred_element_type=F32)
        x1_out[...] = x1
        ms = jnp.mean(x1 * x1, axis=-1, keepdims=True)
        h2_out[...] = (x1 * lax.rsqrt(ms + EPS) * g_ref[...]).astype(h2_out.dtype)

    project(x_ref, mg_ref, x1_ref, h2_ref)

    @pl.when(pl.program_id(0) == 0)
    def _():
        project(x_s, mg_s, x1_s, h2_s)


def _outproj(x2d, merged, x2d_s, merged_s, w_out, layer, g2, tm):
    m, d = x2d.shape
    ms = x2d_s.shape[0]
    rows = pl.BlockSpec((tm, d), lambda i: (i, 0))
    rows_s = pl.BlockSpec((ms, d), lambda i: (0, 0))
    return pl.pallas_call(
        _outproj_kernel,
        grid=(m // tm,),
        in_specs=[rows, rows, rows_s, rows_s, pl.BlockSpec((None, d, d), lambda i: (layer, 0, 0)),
                  pl.BlockSpec((1, d), lambda i: (0, 0))],
        out_specs=[rows, rows, rows_s, rows_s],
        out_shape=[jax.ShapeDtypeStruct((m, d), F32), jax.ShapeDtypeStruct((m, d), BF16),
                   jax.ShapeDtypeStruct((ms, d), F32), jax.ShapeDtypeStruct((ms, d), BF16)],
        compiler_params=_cparams("arbitrary"),
        name="outproj_norm",
    )(x2d, merged, x2d_s, merged_s, w_out, g2.reshape(1, d))


def _mlp_kernel(x1_ref, h2_ref, x1_s, h2_s, wu_ref, wd_ref, o_ref, os_ref):
    def accumulate(x1, h2, out):
        @pl.when(pl.program_id(1) == 0)
        def _():
            out[...] = x1[...]

        u = jnp.dot(h2[...], wu_ref[...], preferred_element_type=F32)
        u = jnp.square(jnp.maximum(u, 0.0)).astype(BF16)
        out[...] += jnp.dot(u, wd_ref[...], preferred_element_type=F32)

    accumulate(x1_ref, h2_ref, o_ref)

    @pl.when(pl.program_id(0) == 0)
    def _():
        accumulate(x1_s, h2_s, os_ref)


def _mlp(x1, h2, x1_s, h2_s, w_up, w_down, layer, tm, tf):
    m, d = x1.shape
    ms = x1_s.shape[0]
    f = w_up.shape[2]
    rows = pl.BlockSpec((tm, d), lambda i, j: (i, 0))
    rows_s = pl.BlockSpec((ms, d), lambda i, j: (0, 0))
    return pl.pallas_call(
        _mlp_kernel,
        grid=(m // tm, f // tf),
        in_specs=[rows, rows, rows_s, rows_s, pl.BlockSpec((None, d, tf), lambda i, j: (layer, 0, j)),
                  pl.BlockSpec((None, tf, d), lambda i, j: (layer, j, 0))],
        out_specs=[rows, rows_s],
        out_shape=[jax.ShapeDtypeStruct((m, d), F32), jax.ShapeDtypeStruct((ms, d), F32)],
        compiler_params=_cparams("arbitrary", "arbitrary"),
        name="mlp",
    )(x1, h2, x1_s, h2_s, w_up, w_down)


def _row_tile(m, pref):
    return pref if m % pref == 0 else m


def _pad_lanes(v):
    return jnp.pad(v.astype(F32), (0, LANES - v.shape[0])).reshape(1, LANES)


def _prep_layer(l, shared, norm1_g, sc_conv_w, ssm_conv_w, ssm_conv_b, ssm_dt_bias, ssm_A_log, ssm_D,
                ssm_norm_g, q_norm_g, k_norm_g, norm2_g):
    wt_in = shared["wt_in"]
    d = wt_in.shape[2]
    heads = d // SSM_HEAD_DIM
    aw = ATT_HPG * len(ATT_GROUPS) * ATT_HEAD_DIM
    n_main = N_BRANCH * d + 3 * d + d + (d + 2 * SSM_GROUPS * SSM_STATE)
    assert n_main % LANES == 0 and heads <= LANES
    n_att_heads = aw // ATT_HEAD_DIM
    gains = jnp.concatenate([jnp.tile(q_norm_g[l], n_att_heads), jnp.tile(k_norm_g[l], n_att_heads),
                             jnp.ones((aw,), F32)]).reshape(1, 3 * aw)
    return dict(
        shared, layer=l, n_main=n_main, qkv_row=n_main + heads, n_qkv=3 * aw,
        norm1_g=norm1_g[l], norm2_g=norm2_g[l], qk_gains=gains,
        sc_conv_w=sc_conv_w[l], ssm_conv_w=ssm_conv_w[l], ssm_conv_b=ssm_conv_b[l].reshape(1, -1),
        dt_bias=_pad_lanes(ssm_dt_bias[l]), a_log=_pad_lanes(ssm_A_log[l]),
        d_exp=jnp.repeat(ssm_D[l].astype(F32), SSM_HEAD_DIM).reshape(1, d),
        ssm_norm_g=ssm_norm_g[l].reshape(1, d),
    )


def _in_proj(x2d, x2d_s, w):
    m = x2d.shape[0]
    tm = _row_tile(m, 1024)
    h = _rmsnorm(x2d, w["norm1_g"], _row_tile(m, 512))
    h_s = _rmsnorm(x2d_s, w["norm1_g"], x2d_s.shape[0])
    p = _matmul_wcast(h, h_s, w["wt_in"], w["layer"], 0, w["n_main"], tm, 1024, name="in_proj_main")
    dtr = _matmul_wcast(h, h_s, w["wt_in"], w["layer"], w["n_main"] // LANES, LANES, tm, LANES, name="in_proj_dt")
    qkv = _matmul_qkv(h, h_s, w["wt_in"], w["layer"], w["qkv_row"], w["n_qkv"], w["qk_gains"], tm,
                      ATT_HPG * ATT_HEAD_DIM)
    return (p[0], dtr[0], qkv[0]), (p[1], dtr[1], qkv[1])


def _tail(x2d, p, ys, x2d_s, p_s, ys_s, w):
    m = x2d.shape[0]
    l = w["layer"]
    merged, merged_s = _branch_merge(ys, ys_s, w["w_br_sc"], w["w_br_ssm"], w["w_br_att"], l, p, p_s,
                                     _row_tile(m, 1024), 512)
    x1, h2, x1_s, h2_s = _outproj(x2d, merged, x2d_s, merged_s, w["w_out"], l, w["norm2_g"], _row_tile(m, 512))
    return _mlp(x1, h2, x1_s, h2_s, w["w_up"], w["w_down"], l, _row_tile(m, 512), 1024)


def _kv_rows(qkv3, gi, keep):
    aw = qkv3.shape[2] // 3
    gw = ATT_HPG * ATT_HEAD_DIM
    bsz, ln, _ = qkv3.shape
    k = qkv3[:, ln - keep:, aw + gi * gw:aw + (gi + 1) * gw].reshape(bsz, keep, ATT_HPG, ATT_HEAD_DIM)
    v = qkv3[:, ln - keep:, 2 * aw + gi * gw:2 * aw + (gi + 1) * gw].reshape(bsz, keep, ATT_HPG, ATT_HEAD_DIM)
    return jnp.stack([k, v], axis=2)


def _prompt_mixers(proj, w, bias_p, bsz, s, d):
    p, dtr, qkv = proj
    p3 = p.reshape(bsz, s, -1)
    qkv3 = qkv.reshape(bsz, s, -1)
    ysc, sc_new = _sc_prompt(p3, w["sc_conv_w"], d, 1024, 1024)
    yssm, ht = _ssd_prompt(p3, dtr.reshape(bsz, s, LANES), w["ssm_conv_w"], w["ssm_conv_b"], w["dt_bias"],
                           w["a_log"], w["d_exp"], w["ssm_norm_g"], d)
    xbc_off = N_BRANCH * d + 3 * d + d
    ssm_conv_new = p3[:, s - (SSM_CONV - 1):, xbc_off:]
    kv_new = [_kv_rows(qkv3, gi, min(window, s)) for gi, (window, _) in enumerate(ATT_GROUPS)]
    yatt = _attn_prompt(qkv, bias_p, bsz, s)
    heads = d // SSM_HEAD_DIM
    st = (sc_new, ssm_conv_new, ht.reshape(bsz, heads, SSM_HEAD_DIM, SSM_STATE), kv_new[0], kv_new[1], kv_new[2])
    return (ysc.reshape(bsz * s, d), yssm.reshape(bsz * s, d), yatt), st


def _sample_mixers(proj, w, bias_d, bias0, sc_buf, ssm_buf, ssm_h0_all, kv_caches, d):
    p, dtr, qkv = proj
    db = p.shape[0]
    l = w["layer"]
    ysc, u_new = _sc_sample(p, sc_buf.reshape(db, -1), w["sc_conv_w"], d, 512)
    sc_new = jnp.concatenate([sc_buf[:, 1:], u_new[:, None, :]], axis=1)
    heads = d // SSM_HEAD_DIM
    depth = ssm_h0_all.shape[0]
    yssm, ht = _ssd_sample(p, dtr, ssm_buf.reshape(db, -1),
                           ssm_h0_all.reshape(depth, db, heads * SSM_HEAD_DIM, SSM_STATE), l,
                           w["ssm_conv_w"], w["ssm_conv_b"], w["dt_bias"], w["a_log"], w["d_exp"],
                           w["ssm_norm_g"], d)
    xbc_off = N_BRANCH * d + 3 * d + d
    ssm_conv_new = jnp.concatenate([ssm_buf[:, 1:], p[:, None, xbc_off:]], axis=1)
    yatt = _attn_sample(qkv, kv_caches, l, bias_d, bias0)
    qkv3 = qkv.reshape(db, 1, -1)
    kv_new = [_kv_rows(qkv3, gi, 1) for gi in range(len(ATT_GROUPS))]
    st = (sc_new, ssm_conv_new, ht.reshape(db, heads, SSM_HEAD_DIM, SSM_STATE), kv_new[0], kv_new[1], kv_new[2])
    return (ysc.astype(BF16), yssm.astype(BF16), yatt.astype(BF16)), st


def kernel(x_prompt, x_sample, state_sc_conv, state_ssm_conv, state_ssm, cache_kv_w128, cache_kv_w512,
           cache_kv_w2048, norm1_g, w_in, sc_conv_w, ssm_conv_w, ssm_conv_b, ssm_dt_bias, ssm_A_log, ssm_D,
           ssm_norm_g, q_norm_g, k_norm_g, rel_bias, w_br_sc, w_br_ssm, w_br_att, w_out, norm2_g, w_up, w_down):
    depth = w_in.shape[0]
    ng = len(ATT_GROUPS)
    bias_p, bias_d = _build_bias(rel_bias)
    bias_p = bias_p.reshape((ng, ATT_HPG) + bias_p.shape[1:])
    bias_d = bias_d.reshape((ng, ATT_HPG) + bias_d.shape[1:]).transpose(0, 2, 1, 3)
    bias_d = jnp.concatenate([bias_d, jnp.zeros_like(bias_d)], axis=2)
    bias0 = jnp.broadcast_to(rel_bias[0].reshape(ng, ATT_HPG, 1), (ng, ATT_HPG, LANES))
    bsz, s, d = x_prompt.shape
    db, t, _ = x_sample.shape
    assert t == 1, "decode kernels handle one new token per sample"
    xp, xs = x_prompt.reshape(bsz * s, d), x_sample.reshape(db, d)
    p_new, s_new = [], []
    shared = dict(wt_in=jnp.swapaxes(w_in, 1, 2),
                  w_br_sc=w_br_sc.astype(BF16), w_br_ssm=w_br_ssm.astype(BF16), w_br_att=w_br_att.astype(BF16),
                  w_out=w_out.astype(BF16), w_up=w_up.astype(BF16), w_down=w_down.astype(BF16))
    for l in range(depth):
        w = _prep_layer(l, shared, norm1_g, sc_conv_w, ssm_conv_w, ssm_conv_b, ssm_dt_bias, ssm_A_log, ssm_D,
                        ssm_norm_g, q_norm_g, k_norm_g, norm2_g)
        proj, proj_s = _in_proj(xp, xs, w)
        ys, st = _prompt_mixers(proj, w, bias_p, bsz, s, d)
        p_new.append(st)
        ys_s, st = _sample_mixers(proj_s, w, bias_d, bias0, state_sc_conv[l], state_ssm_conv[l], state_ssm,
                                  (cache_kv_w128, cache_kv_w512, cache_kv_w2048), d)
        s_new.append(st)
        xp, xs = _tail(xp, proj[0], ys, xs, proj_s[0], ys_s, w)
    p_out = [jnp.stack(a) for a in zip(*p_new)]
    s_out = [jnp.stack(a) for a in zip(*s_new)]
    return (xp.reshape(bsz, s, d), xs.reshape(db, 1, d), *p_out, *s_out)
```

```python
import functools
import math

import numpy as np
import jax
import jax.numpy as jnp
from jax import lax
from jax.experimental import pallas as pl
from jax.experimental.pallas import tpu as pltpu

F32 = jnp.float32
BF16 = jnp.bfloat16

N_BRANCH = 3
SC_CONV = 3
SSM_HEAD_DIM = 64
SSM_GROUPS = 8
SSM_STATE = 128
SSM_CONV = 4
SSM_CHUNK = 128
ATT_HEAD_DIM = 128
ATT_GROUPS = ((128, 1), (512, 4), (2048, 16))
ATT_HPG = 4
ATT_BLOCK = 128
ATT_SCALE = 1.0 / math.sqrt(ATT_HEAD_DIM)
N_BUCKETS = 32
MAX_DISTANCE = 2048
EPS = 1e-6

LANES = 128
SUBLANES = 8
MXU_COLS = 256
VMEM_LIMIT_BYTES = 60 * 1024 * 1024

DEC_TILE = 8
NEG_INF = float("-inf")


def _cparams(*sem):
    return pltpu.CompilerParams(dimension_semantics=sem, vmem_limit_bytes=VMEM_LIMIT_BYTES)


def _sigmoid(v):
    return 1.0 / (1.0 + jnp.exp(-v))


def _silu(v):
    return v * _sigmoid(v)


def _bdot(a, b):
    return jnp.dot(a.astype(BF16), b.astype(BF16), preferred_element_type=F32)


def _bdot_nt(a, b):
    return lax.dot_general(a.astype(BF16), b.astype(BF16), (((1,), (1,)), ((), ())),
                           preferred_element_type=F32)


def _rmsnorm_kernel(x_ref, g_ref, o_ref):
    x = x_ref[...]
    ms = jnp.mean(x * x, axis=-1, keepdims=True)
    o_ref[...] = (x * lax.rsqrt(ms + EPS) * g_ref[...]).astype(o_ref.dtype)


def _rmsnorm(x2d, g, tm):
    m, d = x2d.shape
    return pl.pallas_call(
        _rmsnorm_kernel,
        grid=(m // tm,),
        in_specs=[pl.BlockSpec((tm, d), lambda i: (i, 0)),
                  pl.BlockSpec((1, d), lambda i: (0, 0))],
        out_specs=pl.BlockSpec((tm, d), lambda i: (i, 0)),
        out_shape=jax.ShapeDtypeStruct((m, d), BF16),
        compiler_params=_cparams("parallel"),
        name="rmsnorm",
    )(x2d, g.reshape(1, d))


ROW_CHUNK = 1024


def _resident_rows_dot(a_ref, as_ref, w_ref, wb_ref, emit):
    wb_ref[...] = w_ref[...].astype(BF16)
    m = a_ref.shape[0]
    rc = min(m, ROW_CHUNK)
    for r in range(m // rc):
        rows = slice(r * rc, (r + 1) * rc)
        emit(0, rows, _bdot_nt(a_ref[rows, :], wb_ref[...]))
    emit(1, slice(0, as_ref.shape[0]), _bdot_nt(as_ref[...], wb_ref[...]))


def _mm_wcast_kernel(a_ref, as_ref, w_ref, o_ref, os_ref, wb_ref):
    outs = (o_ref, os_ref)

    def emit(which, rows, acc):
        outs[which][rows, :] = acc

    _resident_rows_dot(a_ref, as_ref, w_ref, wb_ref, emit)


def _resident(shape):
    return pl.BlockSpec(shape, lambda j: (0,) * len(shape), pipeline_mode=pl.Buffered(1))


def _matmul_wcast(a, a_s, wt_all, layer, first_block, n_cols, tn, name):
    m, k = a.shape
    ms = a_s.shape[0]
    assert m % min(m, ROW_CHUNK) == 0
    return pl.pallas_call(
        _mm_wcast_kernel,
        grid=(n_cols // tn,),
        in_specs=[_resident((m, k)), _resident((ms, k)),
                  pl.BlockSpec((None, tn, k), lambda j: (layer, first_block + j, 0))],
        out_specs=[pl.BlockSpec((m, tn), lambda j: (0, j)), pl.BlockSpec((ms, tn), lambda j: (0, j))],
        out_shape=[jax.ShapeDtypeStruct((m, n_cols), F32), jax.ShapeDtypeStruct((ms, n_cols), F32)],
        scratch_shapes=[pltpu.VMEM((tn, k), BF16)],
        compiler_params=_cparams("arbitrary"),
        name=name,
    )(a, a_s, wt_all)


def _mm_qkv_kernel(a_ref, as_ref, w_ref, g_ref, o_ref, os_ref, wb_ref, *, n_norm_blocks):
    is_qk = pl.program_id(0) < n_norm_blocks
    outs = (o_ref, os_ref)

    def emit(which, rows, acc):
        for h in range(acc.shape[1] // ATT_HEAD_DIM):
            sl = slice(h * ATT_HEAD_DIM, (h + 1) * ATT_HEAD_DIM)
            t = acc[:, sl]
            ms = jnp.mean(t * t, axis=-1, keepdims=True)
            outs[which][rows, sl] = t * jnp.where(is_qk, lax.rsqrt(ms + EPS), 1.0) * g_ref[:, sl]

    _resident_rows_dot(a_ref, as_ref, w_ref.at[0], wb_ref, emit)


def _matmul_qkv(a, a_s, wt_all, layer, first_row, n, gains, tn):
    m, k = a.shape
    ms = a_s.shape[0]
    assert first_row % SUBLANES == 0 and tn % ATT_HEAD_DIM == 0
    att_width = n // 3
    return pl.pallas_call(
        functools.partial(_mm_qkv_kernel, n_norm_blocks=2 * att_width // tn),
        grid=(n // tn,),
        in_specs=[_resident((m, k)), _resident((ms, k)),
                  pl.BlockSpec((pl.Element(1), pl.Element(tn), pl.Element(k)),
                               lambda j: (layer, pl.multiple_of(first_row + j * tn, SUBLANES), 0)),
                  pl.BlockSpec((1, tn), lambda j: (0, j))],
        out_specs=[pl.BlockSpec((m, tn), lambda j: (0, j)), pl.BlockSpec((ms, tn), lambda j: (0, j))],
        out_shape=[jax.ShapeDtypeStruct((m, n), F32), jax.ShapeDtypeStruct((ms, n), F32)],
        scratch_shapes=[pltpu.VMEM((tn, k), BF16)],
        compiler_params=_cparams("arbitrary"),
        name="matmul_qkv",
    )(a, a_s, wt_all, gains)


def _sc_prompt_kernel(b_ref, c_ref, x_ref, hc_ref, hx_ref, w_ref, y_ref, tail_ref, ext_ref, *, ts):
    t = pl.program_id(2)
    u = c_ref[...] * x_ref[...]
    hu = hc_ref[...] * hx_ref[...]
    ext_ref[0:SUBLANES, :] = jnp.where(t > 0, hu, 0.0)
    ext_ref[SUBLANES:SUBLANES + ts, :] = u
    u2 = ext_ref[pl.ds(SUBLANES - 2, ts), :]
    u1 = ext_ref[pl.ds(SUBLANES - 1, ts), :]
    y = u2 * w_ref[0:1, :] + u1 * w_ref[1:2, :] + u * w_ref[2:3, :]
    y_ref[...] = (b_ref[...] * y).astype(y_ref.dtype)
    tail_ref[...] = ext_ref[pl.ds(ts, SUBLANES), :]


def _sc_prompt(p3, conv_w, d, ts, tc):
    bsz, s, _ = p3.shape
    ncb = d // tc
    off_b, off_c, off_x = 3 * ncb, 4 * ncb, 5 * ncb
    hb = ts // SUBLANES

    def halo(off):
        return pl.BlockSpec((None, SUBLANES, tc),
                            lambda b, c, t: (b, jnp.maximum(t * hb - 1, 0), off + c))

    def main(off):
        return pl.BlockSpec((None, ts, tc), lambda b, c, t: (b, t, off + c))

    y, tail = pl.pallas_call(
        functools.partial(_sc_prompt_kernel, ts=ts),
        grid=(bsz, ncb, s // ts),
        in_specs=[main(off_b), main(off_c), main(off_x), halo(off_c), halo(off_x),
                  pl.BlockSpec((SC_CONV, tc), lambda b, c, t: (0, c))],
        out_specs=[pl.BlockSpec((None, ts, tc), lambda b, c, t: (b, t, c)),
                   pl.BlockSpec((None, SUBLANES, tc), lambda b, c, t: (b, 0, c))],
        out_shape=[jax.ShapeDtypeStruct((bsz, s, d), BF16),
                   jax.ShapeDtypeStruct((bsz, SUBLANES, d), F32)],
        scratch_shapes=[pltpu.VMEM((ts + SUBLANES, tc), F32)],
        compiler_params=_cparams("parallel", "parallel", "arbitrary"),
        name="short_conv_prompt",
    )(p3, p3, p3, p3, p3, conv_w)
    return y, tail[:, SUBLANES - (SC_CONV - 1):, :]


def _sc_sample_kernel(b_ref, c_ref, x_ref, s0_ref, s1_ref, w_ref, y_ref, u_ref):
    u = c_ref[...] * x_ref[...]
    y = s0_ref[...] * w_ref[0:1, :] + s1_ref[...] * w_ref[1:2, :] + u * w_ref[2:3, :]
    y_ref[...] = b_ref[...] * y
    u_ref[...] = u


def _sc_sample(p2, state2d, conv_w, d, tc):
    db = p2.shape[0]
    ncb = d // tc

    def col(off):
        return pl.BlockSpec((db, tc), lambda c: (0, off + c))

    return pl.pallas_call(
        _sc_sample_kernel,
        grid=(ncb,),
        in_specs=[col(3 * ncb), col(4 * ncb), col(5 * ncb), col(0), col(ncb),
                  pl.BlockSpec((SC_CONV, tc), lambda c: (0, c))],
        out_specs=[col(0), col(0)],
        out_shape=[jax.ShapeDtypeStruct((db, d), F32), jax.ShapeDtypeStruct((db, d), F32)],
        compiler_params=_cparams("parallel"),
        name="short_conv_sample",
    )(p2, p2, p2, state2d, state2d, conv_w)


def _softplus(v):
    return jnp.maximum(v, 0.0) + jnp.log1p(jnp.exp(-jnp.abs(v)))


def _gated_group_norm(y, xs, z, dexp, ng):
    val = (y + dexp * xs) * _silu(z)
    ms = jnp.mean(val * val, axis=-1, keepdims=True)
    return val * lax.rsqrt(ms + EPS) * ng


def _ssd_prompt_kernel(x_ref, bc_ref, z_ref, dtr_ref, cwx_ref, cwbc_ref, cbx_ref, cbbc_ref, dtb_ref,
                       alog_ref, dexp_ref, ng_ref, y_ref, ht_ref, extx, extbc, xs_s, bc_s, state):
    q = SSM_CHUNK
    c = pl.program_id(1)
    pad = SUBLANES
    taps = SSM_CONV

    @pl.when(c == 0)
    def _():
        extx[0:pad, :] = jnp.zeros((pad, extx.shape[1]), F32)
        extbc[0:pad, :] = jnp.zeros((pad, extbc.shape[1]), F32)
        state[...] = jnp.zeros(state.shape, F32)

    extx[pad:pad + q, :] = x_ref[...]
    extbc[pad:pad + q, :] = bc_ref[...]

    def conv(ext, w_ref, b_ref):
        acc = ext[pl.ds(pad - taps + 1, q), :] * w_ref[0:1, :]
        for i in range(1, taps):
            acc = acc + ext[pl.ds(pad - taps + 1 + i, q), :] * w_ref[i:i + 1, :]
        return _silu(acc + b_ref[...])

    xs_s[...] = conv(extx, cwx_ref, cbx_ref)
    bc_s[...] = conv(extbc, cwbc_ref, cbbc_ref)
    extx[0:pad, :] = extx[q:q + pad, :]
    extbc[0:pad, :] = extbc[q:q + pad, :]

    dt = _softplus(dtr_ref[...] + dtb_ref[...])
    a_neg = -jnp.exp(alog_ref[...])
    ii = lax.broadcasted_iota(jnp.int32, (q, q), 0)
    jj = lax.broadcasted_iota(jnp.int32, (q, q), 1)
    causal = ii >= jj
    acum = jnp.dot(causal.astype(F32), dt * a_neg, precision=lax.Precision.HIGHEST,
                   preferred_element_type=F32)
    acum_t = acum.T
    dt_t = dt.T
    lo_half = jj < SSM_HEAD_DIM
    gs = SSM_STATE
    hd = SSM_HEAD_DIM
    hpg = xs_s.shape[1] // (SSM_GROUPS * hd)
    gw = hpg * hd
    n_bc = SSM_GROUPS * gs

    for g in range(SSM_GROUPS):
        bg = bc_s[:, g * gs:(g + 1) * gs].astype(BF16)
        cg = bc_s[:, n_bc + g * gs:n_bc + (g + 1) * gs].astype(BF16)
        cb = _bdot_nt(cg, bg)
        ys = []
        for hp in range(hpg // 2):
            e0 = g * hpg + 2 * hp
            rows = slice(e0 * hd, (e0 + 2) * hd)
            xp = xs_s[:, rows]
            xp16 = xp.astype(BF16)
            cols_a, cols_dt, parts = [], [], []
            for k in range(2):
                e = e0 + k
                col_a = jnp.broadcast_to(acum[:, e:e + 1], (q, q))
                cols_a.append(col_a)
                cols_dt.append(jnp.broadcast_to(dt[:, e:e + 1], (q, q)))
                dec = jnp.exp(jnp.where(causal, col_a - acum_t[e:e + 1, :], NEG_INF))
                wm = cb * dec * dt_t[e:e + 1, :]
                parts.append(jnp.dot(wm.astype(BF16), xp16, preferred_element_type=F32))
            y_diag = jnp.where(lo_half, parts[0], parts[1])
            sel_a = jnp.where(lo_half, cols_a[0], cols_a[1])
            sel_dt = jnp.where(lo_half, cols_dt[0], cols_dt[1])
            hpair = state[rows, :]
            y_off = _bdot_nt(cg, hpair) * jnp.exp(sel_a)
            ys.append(y_diag + y_off)
            last_a = sel_a[q - 1:q, :]
            xw = xp * jnp.exp(last_a - sel_a) * sel_dt
            st = jnp.dot(xw.T.astype(BF16), bg, preferred_element_type=F32)
            cd = jnp.concatenate([jnp.broadcast_to(cols_a[0][q - 1:q, :], (hd, gs)),
                                  jnp.broadcast_to(cols_a[1][q - 1:q, :], (hd, gs))], axis=0)
            state[rows, :] = hpair * jnp.exp(cd) + st
        gsl = slice(g * gw, (g + 1) * gw)
        yg = jnp.concatenate(ys, axis=1)
        y_ref[:, gsl] = _gated_group_norm(yg, xs_s[:, gsl], z_ref[:, gsl], dexp_ref[:, gsl],
                                          ng_ref[:, gsl]).astype(y_ref.dtype)

    @pl.when(c == pl.num_programs(1) - 1)
    def _():
        ht_ref[...] = state[...]


def _ssd_prompt(p3, dtr3, cw, cb, dtb, alog, dexp, ng, d):
    bsz, s, _ = p3.shape
    q = SSM_CHUNK
    heads = d // SSM_HEAD_DIM
    const = lambda shape: pl.BlockSpec(shape, lambda b, c: (0, 0))
    y, ht = pl.pallas_call(
        _ssd_prompt_kernel,
        grid=(bsz, s // q),
        in_specs=[pl.BlockSpec((None, q, d), lambda b, c: (b, c, 7)),
                  pl.BlockSpec((None, q, d), lambda b, c: (b, c, 8)),
                  pl.BlockSpec((None, q, d), lambda b, c: (b, c, 6)),
                  pl.BlockSpec((None, q, LANES), lambda b, c: (b, c, 0)),
                  const((SSM_CONV, d)), const((SSM_CONV, d)), const((1, d)), const((1, d)),
                  const((1, LANES)), const((1, LANES)), const((1, d)), const((1, d))],
        out_specs=[pl.BlockSpec((None, q, d), lambda b, c: (b, c, 0)),
                   pl.BlockSpec((None, heads * SSM_HEAD_DIM, SSM_STATE), lambda b, c: (b, 0, 0))],
        out_shape=[jax.ShapeDtypeStruct((bsz, s, d), BF16),
                   jax.ShapeDtypeStruct((bsz, heads * SSM_HEAD_DIM, SSM_STATE), F32)],
        scratch_shapes=[pltpu.VMEM((q + SUBLANES, d), F32), pltpu.VMEM((q + SUBLANES, d), F32),
                        pltpu.VMEM((q, d), F32), pltpu.VMEM((q, d), F32),
                        pltpu.VMEM((heads * SSM_HEAD_DIM, SSM_STATE), F32)],
        compiler_params=_cparams("parallel", "arbitrary"),
        name="ssd_prompt",
    )(p3, p3, p3, dtr3, cw[:, :d], cw[:, d:], cb[:, :d], cb[:, d:], dtb, alog, dexp, ng)
    return y, ht


def _ssd_sample_kernel(x_ref, bc_ref, z_ref, dtr_ref, s0x, s0bc, s1x, s1bc, s2x, s2bc, h0_ref, cwx_ref,
                       cwbc_ref, cbx_ref, cbbc_ref, dtb_ref, alog_ref, dexp_ref, ng_ref,
                       y_ref, ht_ref, padx, padbc, padd, pada, y_s):
    bt = DEC_TILE
    hd = SSM_HEAD_DIM
    gs = SSM_STATE
    d = x_ref.shape[1]
    n_chunk = d // LANES
    hpg = d // (SSM_GROUPS * hd)
    gw = hpg * hd
    n_bc = SSM_GROUPS * gs

    @pl.when(pl.program_id(0) == 0)
    def _():
        padx[...] = jnp.zeros(padx.shape, F32)
        padbc[...] = jnp.zeros(padbc.shape, F32)
        padd[...] = jnp.zeros(padd.shape, F32)
        pada[...] = jnp.zeros(pada.shape, F32)

    def conv(s0, s1, s2, u, w_ref, b_ref):
        acc = s0[...] * w_ref[0:1, :] + s1[...] * w_ref[1:2, :] + s2[...] * w_ref[2:3, :] + u[...] * w_ref[3:4, :]
        return _silu(acc + b_ref[...])

    padx[0:bt, :] = conv(s0x, s1x, s2x, x_ref, cwx_ref, cbx_ref)
    padbc[0:bt, :] = conv(s0bc, s1bc, s2bc, bc_ref, cwbc_ref, cbbc_ref)
    dt = _softplus(dtr_ref[...] + dtb_ref[...])
    padd[0:bt, :] = dt
    pada[0:bt, :] = jnp.exp(dt * (-jnp.exp(alog_ref[...])))
    dt_t = padd[...].T
    da_t = pada[...].T
    wide = bt * gs
    row_w = lax.broadcasted_iota(jnp.int32, (LANES, wide), 0)
    blk_w = lax.broadcasted_iota(jnp.int32, (LANES, wide), 1) // gs
    row_t = lax.broadcasted_iota(jnp.int32, (wide, LANES), 0) // gs
    lane_t = lax.broadcasted_iota(jnp.int32, (wide, LANES), 1)

    bbig = cbig_t = None
    for ci in range(n_chunk):
        g = (2 * ci) // hpg
        if (2 * ci) % hpg == 0:
            bg = padbc[:, g * gs:(g + 1) * gs]
            bbig = jnp.where(row_w == blk_w, jnp.concatenate([bg] * bt, axis=1), 0.0).astype(BF16)
            cg_t = padbc[:, n_bc + g * gs:n_bc + (g + 1) * gs].T
            cbig_t = jnp.where(row_t == lane_t, jnp.concatenate([cg_t] * bt, axis=0), 0.0).astype(BF16)
        csl = slice(ci * LANES, (ci + 1) * LANES)
        x_t = padx[:, csl].T
        dt_rows = jnp.concatenate([jnp.broadcast_to(dt_t[2 * ci:2 * ci + 1, :], (hd, LANES)),
                                   jnp.broadcast_to(dt_t[2 * ci + 1:2 * ci + 2, :], (hd, LANES))], axis=0)
        inc = jnp.dot((x_t * dt_rows).astype(BF16), bbig, preferred_element_type=F32)
        hs = []
        for k in range(bt):
            da_rows = jnp.concatenate(
                [jnp.broadcast_to(da_t[2 * ci:2 * ci + 1, k:k + 1], (hd, gs)),
                 jnp.broadcast_to(da_t[2 * ci + 1:2 * ci + 2, k:k + 1], (hd, gs))], axis=0)
            hn = h0_ref[k, csl, :] * da_rows + inc[:, k * gs:(k + 1) * gs]
            ht_ref[k, csl, :] = hn
            hs.append(hn.astype(BF16))
        y_t = jnp.dot(jnp.concatenate(hs, axis=1), cbig_t, preferred_element_type=F32)
        y_s[:, csl] = y_t.T[0:bt, :]

    for g in range(SSM_GROUPS):
        gsl = slice(g * gw, (g + 1) * gw)
        y_ref[:, gsl] = _gated_group_norm(y_s[:, gsl], padx[0:bt, gsl], z_ref[:, gsl], dexp_ref[:, gsl],
                                          ng_ref[:, gsl])


def _ssd_sample(p2, dtr2, conv_state2d, h0_all, layer, cw, cb, dtb, alog, dexp, ng, d):
    db = p2.shape[0]
    bt = DEC_TILE
    hp = h0_all.shape[2]
    row = lambda off: pl.BlockSpec((bt, d), lambda i: (i, off))
    const = lambda shape: pl.BlockSpec(shape, lambda i: (0, 0))
    y, ht = pl.pallas_call(
        _ssd_sample_kernel,
        grid=(db // bt,),
        in_specs=[row(7), row(8), row(6), pl.BlockSpec((bt, LANES), lambda i: (i, 0)),
                  row(0), row(1), row(2), row(3), row(4), row(5),
                  pl.BlockSpec((None, bt, hp, SSM_STATE), lambda i: (layer, i, 0, 0)),
                  const((SSM_CONV, d)), const((SSM_CONV, d)), const((1, d)), const((1, d)),
                  const((1, LANES)), const((1, LANES)), const((1, d)), const((1, d))],
        out_specs=[pl.BlockSpec((bt, d), lambda i: (i, 0)),
                   pl.BlockSpec((bt, hp, SSM_STATE), lambda i: (i, 0, 0))],
        out_shape=[jax.ShapeDtypeStruct((db, d), F32),
                   jax.ShapeDtypeStruct((db, hp, SSM_STATE), F32)],
        scratch_shapes=[pltpu.VMEM((LANES, d), F32), pltpu.VMEM((LANES, d), F32),
                        pltpu.VMEM((LANES, LANES), F32), pltpu.VMEM((LANES, LANES), F32),
                        pltpu.VMEM((bt, d), F32)],
        compiler_params=_cparams("arbitrary"),
        name="ssd_sample",
    )(p2, p2, p2, dtr2, conv_state2d, conv_state2d, conv_state2d, conv_state2d, conv_state2d,
      conv_state2d, h0_all, cw[:, :d], cw[:, d:], cb[:, :d], cb[:, d:], dtb, alog, dexp, ng)
    return y, ht


def _t5_bucket_np(dist):
    d = np.asarray(dist, np.int32)
    max_exact = N_BUCKETS // 2
    df = np.maximum(d, 1).astype(np.float32)
    large = max_exact + (np.log(df / np.float32(max_exact)) / np.float32(math.log(MAX_DISTANCE / max_exact))
                         * np.float32(N_BUCKETS - max_exact)).astype(np.int32)
    large = np.minimum(large, N_BUCKETS - 1)
    return np.where(d < max_exact, d, large).astype(np.int32)


def _bias_index_tables():
    blk = ATT_BLOCK
    qi = np.arange(blk)[:, None]
    kj = np.arange(2 * blk)[None, :]
    sdist = qi + blk - kj
    idx_p, idx_d = [], []
    for window, dil in ATT_GROUPS:
        nk = window // dil
        band = (sdist >= 0) & (sdist <= nk)
        idx_p.append(np.where(band, _t5_bucket_np(np.clip(sdist, 0, nk) * dil), -1))
        steps = nk - np.arange(nk)
        idx_d.append(np.broadcast_to(_t5_bucket_np(steps * dil)[:, None], (nk, LANES)))
    return np.stack(idx_p).astype(np.int32), np.stack(idx_d).astype(np.int32)


def _bias_kernel(tab_ref, idxp_ref, idxd_ref, bp_ref, bd_ref):
    h = pl.program_id(0)
    ip = idxp_ref[...]
    idd = idxd_ref[...]
    accp = jnp.full(ip.shape, NEG_INF, F32)
    accd = jnp.full(idd.shape, NEG_INF, F32)
    for b in range(N_BUCKETS):
        v = tab_ref[b, h]
        accp = jnp.where(ip == b, v, accp)
        accd = jnp.where(idd == b, v, accd)
    bp_ref[...] = accp
    bd_ref[...] = accd


def _build_bias(rel_bias):
    idx_p, idx_d = _bias_index_tables()
    n_heads = rel_bias.shape[1]
    blk = ATT_BLOCK
    nk = idx_d.shape[1]
    return pl.pallas_call(
        _bias_kernel,
        grid=(n_heads,),
        in_specs=[pl.BlockSpec(memory_space=pltpu.SMEM),
                  pl.BlockSpec((None, blk, 2 * blk), lambda h: (h // ATT_HPG, 0, 0)),
                  pl.BlockSpec((None, nk, LANES), lambda h: (h // ATT_HPG, 0, 0))],
        out_specs=[pl.BlockSpec((None, blk, 2 * blk), lambda h: (h, 0, 0)),
                   pl.BlockSpec((None, nk, LANES), lambda h: (h, 0, 0))],
        out_shape=[jax.ShapeDtypeStruct((n_heads, blk, 2 * blk), F32),
                   jax.ShapeDtypeStruct((n_heads, nk, LANES), F32)],
        compiler_params=_cparams("arbitrary"),
        name="t5_bias",
    )(rel_bias, jnp.asarray(idx_p), jnp.asarray(idx_d))


ATT_TILE = ATT_BLOCK * max(dil for _, dil in ATT_GROUPS)
MERGE_ROWS = 256


def _merge_three(parts):
    m_all = jnp.maximum(jnp.maximum(parts[0][1], parts[1][1]), parts[2][1])
    w = jnp.exp(parts[0][1] - m_all)
    num = w * parts[0][0]
    den = w * parts[0][2]
    for o, m, l in parts[1:]:
        w = jnp.exp(m - m_all)
        num = num + w * o
        den = den + w * l
    return num / den


def _attn_prompt_kernel(*refs):
    ng = len(ATT_GROUPS)
    blk = ATT_BLOCK
    bias_ref = refs[5 * ng]
    y_ref = refs[5 * ng + 1]
    scr = refs[5 * ng + 2:]
    first_tile = pl.program_id(1) == 0

    def rows_of(start, dil):
        return pl.ds(start, blk) if dil == 1 else pl.ds(start, blk, stride=dil)

    for gi, (window, dil) in enumerate(ATT_GROUPS):
        q_ref, kc_ref, vc_ref, kp_ref, vp_ref = refs[5 * gi:5 * gi + 5]
        o_s, m_s, l_s = scr[3 * gi:3 * gi + 3]
        span = blk * dil
        bias = bias_ref[gi]
        for j in range(ATT_TILE // span):
            for r in range(dil):
                rows = rows_of(j * span + r, dil)
                q = q_ref[rows, :]
                if j == 0:
                    prows = rows_of(r, dil)
                    s_prev = _bdot_nt(q, kp_ref[prows, :]) * ATT_SCALE + bias[:, :blk]
                    s_prev = jnp.where(first_tile, NEG_INF, s_prev)
                    v_prev = vp_ref[prows, :]
                else:
                    prows = rows_of((j - 1) * span + r, dil)
                    s_prev = _bdot_nt(q, kc_ref[prows, :]) * ATT_SCALE + bias[:, :blk]
                    v_prev = vc_ref[prows, :]
                s_cur = _bdot_nt(q, kc_ref[rows, :]) * ATT_SCALE + bias[:, blk:]
                m = jnp.maximum(jnp.max(s_prev, axis=-1, keepdims=True), jnp.max(s_cur, axis=-1, keepdims=True))
                p_prev = jnp.exp(s_prev - m)
                p_cur = jnp.exp(s_cur - m)
                l = jnp.sum(p_prev, axis=-1, keepdims=True) + jnp.sum(p_cur, axis=-1, keepdims=True)
                o_s[rows, :] = _bdot(p_prev, v_prev) + _bdot(p_cur, vc_ref[rows, :])
                m_s[rows, :] = jnp.broadcast_to(m, (blk, ATT_HEAD_DIM))
                l_s[rows, :] = jnp.broadcast_to(l, (blk, ATT_HEAD_DIM))

    for c in range(ATT_TILE // MERGE_ROWS):
        rs = slice(c * MERGE_ROWS, (c + 1) * MERGE_ROWS)
        parts = [(scr[3 * gi][rs, :], scr[3 * gi + 1][rs, :], scr[3 * gi + 2][rs, :]) for gi in range(ng)]
        y_ref[rs, :] = _merge_three(parts).astype(y_ref.dtype)


def _attn_prompt(qkv, bias_p, bsz, s):
    hdim = ATT_HEAD_DIM
    ng = len(ATT_GROUPS)
    n_heads = qkv.shape[1] // (3 * hdim)
    tile = ATT_TILE
    assert s % tile == 0
    nt = s // tile
    in_specs, args = [], []
    for gi, (window, dil) in enumerate(ATT_GROUPS):
        span = ATT_BLOCK * dil
        per = tile // span

        def cur(part, gi=gi):
            return pl.BlockSpec((tile, hdim), lambda b, i, h: (b * nt + i, part * n_heads + gi * ATT_HPG + h))

        def prev(part, gi=gi, per=per):
            return pl.BlockSpec((span, hdim),
                                lambda b, i, h: (jnp.maximum((b * nt + i) * per - 1, 0),
                                                 part * n_heads + gi * ATT_HPG + h))

        in_specs += [cur(0), cur(1), cur(2), prev(1), prev(2)]
        args += [qkv] * 5
    in_specs.append(pl.BlockSpec((ng, None, ATT_BLOCK, 2 * ATT_BLOCK), lambda b, i, h: (0, h, 0, 0)))
    args.append(bias_p)
    return pl.pallas_call(
        _attn_prompt_kernel,
        grid=(bsz, nt, ATT_HPG),
        in_specs=in_specs,
        out_specs=pl.BlockSpec((tile, hdim), lambda b, i, h: (b * nt + i, h)),
        out_shape=jax.ShapeDtypeStruct((bsz * s, ATT_HPG * hdim), BF16),
        scratch_shapes=[pltpu.VMEM((tile, hdim), F32)] * (3 * ng),
        compiler_params=_cparams("parallel", "parallel", "arbitrary"),
        name="attn_prompt",
    )(*args)


REDUCE_WAYS = 4
DEC_KEYS = 32


def _reduce_rows(x, op):
    n = x.shape[0]
    part = op(x.reshape((REDUCE_WAYS, n // REDUCE_WAYS) + x.shape[1:]), axis=1)
    return op(part, axis=0)


def _attn_sample_kernel(qkv_ref, c0_ref, c1_ref, c2_ref, bias_ref, bias0_ref, y_ref):
    ng = len(ATT_GROUPS)
    hp = ATT_HPG
    caches = (c0_ref, c1_ref, c2_ref)
    zeros = jnp.zeros((hp, ATT_HEAD_DIM), F32)
    ones = jnp.ones((hp, ATT_HEAD_DIM), F32)

    def body(b, carry):
        parts = []
        for gi in range(ng):
            q = qkv_ref[b, gi]
            k_new = qkv_ref[b, ng + gi]
            v_new = qkv_ref[b, 2 * ng + gi]
            q2 = jnp.concatenate([q * ATT_SCALE, zeros], axis=0)
            s_self = jnp.sum(k_new * q, axis=-1, keepdims=True) * ATT_SCALE + bias0_ref[gi]
            m = jnp.concatenate([s_self, zeros], axis=0)
            l = jnp.concatenate([ones, ones], axis=0)
            o = jnp.concatenate([zeros, v_new], axis=0)
            for c in range(caches[gi].shape[1] // DEC_KEYS):
                ks = slice(c * DEC_KEYS, (c + 1) * DEC_KEYS)
                kv = caches[gi][b, ks]
                s = jnp.sum(kv * q2[None], axis=-1, keepdims=True) + bias_ref[gi, ks]
                m_new = jnp.maximum(m, _reduce_rows(s, jnp.max))
                alpha = jnp.exp(m - m_new)
                p = jnp.exp(s - m_new[None])
                l = alpha * l + _reduce_rows(p, jnp.sum)
                p_on_v = pltpu.roll(p, hp, 1)
                o = pltpu.roll(alpha, hp, 0) * o + _reduce_rows(p_on_v * kv, jnp.sum)
                m = m_new
            parts.append((o[hp:], m[:hp], l[:hp]))
        y_ref[b] = _merge_three(parts)
        return carry

    lax.fori_loop(0, DEC_TILE, body, 0)


def _attn_sample(qkv2, caches, layer, bias_d, bias0):
    db = qkv2.shape[0]
    bt = DEC_TILE
    hdim = ATT_HEAD_DIM
    ng = len(ATT_GROUPS)
    views, specs = [], []
    for (window, dil), cache in zip(ATT_GROUPS, caches):
        nk = window // dil
        depth, _, wb = cache.shape[:3]
        assert wb == window, "cached window must equal the group's window"
        views.append(cache.reshape(depth, db, nk, dil, 2 * ATT_HPG, hdim))
        specs.append(pl.BlockSpec((None, bt, nk, None, 2 * ATT_HPG, hdim),
                                  lambda i: (layer, i, 0, 0, 0, 0)))
    y = pl.pallas_call(
        _attn_sample_kernel,
        grid=(db // bt,),
        in_specs=[pl.BlockSpec((bt, 3 * ng, ATT_HPG, hdim), lambda i: (i, 0, 0, 0))] + specs +
                 [pl.BlockSpec(bias_d.shape, lambda i: (0, 0, 0, 0)),
                  pl.BlockSpec(bias0.shape, lambda i: (0, 0, 0))],
        out_specs=pl.BlockSpec((bt, ATT_HPG, hdim), lambda i: (i, 0, 0)),
        out_shape=jax.ShapeDtypeStruct((db, ATT_HPG, hdim), F32),
        compiler_params=_cparams("parallel"),
        name="attn_sample",
    )(qkv2.reshape(db, 3 * ng, ATT_HPG, hdim), *views, bias_d, bias0)
    return y.reshape(db, ATT_HPG * hdim)


def _branch_kernel(ysc_ref, yssm_ref, yatt_ref, ysc_s, yssm_s, yatt_s, wsc_ref, wssm_ref, watt_ref,
                   g0_ref, g1_ref, g2_ref, g0_s, g1_s, g2_s, o_ref, os_ref):
    def merged(ysc, yssm, yatt, g0, g1, g2):
        a = jnp.dot(ysc[...], wsc_ref[...], preferred_element_type=F32)
        b = jnp.dot(yssm[...], wssm_ref[...], preferred_element_type=F32)
        c = jnp.dot(yatt[...], watt_ref[...], preferred_element_type=F32)
        return (_sigmoid(g0[...]) * a + _sigmoid(g1[...]) * b + _sigmoid(g2[...]) * c).astype(o_ref.dtype)

    o_ref[...] = merged(ysc_ref, yssm_ref, yatt_ref, g0_ref, g1_ref, g2_ref)

    @pl.when(pl.program_id(1) == 0)
    def _():
        os_ref[...] = merged(ysc_s, yssm_s, yatt_s, g0_s, g1_s, g2_s)


def _branch_merge(ys, ys_s, wsc, wssm, watt, layer, p2, p2_s, tm, tn):
    m, d = ys[0].shape
    ms = ys_s[0].shape[0]
    da = ys[2].shape[1]
    nj = d // tn
    rows = lambda width: pl.BlockSpec((tm, width), lambda j, i: (i, 0))
    rows_s = lambda width: pl.BlockSpec((ms, width), lambda j, i: (0, 0))
    wcol = lambda k: pl.BlockSpec((None, k, tn), lambda j, i: (layer, 0, j))
    gate = lambda br: pl.BlockSpec((tm, tn), lambda j, i: (i, br * nj + j))
    gate_s = lambda br: pl.BlockSpec((ms, tn), lambda j, i: (0, br * nj + j))
    return pl.pallas_call(
        _branch_kernel,
        grid=(nj, m // tm),
        in_specs=[rows(d), rows(d), rows(da), rows_s(d), rows_s(d), rows_s(da), wcol(d), wcol(d), wcol(da),
                  gate(0), gate(1), gate(2), gate_s(0), gate_s(1), gate_s(2)],
        out_specs=[pl.BlockSpec((tm, tn), lambda j, i: (i, j)), pl.BlockSpec((ms, tn), lambda j, i: (0, j))],
        out_shape=[jax.ShapeDtypeStruct((m, d), BF16), jax.ShapeDtypeStruct((ms, d), BF16)],
        compiler_params=_cparams("parallel", "arbitrary"),
        name="branch_merge",
    )(*ys, *ys_s, wsc, wssm, watt, p2, p2, p2, p2_s, p2_s, p2_s)


def _outproj_kernel(x_ref, mg_ref, x_s, mg_s, w_ref, g_ref, x1_ref, h2_ref, x1_s, h2_s):
    def project(x, mg, x1_out, h2_out):
        x1 = x[...] + jnp.dot(mg[...], w_ref[...], preferred_element_type=F32)
        x1_out[...] = x1
        ms = jnp.mean(x1 * x1, axis=-1, keepdims=True)
        h2_out[...] = (x1 * lax.rsqrt(ms + EPS) * g_ref[...]).astype(h2_out.dtype)

    project(x_ref, mg_ref, x1_ref, h2_ref)

    @pl.when(pl.program_id(0) == 0)
    def _():
        project(x_s, mg_s, x1_s, h2_s)


def _outproj(x2d, merged, x2d_s, merged_s, w_out, layer, g2, tm):
    m, d = x2d.shape
    ms = x2d_s.shape[0]
    rows = pl.BlockSpec((tm, d), lambda i: (i, 0))
    rows_s = pl.BlockSpec((ms, d), lambda i: (0, 0))
    return pl.pallas_call(
        _outproj_kernel,
        grid=(m // tm,),
        in_specs=[rows, rows, rows_s, rows_s, pl.BlockSpec((None, d, d), lambda i: (layer, 0, 0)),
                  pl.BlockSpec((1, d), lambda i: (0, 0))],
        out_specs=[rows, rows, rows_s, rows_s],
        out_shape=[jax.ShapeDtypeStruct((m, d), F32), jax.ShapeDtypeStruct((m, d), BF16),
                   jax.ShapeDtypeStruct((ms, d), F32), jax.ShapeDtypeStruct((ms, d), BF16)],
        compiler_params=_cparams("arbitrary"),
        name="outproj_norm",
    )(x2d, merged, x2d_s, merged_s, w_out, g2.reshape(1, d))


def _mlp_kernel(x1_ref, h2_ref, x1_s, h2_s, wu_ref, wd_ref, o_ref, os_ref):
    def accumulate(x1, h2, out):
        @pl.when(pl.program_id(1) == 0)
        def _():
            out[...] = x1[...]

        u = jnp.dot(h2[...], wu_ref[...], preferred_element_type=F32)
        u = jnp.square(jnp.maximum(u, 0.0)).astype(BF16)
        out[...] += jnp.dot(u, wd_ref[...], preferred_element_type=F32)

    accumulate(x1_ref, h2_ref, o_ref)

    @pl.when(pl.program_id(0) == 0)
    def _():
        accumulate(x1_s, h2_s, os_ref)


def _mlp(x1, h2, x1_s, h2_s, w_up, w_down, layer, tm, tf):
    m, d = x1.shape
    ms = x1_s.shape[0]
    f = w_up.shape[2]
    rows = pl.BlockSpec((tm, d), lambda i, j: (i, 0))
    rows_s = pl.BlockSpec((ms, d), lambda i, j: (0, 0))
    return pl.pallas_call(
        _mlp_kernel,
        grid=(m // tm, f // tf),
        in_specs=[rows, rows, rows_s, rows_s, pl.BlockSpec((None, d, tf), lambda i, j: (layer, 0, j)),
                  pl.BlockSpec((None, tf, d), lambda i, j: (layer, j, 0))],
        out_specs=[rows, rows_s],
        out_shape=[jax.ShapeDtypeStruct((m, d), F32), jax.ShapeDtypeStruct((ms, d), F32)],
        compiler_params=_cparams("arbitrary", "arbitrary"),
        name="mlp",
    )(x1, h2, x1_s, h2_s, w_up, w_down)


def _row_tile(m, pref):
    return pref if m % pref == 0 else m


def _pad_lanes(v):
    return jnp.pad(v.astype(F32), (0, LANES - v.shape[0])).reshape(1, LANES)


def _prep_layer(l, shared, norm1_g, sc_conv_w, ssm_conv_w, ssm_conv_b, ssm_dt_bias, ssm_A_log, ssm_D,
                ssm_norm_g, q_norm_g, k_norm_g, norm2_g):
    wt_in = shared["wt_in"]
    d = wt_in.shape[2]
    heads = d // SSM_HEAD_DIM
    aw = ATT_HPG * len(ATT_GROUPS) * ATT_HEAD_DIM
    n_main = N_BRANCH * d + 3 * d + d + (d + 2 * SSM_GROUPS * SSM_STATE)
    assert n_main % LANES == 0 and heads <= LANES
    n_att_heads = aw // ATT_HEAD_DIM
    gains = jnp.concatenate([jnp.tile(q_norm_g[l], n_att_heads), jnp.tile(k_norm_g[l], n_att_heads),
                             jnp.ones((aw,), F32)]).reshape(1, 3 * aw)
    return dict(
        shared, layer=l, n_main=n_main, qkv_row=n_main + heads, n_qkv=3 * aw,
        norm1_g=norm1_g[l], norm2_g=norm2_g[l], qk_gains=gains,
        sc_conv_w=sc_conv_w[l], ssm_conv_w=ssm_conv_w[l], ssm_conv_b=ssm_conv_b[l].reshape(1, -1),
        dt_bias=_pad_lanes(ssm_dt_bias[l]), a_log=_pad_lanes(ssm_A_log[l]),
        d_exp=jnp.repeat(ssm_D[l].astype(F32), SSM_HEAD_DIM).reshape(1, d),
        ssm_norm_g=ssm_norm_g[l].reshape(1, d),
    )


def _in_proj(x2d, x2d_s, w):
    m = x2d.shape[0]
    h = _rmsnorm(x2d, w["norm1_g"], _row_tile(m, 512))
    h_s = _rmsnorm(x2d_s, w["norm1_g"], x2d_s.shape[0])
    p = _matmul_wcast(h, h_s, w["wt_in"], w["layer"], 0, w["n_main"], MXU_COLS, name="in_proj_main")
    dtr = _matmul_wcast(h, h_s, w["wt_in"], w["layer"], w["n_main"] // LANES, LANES, LANES, name="in_proj_dt")
    qkv = _matmul_qkv(h, h_s, w["wt_in"], w["layer"], w["qkv_row"], w["n_qkv"], w["qk_gains"], MXU_COLS)
    return (p[0], dtr[0], qkv[0]), (p[1], dtr[1], qkv[1])


def _tail(x2d, p, ys, x2d_s, p_s, ys_s, w):
    m = x2d.shape[0]
    l = w["layer"]
    merged, merged_s = _branch_merge(ys, ys_s, w["w_br_sc"], w["w_br_ssm"], w["w_br_att"], l, p, p_s,
                                     _row_tile(m, 1024), 512)
    x1, h2, x1_s, h2_s = _outproj(x2d, merged, x2d_s, merged_s, w["w_out"], l, w["norm2_g"], _row_tile(m, 512))
    return _mlp(x1, h2, x1_s, h2_s, w["w_up"], w["w_down"], l, _row_tile(m, 512), 1024)


def _kv_rows(qkv3, gi, keep):
    aw = qkv3.shape[2] // 3
    gw = ATT_HPG * ATT_HEAD_DIM
    bsz, ln, _ = qkv3.shape
    k = qkv3[:, ln - keep:, aw + gi * gw:aw + (gi + 1) * gw].reshape(bsz, keep, ATT_HPG, ATT_HEAD_DIM)
    v = qkv3[:, ln - keep:, 2 * aw + gi * gw:2 * aw + (gi + 1) * gw].reshape(bsz, keep, ATT_HPG, ATT_HEAD_DIM)
    return jnp.stack([k, v], axis=2)


def _prompt_mixers(proj, w, bias_p, bsz, s, d):
    p, dtr, qkv = proj
    p3 = p.reshape(bsz, s, -1)
    qkv3 = qkv.reshape(bsz, s, -1)
    ysc, sc_new = _sc_prompt(p3, w["sc_conv_w"], d, 1024, 1024)
    yssm, ht = _ssd_prompt(p3, dtr.reshape(bsz, s, LANES), w["ssm_conv_w"], w["ssm_conv_b"], w["dt_bias"],
                           w["a_log"], w["d_exp"], w["ssm_norm_g"], d)
    xbc_off = N_BRANCH * d + 3 * d + d
    ssm_conv_new = p3[:, s - (SSM_CONV - 1):, xbc_off:]
    kv_new = [_kv_rows(qkv3, gi, min(window, s)) for gi, (window, _) in enumerate(ATT_GROUPS)]
    yatt = _attn_prompt(qkv, bias_p, bsz, s)
    heads = d // SSM_HEAD_DIM
    st = (sc_new, ssm_conv_new, ht.reshape(bsz, heads, SSM_HEAD_DIM, SSM_STATE), kv_new[0], kv_new[1], kv_new[2])
    return (ysc.reshape(bsz * s, d), yssm.reshape(bsz * s, d), yatt), st


def _sample_mixers(proj, w, bias_d, bias0, sc_buf, ssm_buf, ssm_h0_all, kv_caches, d):
    p, dtr, qkv = proj
    db = p.shape[0]
    l = w["layer"]
    ysc, u_new = _sc_sample(p, sc_buf.reshape(db, -1), w["sc_conv_w"], d, 512)
    sc_new = jnp.concatenate([sc_buf[:, 1:], u_new[:, None, :]], axis=1)
    heads = d // SSM_HEAD_DIM
    depth = ssm_h0_all.shape[0]
    yssm, ht = _ssd_sample(p, dtr, ssm_buf.reshape(db, -1),
                           ssm_h0_all.reshape(depth, db, heads * SSM_HEAD_DIM, SSM_STATE), l,
                           w["ssm_conv_w"], w["ssm_conv_b"], w["dt_bias"], w["a_log"], w["d_exp"],
                           w["ssm_norm_g"], d)
    xbc_off = N_BRANCH * d + 3 * d + d
    ssm_conv_new = jnp.concatenate([ssm_buf[:, 1:], p[:, None, xbc_off:]], axis=1)
    yatt = _attn_sample(qkv, kv_caches, l, bias_d, bias0)
    qkv3 = qkv.reshape(db, 1, -1)
    kv_new = [_kv_rows(qkv3, gi, 1) for gi in range(len(ATT_GROUPS))]
    st = (sc_new, ssm_conv_new, ht.reshape(db, heads, SSM_HEAD_DIM, SSM_STATE), kv_new[0], kv_new[1], kv_new[2])
    return (ysc.astype(BF16), yssm.astype(BF16), yatt.astype(BF16)), st


def kernel(x_prompt, x_sample, state_sc_conv, state_ssm_conv, state_ssm, cache_kv_w128, cache_kv_w512,
           cache_kv_w2048, norm1_g, w_in, sc_conv_w, ssm_conv_w, ssm_conv_b, ssm_dt_bias, ssm_A_log, ssm_D,
           ssm_norm_g, q_norm_g, k_norm_g, rel_bias, w_br_sc, w_br_ssm, w_br_att, w_out, norm2_g, w_up, w_down):
    depth = w_in.shape[0]
    ng = len(ATT_GROUPS)
    bias_p, bias_d = _build_bias(rel_bias)
    bias_p = bias_p.reshape((ng, ATT_HPG) + bias_p.shape[1:])
    bias_d = bias_d.reshape((ng, ATT_HPG) + bias_d.shape[1:]).transpose(0, 2, 1, 3)
    bias_d = jnp.concatenate([bias_d, jnp.zeros_like(bias_d)], axis=2)
    bias0 = jnp.broadcast_to(rel_bias[0].reshape(ng, ATT_HPG, 1), (ng, ATT_HPG, LANES))
    bsz, s, d = x_prompt.shape
    db, t, _ = x_sample.shape
    assert t == 1, "decode kernels handle one new token per sample"
    xp, xs = x_prompt.reshape(bsz * s, d), x_sample.reshape(db, d)
    p_new, s_new = [], []
    shared = dict(wt_in=jnp.swapaxes(w_in, 1, 2),
                  w_br_sc=w_br_sc.astype(BF16), w_br_ssm=w_br_ssm.astype(BF16), w_br_att=w_br_att.astype(BF16),
                  w_out=w_out.astype(BF16), w_up=w_up.astype(BF16), w_down=w_down.astype(BF16))
    for l in range(depth):
        w = _prep_layer(l, shared, norm1_g, sc_conv_w, ssm_conv_w, ssm_conv_b, ssm_dt_bias, ssm_A_log, ssm_D,
                        ssm_norm_g, q_norm_g, k_norm_g, norm2_g)
        proj, proj_s = _in_proj(xp, xs, w)
        ys, st = _prompt_mixers(proj, w, bias_p, bsz, s, d)
        p_new.append(st)
        ys_s, st = _sample_mixers(proj_s, w, bias_d, bias0, state_sc_conv[l], state_ssm_conv[l], state_ssm,
                                  (cache_kv_w128, cache_kv_w512, cache_kv_w2048), d)
        s_new.append(st)
        xp, xs = _tail(xp, proj[0], ys, xs, proj_s[0], ys_s, w)
    p_out = [jnp.stack(a) for a in zip(*p_new)]
    s_out = [jnp.stack(a) for a in zip(*s_new)]
    return (xp.reshape(bsz, s, d), xs.reshape(db, 1, d), *p_out, *s_out)
```

```python
import functools
import math

import numpy as np
import jax
import jax.numpy as jnp
from jax import lax
from jax.experimental import pallas as pl
from jax.experimental.pallas import tpu as pltpu

F32 = jnp.float32
BF16 = jnp.bfloat16

N_BRANCH = 3
SC_CONV = 3
SSM_HEAD_DIM = 64
SSM_GROUPS = 8
SSM_STATE = 128
SSM_CONV = 4
SSM_CHUNK = 128
ATT_HEAD_DIM = 128
ATT_GROUPS = ((128, 1), (512, 4), (2048, 16))
ATT_HPG = 4
ATT_BLOCK = 128
ATT_SCALE = 1.0 / math.sqrt(ATT_HEAD_DIM)
N_BUCKETS = 32
MAX_DISTANCE = 2048
EPS = 1e-6

LANES = 128
SUBLANES = 8
MXU_COLS = 256
VMEM_LIMIT_BYTES = 60 * 1024 * 1024

DEC_TILE = 8
NEG_INF = float("-inf")


def _cparams(*sem):
    return pltpu.CompilerParams(dimension_semantics=sem, vmem_limit_bytes=VMEM_LIMIT_BYTES)


def _sigmoid(v):
    return 1.0 / (1.0 + jnp.exp(-v))


def _silu(v):
    return v * _sigmoid(v)


def _bdot(a, b):
    return jnp.dot(a.astype(BF16), b.astype(BF16), preferred_element_type=F32)


def _bdot_nt(a, b):
    return lax.dot_general(a.astype(BF16), b.astype(BF16), (((1,), (1,)), ((), ())),
                           preferred_element_type=F32)


def _rmsnorm_kernel(x_ref, g_ref, o_ref):
    x = x_ref[...]
    ms = jnp.mean(x * x, axis=-1, keepdims=True)
    o_ref[...] = (x * lax.rsqrt(ms + EPS) * g_ref[...]).astype(o_ref.dtype)


def _rmsnorm(x2d, g, tm):
    m, d = x2d.shape
    return pl.pallas_call(
        _rmsnorm_kernel,
        grid=(m // tm,),
        in_specs=[pl.BlockSpec((tm, d), lambda i: (i, 0)),
                  pl.BlockSpec((1, d), lambda i: (0, 0))],
        out_specs=pl.BlockSpec((tm, d), lambda i: (i, 0)),
        out_shape=jax.ShapeDtypeStruct((m, d), BF16),
        compiler_params=_cparams("parallel"),
        name="rmsnorm",
    )(x2d, g.reshape(1, d))


ROW_CHUNK = 1024
NORM_ROWS = 256
FF_CHUNK = 512


def _resident_rows_dot(a_ref, as_ref, w_ref, wb_ref, emit):
    wb_ref[...] = w_ref[...].astype(BF16)
    m = a_ref.shape[0]
    rc = min(m, ROW_CHUNK)
    for r in range(m // rc):
        rows = slice(r * rc, (r + 1) * rc)
        emit(0, rows, _bdot_nt(a_ref[rows, :], wb_ref[...]))
    emit(1, slice(0, as_ref.shape[0]), _bdot_nt(as_ref[...], wb_ref[...]))


def _mm_wcast_kernel(a_ref, as_ref, w_ref, o_ref, os_ref, wb_ref):
    outs = (o_ref, os_ref)

    def emit(which, rows, acc):
        outs[which][rows, :] = acc

    _resident_rows_dot(a_ref, as_ref, w_ref, wb_ref, emit)


def _resident(shape):
    return pl.BlockSpec(shape, lambda j: (0,) * len(shape), pipeline_mode=pl.Buffered(1))


def _matmul_wcast(a, a_s, wt_all, layer, first_block, n_cols, tn, name):
    m, k = a.shape
    ms = a_s.shape[0]
    assert m % min(m, ROW_CHUNK) == 0
    return pl.pallas_call(
        _mm_wcast_kernel,
        grid=(n_cols // tn,),
        in_specs=[_resident((m, k)), _resident((ms, k)),
                  pl.BlockSpec((None, tn, k), lambda j: (layer, first_block + j, 0))],
        out_specs=[pl.BlockSpec((m, tn), lambda j: (0, j)), pl.BlockSpec((ms, tn), lambda j: (0, j))],
        out_shape=[jax.ShapeDtypeStruct((m, n_cols), F32), jax.ShapeDtypeStruct((ms, n_cols), F32)],
        scratch_shapes=[pltpu.VMEM((tn, k), BF16)],
        compiler_params=_cparams("arbitrary"),
        name=name,
    )(a, a_s, wt_all)


def _mm_qkv_kernel(a_ref, as_ref, w_ref, g_ref, o_ref, os_ref, wb_ref, *, n_norm_blocks):
    is_qk = pl.program_id(0) < n_norm_blocks
    outs = (o_ref, os_ref)

    def emit(which, rows, acc):
        for h in range(acc.shape[1] // ATT_HEAD_DIM):
            sl = slice(h * ATT_HEAD_DIM, (h + 1) * ATT_HEAD_DIM)
            t = acc[:, sl]
            ms = jnp.mean(t * t, axis=-1, keepdims=True)
            outs[which][rows, sl] = t * jnp.where(is_qk, lax.rsqrt(ms + EPS), 1.0) * g_ref[:, sl]

    _resident_rows_dot(a_ref, as_ref, w_ref.at[0], wb_ref, emit)


def _matmul_qkv(a, a_s, wt_all, layer, first_row, n, gains, tn):
    m, k = a.shape
    ms = a_s.shape[0]
    assert first_row % SUBLANES == 0 and tn % ATT_HEAD_DIM == 0
    att_width = n // 3
    return pl.pallas_call(
        functools.partial(_mm_qkv_kernel, n_norm_blocks=2 * att_width // tn),
        grid=(n // tn,),
        in_specs=[_resident((m, k)), _resident((ms, k)),
                  pl.BlockSpec((pl.Element(1), pl.Element(tn), pl.Element(k)),
                               lambda j: (layer, pl.multiple_of(first_row + j * tn, SUBLANES), 0)),
                  pl.BlockSpec((1, tn), lambda j: (0, j))],
        out_specs=[pl.BlockSpec((m, tn), lambda j: (0, j)), pl.BlockSpec((ms, tn), lambda j: (0, j))],
        out_shape=[jax.ShapeDtypeStruct((m, n), F32), jax.ShapeDtypeStruct((ms, n), F32)],
        scratch_shapes=[pltpu.VMEM((tn, k), BF16)],
        compiler_params=_cparams("arbitrary"),
        name="matmul_qkv",
    )(a, a_s, wt_all, gains)


def _sc_prompt_kernel(b_ref, c_ref, x_ref, hc_ref, hx_ref, w_ref, y_ref, tail_ref, ext_ref, *, ts):
    t = pl.program_id(2)
    u = c_ref[...] * x_ref[...]
    hu = hc_ref[...] * hx_ref[...]
    ext_ref[0:SUBLANES, :] = jnp.where(t > 0, hu, 0.0)
    ext_ref[SUBLANES:SUBLANES + ts, :] = u
    u2 = ext_ref[pl.ds(SUBLANES - 2, ts), :]
    u1 = ext_ref[pl.ds(SUBLANES - 1, ts), :]
    y = u2 * w_ref[0:1, :] + u1 * w_ref[1:2, :] + u * w_ref[2:3, :]
    y_ref[...] = (b_ref[...] * y).astype(y_ref.dtype)
    tail_ref[...] = ext_ref[pl.ds(ts, SUBLANES), :]


def _sc_prompt(p3, conv_w, d, ts, tc):
    bsz, s, _ = p3.shape
    ncb = d // tc
    off_b, off_c, off_x = 3 * ncb, 4 * ncb, 5 * ncb
    hb = ts // SUBLANES

    def halo(off):
        return pl.BlockSpec((None, SUBLANES, tc),
                            lambda b, c, t: (b, jnp.maximum(t * hb - 1, 0), off + c))

    def main(off):
        return pl.BlockSpec((None, ts, tc), lambda b, c, t: (b, t, off + c))

    y, tail = pl.pallas_call(
        functools.partial(_sc_prompt_kernel, ts=ts),
        grid=(bsz, ncb, s // ts),
        in_specs=[main(off_b), main(off_c), main(off_x), halo(off_c), halo(off_x),
                  pl.BlockSpec((SC_CONV, tc), lambda b, c, t: (0, c))],
        out_specs=[pl.BlockSpec((None, ts, tc), lambda b, c, t: (b, t, c)),
                   pl.BlockSpec((None, SUBLANES, tc), lambda b, c, t: (b, 0, c))],
        out_shape=[jax.ShapeDtypeStruct((bsz, s, d), BF16),
                   jax.ShapeDtypeStruct((bsz, SUBLANES, d), F32)],
        scratch_shapes=[pltpu.VMEM((ts + SUBLANES, tc), F32)],
        compiler_params=_cparams("parallel", "parallel", "arbitrary"),
        name="short_conv_prompt",
    )(p3, p3, p3, p3, p3, conv_w)
    return y, tail[:, SUBLANES - (SC_CONV - 1):, :]


def _sc_sample_kernel(b_ref, c_ref, x_ref, s0_ref, s1_ref, w_ref, y_ref, u_ref):
    u = c_ref[...] * x_ref[...]
    y = s0_ref[...] * w_ref[0:1, :] + s1_ref[...] * w_ref[1:2, :] + u * w_ref[2:3, :]
    y_ref[...] = b_ref[...] * y
    u_ref[...] = u


def _sc_sample(p2, state2d, conv_w, d, tc):
    db = p2.shape[0]
    ncb = d // tc

    def col(off):
        return pl.BlockSpec((db, tc), lambda c: (0, off + c))

    return pl.pallas_call(
        _sc_sample_kernel,
        grid=(ncb,),
        in_specs=[col(3 * ncb), col(4 * ncb), col(5 * ncb), col(0), col(ncb),
                  pl.BlockSpec((SC_CONV, tc), lambda c: (0, c))],
        out_specs=[col(0), col(0)],
        out_shape=[jax.ShapeDtypeStruct((db, d), F32), jax.ShapeDtypeStruct((db, d), F32)],
        compiler_params=_cparams("parallel"),
        name="short_conv_sample",
    )(p2, p2, p2, state2d, state2d, conv_w)


def _softplus(v):
    return jnp.maximum(v, 0.0) + jnp.log1p(jnp.exp(-jnp.abs(v)))


def _gated_group_norm(y, xs, z, dexp, ng):
    val = (y + dexp * xs) * _silu(z)
    ms = jnp.mean(val * val, axis=-1, keepdims=True)
    return val * lax.rsqrt(ms + EPS) * ng


def _ssd_prompt_kernel(x_ref, bc_ref, z_ref, dtr_ref, cwx_ref, cwbc_ref, cbx_ref, cbbc_ref, dtb_ref,
                       alog_ref, dexp_ref, ng_ref, y_ref, ht_ref, extx, extbc, xs_s, bc_s, state):
    q = SSM_CHUNK
    c = pl.program_id(1)
    pad = SUBLANES
    taps = SSM_CONV

    @pl.when(c == 0)
    def _():
        extx[0:pad, :] = jnp.zeros((pad, extx.shape[1]), F32)
        extbc[0:pad, :] = jnp.zeros((pad, extbc.shape[1]), F32)
        state[...] = jnp.zeros(state.shape, F32)

    extx[pad:pad + q, :] = x_ref[...]
    extbc[pad:pad + q, :] = bc_ref[...]

    def conv(ext, w_ref, b_ref):
        acc = ext[pl.ds(pad - taps + 1, q), :] * w_ref[0:1, :]
        for i in range(1, taps):
            acc = acc + ext[pl.ds(pad - taps + 1 + i, q), :] * w_ref[i:i + 1, :]
        return _silu(acc + b_ref[...])

    xs_s[...] = conv(extx, cwx_ref, cbx_ref)
    bc_s[...] = conv(extbc, cwbc_ref, cbbc_ref)
    extx[0:pad, :] = extx[q:q + pad, :]
    extbc[0:pad, :] = extbc[q:q + pad, :]

    dt = _softplus(dtr_ref[...] + dtb_ref[...])
    a_neg = -jnp.exp(alog_ref[...])
    ii = lax.broadcasted_iota(jnp.int32, (q, q), 0)
    jj = lax.broadcasted_iota(jnp.int32, (q, q), 1)
    causal = ii >= jj
    acum = jnp.dot(causal.astype(F32), dt * a_neg, precision=lax.Precision.HIGHEST,
                   preferred_element_type=F32)
    acum_t = acum.T
    dt_t = dt.T
    lo_half = jj < SSM_HEAD_DIM
    gs = SSM_STATE
    hd = SSM_HEAD_DIM
    hpg = xs_s.shape[1] // (SSM_GROUPS * hd)
    gw = hpg * hd
    n_bc = SSM_GROUPS * gs

    for g in range(SSM_GROUPS):
        bg = bc_s[:, g * gs:(g + 1) * gs].astype(BF16)
        cg = bc_s[:, n_bc + g * gs:n_bc + (g + 1) * gs].astype(BF16)
        cb = _bdot_nt(cg, bg)
        ys = []
        for hp in range(hpg // 2):
            e0 = g * hpg + 2 * hp
            rows = slice(e0 * hd, (e0 + 2) * hd)
            xp = xs_s[:, rows]
            xp16 = xp.astype(BF16)
            cols_a, cols_dt, parts = [], [], []
            for k in range(2):
                e = e0 + k
                col_a = jnp.broadcast_to(acum[:, e:e + 1], (q, q))
                cols_a.append(col_a)
                cols_dt.append(jnp.broadcast_to(dt[:, e:e + 1], (q, q)))
                dec = jnp.exp(jnp.where(causal, col_a - acum_t[e:e + 1, :], NEG_INF))
                wm = cb * dec * dt_t[e:e + 1, :]
                parts.append(jnp.dot(wm.astype(BF16), xp16, preferred_element_type=F32))
            y_diag = jnp.where(lo_half, parts[0], parts[1])
            sel_a = jnp.where(lo_half, cols_a[0], cols_a[1])
            sel_dt = jnp.where(lo_half, cols_dt[0], cols_dt[1])
            hpair = state[rows, :]
            y_off = _bdot_nt(cg, hpair) * jnp.exp(sel_a)
            ys.append(y_diag + y_off)
            last_a = sel_a[q - 1:q, :]
            xw = xp * jnp.exp(last_a - sel_a) * sel_dt
            st = jnp.dot(xw.T.astype(BF16), bg, preferred_element_type=F32)
            cd = jnp.concatenate([jnp.broadcast_to(cols_a[0][q - 1:q, :], (hd, gs)),
                                  jnp.broadcast_to(cols_a[1][q - 1:q, :], (hd, gs))], axis=0)
            state[rows, :] = hpair * jnp.exp(cd) + st
        gsl = slice(g * gw, (g + 1) * gw)
        yg = jnp.concatenate(ys, axis=1)
        y_ref[:, gsl] = _gated_group_norm(yg, xs_s[:, gsl], z_ref[:, gsl], dexp_ref[:, gsl],
                                          ng_ref[:, gsl]).astype(y_ref.dtype)

    @pl.when(c == pl.num_programs(1) - 1)
    def _():
        ht_ref[...] = state[...]


def _ssd_prompt(p3, dtr3, cw, cb, dtb, alog, dexp, ng, d):
    bsz, s, _ = p3.shape
    q = SSM_CHUNK
    heads = d // SSM_HEAD_DIM
    const = lambda shape: pl.BlockSpec(shape, lambda b, c: (0, 0))
    y, ht = pl.pallas_call(
        _ssd_prompt_kernel,
        grid=(bsz, s // q),
        in_specs=[pl.BlockSpec((None, q, d), lambda b, c: (b, c, 7)),
                  pl.BlockSpec((None, q, d), lambda b, c: (b, c, 8)),
                  pl.BlockSpec((None, q, d), lambda b, c: (b, c, 6)),
                  pl.BlockSpec((None, q, LANES), lambda b, c: (b, c, 0)),
                  const((SSM_CONV, d)), const((SSM_CONV, d)), const((1, d)), const((1, d)),
                  const((1, LANES)), const((1, LANES)), const((1, d)), const((1, d))],
        out_specs=[pl.BlockSpec((None, q, d), lambda b, c: (b, c, 0)),
                   pl.BlockSpec((None, heads * SSM_HEAD_DIM, SSM_STATE), lambda b, c: (b, 0, 0))],
        out_shape=[jax.ShapeDtypeStruct((bsz, s, d), BF16),
                   jax.ShapeDtypeStruct((bsz, heads * SSM_HEAD_DIM, SSM_STATE), F32)],
        scratch_shapes=[pltpu.VMEM((q + SUBLANES, d), F32), pltpu.VMEM((q + SUBLANES, d), F32),
                        pltpu.VMEM((q, d), F32), pltpu.VMEM((q, d), F32),
                        pltpu.VMEM((heads * SSM_HEAD_DIM, SSM_STATE), F32)],
        compiler_params=_cparams("parallel", "arbitrary"),
        name="ssd_prompt",
    )(p3, p3, p3, dtr3, cw[:, :d], cw[:, d:], cb[:, :d], cb[:, d:], dtb, alog, dexp, ng)
    return y, ht


def _ssd_sample_kernel(*refs, layer, owns_all_layers):
    (x_ref, bc_ref, z_ref, dtr_ref, s0x, s0bc, s1x, s1bc, s2x, s2bc, h0_ref, cwx_ref, cwbc_ref, cbx_ref,
     cbbc_ref, dtb_ref, alog_ref, dexp_ref, ng_ref) = refs[:19]
    y_ref, ht_all_ref, padx, padbc, padd, pada, y_s = refs[-7:]
    if owns_all_layers:
        ht_ref = ht_all_ref.at[layer]
        for other in range(ht_all_ref.shape[0]):
            if other != layer:
                ht_all_ref[other] = jnp.zeros(ht_all_ref.shape[1:], F32)
    else:
        ht_ref = ht_all_ref
    bt = DEC_TILE
    hd = SSM_HEAD_DIM
    gs = SSM_STATE
    d = x_ref.shape[1]
    n_chunk = d // LANES
    hpg = d // (SSM_GROUPS * hd)
    gw = hpg * hd
    n_bc = SSM_GROUPS * gs

    @pl.when(pl.program_id(0) == 0)
    def _():
        padx[...] = jnp.zeros(padx.shape, F32)
        padbc[...] = jnp.zeros(padbc.shape, F32)
        padd[...] = jnp.zeros(padd.shape, F32)
        pada[...] = jnp.zeros(pada.shape, F32)

    def conv(s0, s1, s2, u, w_ref, b_ref):
        acc = s0[...] * w_ref[0:1, :] + s1[...] * w_ref[1:2, :] + s2[...] * w_ref[2:3, :] + u[...] * w_ref[3:4, :]
        return _silu(acc + b_ref[...])

    padx[0:bt, :] = conv(s0x, s1x, s2x, x_ref, cwx_ref, cbx_ref)
    padbc[0:bt, :] = conv(s0bc, s1bc, s2bc, bc_ref, cwbc_ref, cbbc_ref)
    dt = _softplus(dtr_ref[...] + dtb_ref[...])
    padd[0:bt, :] = dt
    pada[0:bt, :] = jnp.exp(dt * (-jnp.exp(alog_ref[...])))
    dt_t = padd[...].T
    da_t = pada[...].T
    wide = bt * gs
    row_w = lax.broadcasted_iota(jnp.int32, (LANES, wide), 0)
    blk_w = lax.broadcasted_iota(jnp.int32, (LANES, wide), 1) // gs
    row_t = lax.broadcasted_iota(jnp.int32, (wide, LANES), 0) // gs
    lane_t = lax.broadcasted_iota(jnp.int32, (wide, LANES), 1)

    bbig = cbig_t = None
    for ci in range(n_chunk):
        g = (2 * ci) // hpg
        if (2 * ci) % hpg == 0:
            bg = padbc[:, g * gs:(g + 1) * gs]
            bbig = jnp.where(row_w == blk_w, jnp.concatenate([bg] * bt, axis=1), 0.0).astype(BF16)
            cg_t = padbc[:, n_bc + g * gs:n_bc + (g + 1) * gs].T
            cbig_t = jnp.where(row_t == lane_t, jnp.concatenate([cg_t] * bt, axis=0), 0.0).astype(BF16)
        csl = slice(ci * LANES, (ci + 1) * LANES)
        x_t = padx[:, csl].T
        dt_rows = jnp.concatenate([jnp.broadcast_to(dt_t[2 * ci:2 * ci + 1, :], (hd, LANES)),
                                   jnp.broadcast_to(dt_t[2 * ci + 1:2 * ci + 2, :], (hd, LANES))], axis=0)
        inc = jnp.dot((x_t * dt_rows).astype(BF16), bbig, preferred_element_type=F32)
        hs = []
        for k in range(bt):
            da_rows = jnp.concatenate(
                [jnp.broadcast_to(da_t[2 * ci:2 * ci + 1, k:k + 1], (hd, gs)),
                 jnp.broadcast_to(da_t[2 * ci + 1:2 * ci + 2, k:k + 1], (hd, gs))], axis=0)
            hn = h0_ref[k, csl, :] * da_rows + inc[:, k * gs:(k + 1) * gs]
            ht_ref[k, csl, :] = hn
            hs.append(hn.astype(BF16))
        y_t = jnp.dot(jnp.concatenate(hs, axis=1), cbig_t, preferred_element_type=F32)
        y_s[:, csl] = y_t.T[0:bt, :]

    for g in range(SSM_GROUPS):
        gsl = slice(g * gw, (g + 1) * gw)
        y_ref[:, gsl] = _gated_group_norm(y_s[:, gsl], padx[0:bt, gsl], z_ref[:, gsl], dexp_ref[:, gsl],
                                          ng_ref[:, gsl])


def _ssd_sample(p2, dtr2, conv_state2d, h0_all, layer, ht_all, cw, cb, dtb, alog, dexp, ng, d):
    db = p2.shape[0]
    bt = DEC_TILE
    depth, _, hp, _ = h0_all.shape
    first = ht_all is None
    row = lambda off: pl.BlockSpec((bt, d), lambda i: (i, off))
    const = lambda shape: pl.BlockSpec(shape, lambda i: (0, 0))
    in_specs = [row(7), row(8), row(6), pl.BlockSpec((bt, LANES), lambda i: (i, 0)),
                row(0), row(1), row(2), row(3), row(4), row(5),
                pl.BlockSpec((None, bt, hp, SSM_STATE), lambda i: (layer, i, 0, 0)),
                const((SSM_CONV, d)), const((SSM_CONV, d)), const((1, d)), const((1, d)),
                const((1, LANES)), const((1, LANES)), const((1, d)), const((1, d))]
    args = [p2, p2, p2, dtr2, conv_state2d, conv_state2d, conv_state2d, conv_state2d, conv_state2d,
            conv_state2d, h0_all, cw[:, :d], cw[:, d:], cb[:, :d], cb[:, d:], dtb, alog, dexp, ng]
    if first:
        state_out = pl.BlockSpec((depth, bt, hp, SSM_STATE), lambda i: (0, i, 0, 0))
        aliases = {}
    else:
        state_out = pl.BlockSpec((None, bt, hp, SSM_STATE), lambda i: (layer, i, 0, 0))
        in_specs.append(pl.BlockSpec(memory_space=pl.ANY))
        args.append(ht_all)
        aliases = {len(args) - 1: 1}
    y, ht_all = pl.pallas_call(
        functools.partial(_ssd_sample_kernel, layer=layer, owns_all_layers=first),
        grid=(db // bt,),
        in_specs=in_specs,
        out_specs=[pl.BlockSpec((bt, d), lambda i: (i, 0)), state_out],
        out_shape=[jax.ShapeDtypeStruct((db, d), F32),
                   jax.ShapeDtypeStruct((depth, db, hp, SSM_STATE), F32)],
        scratch_shapes=[pltpu.VMEM((LANES, d), F32), pltpu.VMEM((LANES, d), F32),
                        pltpu.VMEM((LANES, LANES), F32), pltpu.VMEM((LANES, LANES), F32),
                        pltpu.VMEM((bt, d), F32)],
        input_output_aliases=aliases,
        compiler_params=_cparams("arbitrary"),
        name="ssd_sample",
    )(*args)
    return y, ht_all


def _t5_bucket_np(dist):
    d = np.asarray(dist, np.int32)
    max_exact = N_BUCKETS // 2
    df = np.maximum(d, 1).astype(np.float32)
    large = max_exact + (np.log(df / np.float32(max_exact)) / np.float32(math.log(MAX_DISTANCE / max_exact))
                         * np.float32(N_BUCKETS - max_exact)).astype(np.int32)
    large = np.minimum(large, N_BUCKETS - 1)
    return np.where(d < max_exact, d, large).astype(np.int32)


def _bias_index_tables():
    blk = ATT_BLOCK
    qi = np.arange(blk)[:, None]
    kj = np.arange(2 * blk)[None, :]
    sdist = qi + blk - kj
    idx_p, idx_d = [], []
    for window, dil in ATT_GROUPS:
        nk = window // dil
        band = (sdist >= 0) & (sdist <= nk)
        idx_p.append(np.where(band, _t5_bucket_np(np.clip(sdist, 0, nk) * dil), -1))
        steps = nk - np.arange(nk)
        idx_d.append(np.broadcast_to(_t5_bucket_np(steps * dil)[:, None], (nk, LANES)))
    return np.stack(idx_p).astype(np.int32), np.stack(idx_d).astype(np.int32)


def _bias_kernel(tab_ref, idxp_ref, idxd_ref, bp_ref, bd_ref):
    h = pl.program_id(0)
    ip = idxp_ref[...]
    idd = idxd_ref[...]
    accp = jnp.full(ip.shape, NEG_INF, F32)
    accd = jnp.full(idd.shape, NEG_INF, F32)
    for b in range(N_BUCKETS):
        v = tab_ref[b, h]
        accp = jnp.where(ip == b, v, accp)
        accd = jnp.where(idd == b, v, accd)
    bp_ref[...] = accp
    bd_ref[...] = accd


def _build_bias(rel_bias):
    idx_p, idx_d = _bias_index_tables()
    n_heads = rel_bias.shape[1]
    blk = ATT_BLOCK
    nk = idx_d.shape[1]
    return pl.pallas_call(
        _bias_kernel,
        grid=(n_heads,),
        in_specs=[pl.BlockSpec(memory_space=pltpu.SMEM),
                  pl.BlockSpec((None, blk, 2 * blk), lambda h: (h // ATT_HPG, 0, 0)),
                  pl.BlockSpec((None, nk, LANES), lambda h: (h // ATT_HPG, 0, 0))],
        out_specs=[pl.BlockSpec((None, blk, 2 * blk), lambda h: (h, 0, 0)),
                   pl.BlockSpec((None, nk, LANES), lambda h: (h, 0, 0))],
        out_shape=[jax.ShapeDtypeStruct((n_heads, blk, 2 * blk), F32),
                   jax.ShapeDtypeStruct((n_heads, nk, LANES), F32)],
        compiler_params=_cparams("arbitrary"),
        name="t5_bias",
    )(rel_bias, jnp.asarray(idx_p), jnp.asarray(idx_d))


ATT_TILE = ATT_BLOCK * max(dil for _, dil in ATT_GROUPS)
MERGE_ROWS = 256
ATT_UNITS = 4


def _merge_three(parts):
    m_all = jnp.maximum(jnp.maximum(parts[0][1], parts[1][1]), parts[2][1])
    w = jnp.exp(parts[0][1] - m_all)
    num = w * parts[0][0]
    den = w * parts[0][2]
    for o, m, l in parts[1:]:
        w = jnp.exp(m - m_all)
        num = num + w * o
        den = den + w * l
    return num / den


def _attn_prompt_kernel(*refs):
    ng = len(ATT_GROUPS)
    blk = ATT_BLOCK
    hd = ATT_HEAD_DIM
    bias_ref = refs[5 * ng]
    y_ref = refs[5 * ng + 1]
    scr = refs[5 * ng + 2:]
    first_tile = pl.program_id(1) == 0
    ones = jnp.ones((2 * blk, hd), BF16)

    def rows_of(start, n, dil):
        return pl.ds(start, n) if dil == 1 else pl.ds(start, n, stride=dil)

    def unit(gi, dil, start, k2, v2, bias2):
        q_ref = refs[5 * gi]
        o_s, m_s, l_s = scr[3 * gi:3 * gi + 3]
        rows = rows_of(start, blk, dil)
        s = _bdot_nt(q_ref[rows, :] * ATT_SCALE, k2) + bias2
        m = jnp.max(s, axis=-1, keepdims=True)
        p = jnp.exp(s - m)
        ol = _bdot(p, jnp.concatenate([v2.astype(BF16), ones], axis=1))
        o_s[rows, :] = ol[:, :hd]
        m_s[rows, :] = jnp.broadcast_to(m, (blk, hd))
        l_s[rows, :] = ol[:, hd:]

    for gi, (window, dil) in enumerate(ATT_GROUPS):
        _, kc_ref, vc_ref, kp_ref, vp_ref = refs[5 * gi:5 * gi + 5]
        span = blk * dil
        n_sub = ATT_TILE // span
        bias = bias_ref[gi]
        bias_head = jnp.concatenate([jnp.where(first_tile, NEG_INF, bias[:, :blk]), bias[:, blk:]], axis=1)

        def head_unit(r, gi=gi, dil=dil, kc_ref=kc_ref, vc_ref=vc_ref, kp_ref=kp_ref, vp_ref=vp_ref,
                      bias_head=bias_head):
            prow, crow = rows_of(r, blk, dil), rows_of(r, blk, dil)
            k2 = jnp.concatenate([kp_ref[prow, :], kc_ref[crow, :]], axis=0)
            v2 = jnp.concatenate([vp_ref[prow, :], vc_ref[crow, :]], axis=0)
            unit(gi, dil, r, k2, v2, bias_head)

        def inner_unit(j, r, gi=gi, dil=dil, span=span, kc_ref=kc_ref, vc_ref=vc_ref, bias=bias):
            both = rows_of((j - 1) * span + r, 2 * blk, dil)
            unit(gi, dil, j * span + r, kc_ref[both, :], vc_ref[both, :], bias)

        if dil >= ATT_UNITS:
            def head_body(it, carry, head_unit=head_unit):
                for k in range(ATT_UNITS):
                    head_unit(it * ATT_UNITS + k)
                return carry
            lax.fori_loop(0, dil // ATT_UNITS, head_body, 0)
        else:
            for r in range(dil):
                head_unit(r)
        if n_sub > 1:
            per_it = max(ATT_UNITS // dil, 1)
            n_it = (n_sub - 1) // per_it

            def inner_body(it, carry, inner_unit=inner_unit, per_it=per_it, dil=dil):
                for k in range(per_it):
                    for r in range(dil):
                        inner_unit(1 + it * per_it + k, r)
                return carry
            lax.fori_loop(0, n_it, inner_body, 0)
            for j in range(1 + n_it * per_it, n_sub):
                for r in range(dil):
                    inner_unit(j, r)

    for c in range(ATT_TILE // MERGE_ROWS):
        rs = slice(c * MERGE_ROWS, (c + 1) * MERGE_ROWS)
        parts = [(scr[3 * gi][rs, :], scr[3 * gi + 1][rs, :], scr[3 * gi + 2][rs, :]) for gi in range(ng)]
        y_ref[rs, :] = _merge_three(parts).astype(y_ref.dtype)


def _attn_prompt(qkv, bias_p, bsz, s):
    hdim = ATT_HEAD_DIM
    ng = len(ATT_GROUPS)
    n_heads = qkv.shape[1] // (3 * hdim)
    tile = ATT_TILE
    assert s % tile == 0
    nt = s // tile
    in_specs, args = [], []
    for gi, (window, dil) in enumerate(ATT_GROUPS):
        span = ATT_BLOCK * dil
        per = tile // span

        def cur(part, gi=gi):
            return pl.BlockSpec((tile, hdim), lambda b, i, h: (b * nt + i, part * n_heads + gi * ATT_HPG + h))

        def prev(part, gi=gi, per=per):
            return pl.BlockSpec((span, hdim),
                                lambda b, i, h: (jnp.maximum((b * nt + i) * per - 1, 0),
                                                 part * n_heads + gi * ATT_HPG + h))

        in_specs += [cur(0), cur(1), cur(2), prev(1), prev(2)]
        args += [qkv] * 5
    in_specs.append(pl.BlockSpec((ng, None, ATT_BLOCK, 2 * ATT_BLOCK), lambda b, i, h: (0, h, 0, 0)))
    args.append(bias_p)
    return pl.pallas_call(
        _attn_prompt_kernel,
        grid=(bsz, nt, ATT_HPG),
        in_specs=in_specs,
        out_specs=pl.BlockSpec((tile, hdim), lambda b, i, h: (b * nt + i, h)),
        out_shape=jax.ShapeDtypeStruct((bsz * s, ATT_HPG * hdim), BF16),
        scratch_shapes=[pltpu.VMEM((tile, hdim), F32)] * (3 * ng),
        compiler_params=_cparams("parallel", "parallel", "arbitrary"),
        name="attn_prompt",
    )(*args)


REDUCE_WAYS = 4
DEC_KEYS = 32


def _reduce_rows(x, op):
    n = x.shape[0]
    part = op(x.reshape((REDUCE_WAYS, n // REDUCE_WAYS) + x.shape[1:]), axis=1)
    return op(part, axis=0)


def _attn_sample_kernel(qkv_ref, c0_ref, c1_ref, c2_ref, bias_ref, bias0_ref, y_ref):
    ng = len(ATT_GROUPS)
    hp = ATT_HPG
    caches = (c0_ref, c1_ref, c2_ref)
    zeros = jnp.zeros((hp, ATT_HEAD_DIM), F32)
    ones = jnp.ones((hp, ATT_HEAD_DIM), F32)

    def body(b, carry):
        parts = []
        for gi in range(ng):
            q = qkv_ref[b, gi]
            k_new = qkv_ref[b, ng + gi]
            v_new = qkv_ref[b, 2 * ng + gi]
            q2 = jnp.concatenate([q * ATT_SCALE, zeros], axis=0)
            s_self = jnp.sum(k_new * q, axis=-1, keepdims=True) * ATT_SCALE + bias0_ref[gi]
            m = jnp.concatenate([s_self, zeros], axis=0)
            l = jnp.concatenate([ones, ones], axis=0)
            o = jnp.concatenate([zeros, v_new], axis=0)
            for c in range(caches[gi].shape[1] // DEC_KEYS):
                ks = slice(c * DEC_KEYS, (c + 1) * DEC_KEYS)
                kv = caches[gi][b, ks]
                s = jnp.sum(kv * q2[None], axis=-1, keepdims=True) + bias_ref[gi, ks]
                m_new = jnp.maximum(m, _reduce_rows(s, jnp.max))
                alpha = jnp.exp(m - m_new)
                p = jnp.exp(s - m_new[None])
                l = alpha * l + _reduce_rows(p, jnp.sum)
                p_on_v = pltpu.roll(p, hp, 1)
                o = pltpu.roll(alpha, hp, 0) * o + _reduce_rows(p_on_v * kv, jnp.sum)
                m = m_new
            parts.append((o[hp:], m[:hp], l[:hp]))
        y_ref[b] = _merge_three(parts)
        return carry

    lax.fori_loop(0, DEC_TILE, body, 0)


def _attn_sample(qkv2, caches, layer, bias_d, bias0):
    db = qkv2.shape[0]
    bt = DEC_TILE
    hdim = ATT_HEAD_DIM
    ng = len(ATT_GROUPS)
    views, specs = [], []
    for (window, dil), cache in zip(ATT_GROUPS, caches):
        nk = window // dil
        depth, _, wb = cache.shape[:3]
        assert wb == window, "cached window must equal the group's window"
        views.append(cache.reshape(depth, db, nk, dil, 2 * ATT_HPG, hdim))
        specs.append(pl.BlockSpec((None, bt, nk, None, 2 * ATT_HPG, hdim),
                                  lambda i: (layer, i, 0, 0, 0, 0)))
    y = pl.pallas_call(
        _attn_sample_kernel,
        grid=(db // bt,),
        in_specs=[pl.BlockSpec((bt, 3 * ng, ATT_HPG, hdim), lambda i: (i, 0, 0, 0))] + specs +
                 [pl.BlockSpec(bias_d.shape, lambda i: (0, 0, 0, 0)),
                  pl.BlockSpec(bias0.shape, lambda i: (0, 0, 0))],
        out_specs=pl.BlockSpec((bt, ATT_HPG, hdim), lambda i: (i, 0, 0)),
        out_shape=jax.ShapeDtypeStruct((db, ATT_HPG, hdim), F32),
        compiler_params=_cparams("parallel"),
        name="attn_sample",
    )(qkv2.reshape(db, 3 * ng, ATT_HPG, hdim), *views, bias_d, bias0)
    return y.reshape(db, ATT_HPG * hdim)


def _branch_kernel(ysc_ref, yssm_ref, yatt_ref, ysc_s, yssm_s, yatt_s, wsc_ref, wssm_ref, watt_ref,
                   g0_ref, g1_ref, g2_ref, g0_s, g1_s, g2_s, o_ref, os_ref):
    def merged(ysc, yssm, yatt, g0, g1, g2, out):
        tn = out.shape[1]
        cw = min(tn, MXU_COLS)
        for ci in range(tn // cw):
            cs = slice(ci * cw, (ci + 1) * cw)
            a = jnp.dot(ysc[...], wsc_ref[:, cs], preferred_element_type=F32)
            b = jnp.dot(yssm[...], wssm_ref[:, cs], preferred_element_type=F32)
            c = jnp.dot(yatt[...], watt_ref[:, cs], preferred_element_type=F32)
            out[:, cs] = (_sigmoid(g0[:, cs]) * a + _sigmoid(g1[:, cs]) * b
                          + _sigmoid(g2[:, cs]) * c).astype(out.dtype)

    merged(ysc_ref, yssm_ref, yatt_ref, g0_ref, g1_ref, g2_ref, o_ref)

    @pl.when(pl.program_id(1) == 0)
    def _():
        merged(ysc_s, yssm_s, yatt_s, g0_s, g1_s, g2_s, os_ref)


def _branch_merge(ys, ys_s, wsc, wssm, watt, layer, p2, p2_s, tm, tn):
    m, d = ys[0].shape
    ms = ys_s[0].shape[0]
    da = ys[2].shape[1]
    nj = d // tn
    rows = lambda width: pl.BlockSpec((tm, width), lambda j, i: (i, 0))
    rows_s = lambda width: pl.BlockSpec((ms, width), lambda j, i: (0, 0))
    wcol = lambda k: pl.BlockSpec((None, k, tn), lambda j, i: (layer, 0, j))
    gate = lambda br: pl.BlockSpec((tm, tn), lambda j, i: (i, br * nj + j))
    gate_s = lambda br: pl.BlockSpec((ms, tn), lambda j, i: (0, br * nj + j))
    return pl.pallas_call(
        _branch_kernel,
        grid=(nj, m // tm),
        in_specs=[rows(d), rows(d), rows(da), rows_s(d), rows_s(d), rows_s(da), wcol(d), wcol(d), wcol(da),
                  gate(0), gate(1), gate(2), gate_s(0), gate_s(1), gate_s(2)],
        out_specs=[pl.BlockSpec((tm, tn), lambda j, i: (i, j)), pl.BlockSpec((ms, tn), lambda j, i: (0, j))],
        out_shape=[jax.ShapeDtypeStruct((m, d), BF16), jax.ShapeDtypeStruct((ms, d), BF16)],
        compiler_params=_cparams("parallel", "arbitrary"),
        name="branch_merge",
    )(*ys, *ys_s, wsc, wssm, watt, p2, p2, p2, p2_s, p2_s, p2_s)


def _outproj_kernel(x_ref, mg_ref, x_s, mg_s, w_ref, g_ref, x1_ref, h2_ref, x1_s, h2_s):
    def project(x, mg, x1_out, h2_out):
        m = x.shape[0]
        rc = min(m, NORM_ROWS)
        for r in range(m // rc):
            rows = slice(r * rc, (r + 1) * rc)
            x1 = x[rows, :] + jnp.dot(mg[rows, :], w_ref[...], preferred_element_type=F32)
            x1_out[rows, :] = x1
            ms = jnp.mean(x1 * x1, axis=-1, keepdims=True)
            h2_out[rows, :] = (x1 * lax.rsqrt(ms + EPS) * g_ref[...]).astype(h2_out.dtype)

    project(x_ref, mg_ref, x1_ref, h2_ref)

    @pl.when(pl.program_id(0) == 0)
    def _():
        project(x_s, mg_s, x1_s, h2_s)


def _outproj(x2d, merged, x2d_s, merged_s, w_out, layer, g2, tm):
    m, d = x2d.shape
    ms = x2d_s.shape[0]
    rows = pl.BlockSpec((tm, d), lambda i: (i, 0))
    rows_s = pl.BlockSpec((ms, d), lambda i: (0, 0))
    return pl.pallas_call(
        _outproj_kernel,
        grid=(m // tm,),
        in_specs=[rows, rows, rows_s, rows_s, pl.BlockSpec((None, d, d), lambda i: (layer, 0, 0)),
                  pl.BlockSpec((1, d), lambda i: (0, 0))],
        out_specs=[rows, rows, rows_s, rows_s],
        out_shape=[jax.ShapeDtypeStruct((m, d), F32), jax.ShapeDtypeStruct((m, d), BF16),
                   jax.ShapeDtypeStruct((ms, d), F32), jax.ShapeDtypeStruct((ms, d), BF16)],
        compiler_params=_cparams("arbitrary"),
        name="outproj_norm",
    )(x2d, merged, x2d_s, merged_s, w_out, g2.reshape(1, d))


def _mlp_kernel(x1_ref, h2_ref, x1_s, h2_s, wu_ref, wd_ref, o_ref, os_ref):
    def accumulate(x1, h2, out):
        @pl.when(pl.program_id(1) == 0)
        def _():
            out[...] = x1[...]

        tf = wu_ref.shape[1]
        cf = min(tf, FF_CHUNK)
        us = []
        for c in range(tf // cf):
            u = jnp.dot(h2[...], wu_ref[:, c * cf:(c + 1) * cf], preferred_element_type=F32)
            us.append(jnp.square(jnp.maximum(u, 0.0)).astype(BF16))
        out[...] += jnp.dot(jnp.concatenate(us, axis=1), wd_ref[...], preferred_element_type=F32)

    accumulate(x1_ref, h2_ref, o_ref)

    @pl.when(pl.program_id(0) == 0)
    def _():
        accumulate(x1_s, h2_s, os_ref)


def _mlp(x1, h2, x1_s, h2_s, w_up, w_down, layer, tm, tf):
    m, d = x1.shape
    ms = x1_s.shape[0]
    f = w_up.shape[2]
    rows = pl.BlockSpec((tm, d), lambda i, j: (i, 0))
    rows_s = pl.BlockSpec((ms, d), lambda i, j: (0, 0))
    return pl.pallas_call(
        _mlp_kernel,
        grid=(m // tm, f // tf),
        in_specs=[rows, rows, rows_s, rows_s, pl.BlockSpec((None, d, tf), lambda i, j: (layer, 0, j)),
                  pl.BlockSpec((None, tf, d), lambda i, j: (layer, j, 0))],
        out_specs=[rows, rows_s],
        out_shape=[jax.ShapeDtypeStruct((m, d), F32), jax.ShapeDtypeStruct((ms, d), F32)],
        compiler_params=_cparams("arbitrary", "arbitrary"),
        name="mlp",
    )(x1, h2, x1_s, h2_s, w_up, w_down)


def _row_tile(m, pref):
    return pref if m % pref == 0 else m


def _pad_lanes(v):
    return jnp.pad(v.astype(F32), (0, LANES - v.shape[0])).reshape(1, LANES)


def _prep_layer(l, shared, norm1_g, sc_conv_w, ssm_conv_w, ssm_conv_b, ssm_dt_bias, ssm_A_log, ssm_D,
                ssm_norm_g, q_norm_g, k_norm_g, norm2_g):
    wt_in = shared["wt_in"]
    d = wt_in.shape[2]
    heads = d // SSM_HEAD_DIM
    aw = ATT_HPG * len(ATT_GROUPS) * ATT_HEAD_DIM
    n_main = N_BRANCH * d + 3 * d + d + (d + 2 * SSM_GROUPS * SSM_STATE)
    assert n_main % LANES == 0 and heads <= LANES
    n_att_heads = aw // ATT_HEAD_DIM
    gains = jnp.concatenate([jnp.tile(q_norm_g[l], n_att_heads), jnp.tile(k_norm_g[l], n_att_heads),
                             jnp.ones((aw,), F32)]).reshape(1, 3 * aw)
    return dict(
        shared, layer=l, n_main=n_main, qkv_row=n_main + heads, n_qkv=3 * aw,
        norm1_g=norm1_g[l], norm2_g=norm2_g[l], qk_gains=gains,
        sc_conv_w=sc_conv_w[l], ssm_conv_w=ssm_conv_w[l], ssm_conv_b=ssm_conv_b[l].reshape(1, -1),
        dt_bias=_pad_lanes(ssm_dt_bias[l]), a_log=_pad_lanes(ssm_A_log[l]),
        d_exp=jnp.repeat(ssm_D[l].astype(F32), SSM_HEAD_DIM).reshape(1, d),
        ssm_norm_g=ssm_norm_g[l].reshape(1, d),
    )


def _in_proj(x2d, x2d_s, w):
    m = x2d.shape[0]
    h = _rmsnorm(x2d, w["norm1_g"], _row_tile(m, 512))
    h_s = _rmsnorm(x2d_s, w["norm1_g"], x2d_s.shape[0])
    p = _matmul_wcast(h, h_s, w["wt_in"], w["layer"], 0, w["n_main"], MXU_COLS, name="in_proj_main")
    dtr = _matmul_wcast(h, h_s, w["wt_in"], w["layer"], w["n_main"] // LANES, LANES, LANES, name="in_proj_dt")
    qkv = _matmul_qkv(h, h_s, w["wt_in"], w["layer"], w["qkv_row"], w["n_qkv"], w["qk_gains"], MXU_COLS)
    return (p[0], dtr[0], qkv[0]), (p[1], dtr[1], qkv[1])


def _tail(x2d, p, ys, x2d_s, p_s, ys_s, w):
    m = x2d.shape[0]
    l = w["layer"]
    merged, merged_s = _branch_merge(ys, ys_s, w["w_br_sc"], w["w_br_ssm"], w["w_br_att"], l, p, p_s,
                                     _row_tile(m, 1024), 512)
    x1, h2, x1_s, h2_s = _outproj(x2d, merged, x2d_s, merged_s, w["w_out"], l, w["norm2_g"], _row_tile(m, 512))
    return _mlp(x1, h2, x1_s, h2_s, w["w_up"], w["w_down"], l, _row_tile(m, 1024), 512)


def _kv_rows(qkv3, gi, keep):
    aw = qkv3.shape[2] // 3
    gw = ATT_HPG * ATT_HEAD_DIM
    bsz, ln, _ = qkv3.shape
    k = qkv3[:, ln - keep:, aw + gi * gw:aw + (gi + 1) * gw].reshape(bsz, keep, ATT_HPG, ATT_HEAD_DIM)
    v = qkv3[:, ln - keep:, 2 * aw + gi * gw:2 * aw + (gi + 1) * gw].reshape(bsz, keep, ATT_HPG, ATT_HEAD_DIM)
    return jnp.stack([k, v], axis=2)


def _prompt_mixers(proj, w, bias_p, bsz, s, d):
    p, dtr, qkv = proj
    p3 = p.reshape(bsz, s, -1)
    qkv3 = qkv.reshape(bsz, s, -1)
    ysc, sc_new = _sc_prompt(p3, w["sc_conv_w"], d, 1024, 1024)
    yssm, ht = _ssd_prompt(p3, dtr.reshape(bsz, s, LANES), w["ssm_conv_w"], w["ssm_conv_b"], w["dt_bias"],
                           w["a_log"], w["d_exp"], w["ssm_norm_g"], d)
    xbc_off = N_BRANCH * d + 3 * d + d
    ssm_conv_new = p3[:, s - (SSM_CONV - 1):, xbc_off:]
    kv_new = [_kv_rows(qkv3, gi, min(window, s)) for gi, (window, _) in enumerate(ATT_GROUPS)]
    yatt = _attn_prompt(qkv, bias_p, bsz, s)
    heads = d // SSM_HEAD_DIM
    st = (sc_new, ssm_conv_new, ht.reshape(bsz, heads, SSM_HEAD_DIM, SSM_STATE), kv_new[0], kv_new[1], kv_new[2])
    return (ysc.reshape(bsz * s, d), yssm.reshape(bsz * s, d), yatt), st


def _sample_mixers(proj, w, bias_d, bias0, sc_buf, ssm_buf, ssm_h0_all, ssm_ht_all, kv_caches, d):
    p, dtr, qkv = proj
    db = p.shape[0]
    l = w["layer"]
    ysc, u_new = _sc_sample(p, sc_buf.reshape(db, -1), w["sc_conv_w"], d, 512)
    sc_new = jnp.concatenate([sc_buf[:, 1:], u_new[:, None, :]], axis=1)
    heads = d // SSM_HEAD_DIM
    depth = ssm_h0_all.shape[0]
    yssm, ssm_ht_all = _ssd_sample(p, dtr, ssm_buf.reshape(db, -1),
                                   ssm_h0_all.reshape(depth, db, heads * SSM_HEAD_DIM, SSM_STATE), l, ssm_ht_all,
                                   w["ssm_conv_w"], w["ssm_conv_b"], w["dt_bias"], w["a_log"], w["d_exp"],
                                   w["ssm_norm_g"], d)
    xbc_off = N_BRANCH * d + 3 * d + d
    ssm_conv_new = jnp.concatenate([ssm_buf[:, 1:], p[:, None, xbc_off:]], axis=1)
    yatt = _attn_sample(qkv, kv_caches, l, bias_d, bias0)
    qkv3 = qkv.reshape(db, 1, -1)
    kv_new = [_kv_rows(qkv3, gi, 1) for gi in range(len(ATT_GROUPS))]
    st = (sc_new, ssm_conv_new, kv_new[0], kv_new[1], kv_new[2])
    return (ysc.astype(BF16), yssm.astype(BF16), yatt.astype(BF16)), st, ssm_ht_all


def kernel(x_prompt, x_sample, state_sc_conv, state_ssm_conv, state_ssm, cache_kv_w128, cache_kv_w512,
           cache_kv_w2048, norm1_g, w_in, sc_conv_w, ssm_conv_w, ssm_conv_b, ssm_dt_bias, ssm_A_log, ssm_D,
           ssm_norm_g, q_norm_g, k_norm_g, rel_bias, w_br_sc, w_br_ssm, w_br_att, w_out, norm2_g, w_up, w_down):
    depth = w_in.shape[0]
    ng = len(ATT_GROUPS)
    bias_p, bias_d = _build_bias(rel_bias)
    bias_p = bias_p.reshape((ng, ATT_HPG) + bias_p.shape[1:])
    bias_d = bias_d.reshape((ng, ATT_HPG) + bias_d.shape[1:]).transpose(0, 2, 1, 3)
    bias_d = jnp.concatenate([bias_d, jnp.zeros_like(bias_d)], axis=2)
    bias0 = jnp.broadcast_to(rel_bias[0].reshape(ng, ATT_HPG, 1), (ng, ATT_HPG, LANES))
    bsz, s, d = x_prompt.shape
    db, t, _ = x_sample.shape
    assert t == 1, "decode kernels handle one new token per sample"
    xp, xs = x_prompt.reshape(bsz * s, d), x_sample.reshape(db, d)
    p_new, s_new, s_ssm = [], [], None
    shared = dict(wt_in=jnp.swapaxes(w_in, 1, 2),
                  w_br_sc=w_br_sc.astype(BF16), w_br_ssm=w_br_ssm.astype(BF16), w_br_att=w_br_att.astype(BF16),
                  w_out=w_out.astype(BF16), w_up=w_up.astype(BF16), w_down=w_down.astype(BF16))
    for l in range(depth):
        w = _prep_layer(l, shared, norm1_g, sc_conv_w, ssm_conv_w, ssm_conv_b, ssm_dt_bias, ssm_A_log, ssm_D,
                        ssm_norm_g, q_norm_g, k_norm_g, norm2_g)
        proj, proj_s = _in_proj(xp, xs, w)
        ys, st = _prompt_mixers(proj, w, bias_p, bsz, s, d)
        p_new.append(st)
        ys_s, st, s_ssm = _sample_mixers(proj_s, w, bias_d, bias0, state_sc_conv[l], state_ssm_conv[l], state_ssm,
                                         s_ssm, (cache_kv_w128, cache_kv_w512, cache_kv_w2048), d)
        s_new.append(st)
        xp, xs = _tail(xp, proj[0], ys, xs, proj_s[0], ys_s, w)
    p_out = [jnp.stack(a) for a in zip(*p_new)]
    s_out = [jnp.stack(a) for a in zip(*s_new)]
    s_out.insert(2, s_ssm.reshape(state_ssm.shape))
    return (xp.reshape(bsz, s, d), xs.reshape(db, 1, d), *p_out, *s_out)
```

```python
import functools
import math

import numpy as np
import jax
import jax.numpy as jnp
from jax import lax
from jax.experimental import pallas as pl
from jax.experimental.pallas import tpu as pltpu

F32 = jnp.float32
BF16 = jnp.bfloat16

N_BRANCH = 3
SC_CONV = 3
SSM_HEAD_DIM = 64
SSM_GROUPS = 8
SSM_STATE = 128
SSM_CONV = 4
SSM_CHUNK = 128
ATT_HEAD_DIM = 128
ATT_GROUPS = ((128, 1), (512, 4), (2048, 16))
ATT_HPG = 4
ATT_BLOCK = 128
ATT_SCALE = 1.0 / math.sqrt(ATT_HEAD_DIM)
N_BUCKETS = 32
MAX_DISTANCE = 2048
EPS = 1e-6

LANES = 128
SUBLANES = 8
MXU_COLS = 256
VMEM_LIMIT_BYTES = 60 * 1024 * 1024

DEC_TILE = 8
NEG_INF = float("-inf")


def _cparams(*sem):
    return pltpu.CompilerParams(dimension_semantics=sem, vmem_limit_bytes=VMEM_LIMIT_BYTES)


def _sigmoid(v):
    return 1.0 / (1.0 + jnp.exp(-v))


def _silu(v):
    return v * _sigmoid(v)


def _bdot(a, b):
    return jnp.dot(a.astype(BF16), b.astype(BF16), preferred_element_type=F32)


def _bdot_nt(a, b):
    return lax.dot_general(a.astype(BF16), b.astype(BF16), (((1,), (1,)), ((), ())),
                           preferred_element_type=F32)


def _rmsnorm_kernel(x_ref, g_ref, o_ref):
    x = x_ref[...]
    ms = jnp.mean(x * x, axis=-1, keepdims=True)
    o_ref[...] = (x * lax.rsqrt(ms + EPS) * g_ref[...]).astype(o_ref.dtype)


def _rmsnorm(x2d, g, tm):
    m, d = x2d.shape
    return pl.pallas_call(
        _rmsnorm_kernel,
        grid=(m // tm,),
        in_specs=[pl.BlockSpec((tm, d), lambda i: (i, 0)),
                  pl.BlockSpec((1, d), lambda i: (0, 0))],
        out_specs=pl.BlockSpec((tm, d), lambda i: (i, 0)),
        out_shape=jax.ShapeDtypeStruct((m, d), BF16),
        compiler_params=_cparams("parallel"),
        name="rmsnorm",
    )(x2d, g.reshape(1, d))


ROW_CHUNK = 1024
NORM_ROWS = 256
FF_CHUNK = 512


def _resident_rows_dot(a_ref, as_ref, w_ref, wb_ref, emit):
    wb_ref[...] = w_ref[...].astype(BF16)
    m = a_ref.shape[0]
    rc = min(m, ROW_CHUNK)
    for r in range(m // rc):
        rows = slice(r * rc, (r + 1) * rc)
        emit(0, rows, _bdot_nt(a_ref[rows, :], wb_ref[...]))
    emit(1, slice(0, as_ref.shape[0]), _bdot_nt(as_ref[...], wb_ref[...]))


def _mm_wcast_kernel(a_ref, as_ref, w_ref, o_ref, os_ref, wb_ref):
    outs = (o_ref, os_ref)

    def emit(which, rows, acc):
        outs[which][rows, :] = acc

    _resident_rows_dot(a_ref, as_ref, w_ref, wb_ref, emit)


def _resident(shape):
    return pl.BlockSpec(shape, lambda j: (0,) * len(shape), pipeline_mode=pl.Buffered(1))


def _matmul_wcast(a, a_s, wt_all, layer, first_block, n_cols, tn, name):
    m, k = a.shape
    ms = a_s.shape[0]
    assert m % min(m, ROW_CHUNK) == 0
    return pl.pallas_call(
        _mm_wcast_kernel,
        grid=(n_cols // tn,),
        in_specs=[_resident((m, k)), _resident((ms, k)),
                  pl.BlockSpec((None, tn, k), lambda j: (layer, first_block + j, 0))],
        out_specs=[pl.BlockSpec((m, tn), lambda j: (0, j)), pl.BlockSpec((ms, tn), lambda j: (0, j))],
        out_shape=[jax.ShapeDtypeStruct((m, n_cols), F32), jax.ShapeDtypeStruct((ms, n_cols), F32)],
        scratch_shapes=[pltpu.VMEM((tn, k), BF16)],
        compiler_params=_cparams("arbitrary"),
        name=name,
    )(a, a_s, wt_all)


def _mm_qkv_kernel(a_ref, as_ref, w_ref, g_ref, o_ref, os_ref, wb_ref, *, n_norm_blocks):
    is_qk = pl.program_id(0) < n_norm_blocks
    outs = (o_ref, os_ref)

    def emit(which, rows, acc):
        for h in range(acc.shape[1] // ATT_HEAD_DIM):
            sl = slice(h * ATT_HEAD_DIM, (h + 1) * ATT_HEAD_DIM)
            t = acc[:, sl]
            ms = jnp.mean(t * t, axis=-1, keepdims=True)
            outs[which][rows, sl] = t * jnp.where(is_qk, lax.rsqrt(ms + EPS), 1.0) * g_ref[:, sl]

    _resident_rows_dot(a_ref, as_ref, w_ref.at[0], wb_ref, emit)


def _matmul_qkv(a, a_s, wt_all, layer, first_row, n, gains, tn):
    m, k = a.shape
    ms = a_s.shape[0]
    assert first_row % SUBLANES == 0 and tn % ATT_HEAD_DIM == 0
    att_width = n // 3
    return pl.pallas_call(
        functools.partial(_mm_qkv_kernel, n_norm_blocks=2 * att_width // tn),
        grid=(n // tn,),
        in_specs=[_resident((m, k)), _resident((ms, k)),
                  pl.BlockSpec((pl.Element(1), pl.Element(tn), pl.Element(k)),
                               lambda j: (layer, pl.multiple_of(first_row + j * tn, SUBLANES), 0)),
                  pl.BlockSpec((1, tn), lambda j: (0, j))],
        out_specs=[pl.BlockSpec((m, tn), lambda j: (0, j)), pl.BlockSpec((ms, tn), lambda j: (0, j))],
        out_shape=[jax.ShapeDtypeStruct((m, n), F32), jax.ShapeDtypeStruct((ms, n), F32)],
        scratch_shapes=[pltpu.VMEM((tn, k), BF16)],
        compiler_params=_cparams("arbitrary"),
        name="matmul_qkv",
    )(a, a_s, wt_all, gains)


def _sc_prompt_kernel(b_ref, c_ref, x_ref, hc_ref, hx_ref, w_ref, y_ref, tail_ref, ext_ref, *, ts):
    t = pl.program_id(2)
    u = c_ref[...] * x_ref[...]
    hu = hc_ref[...] * hx_ref[...]
    ext_ref[0:SUBLANES, :] = jnp.where(t > 0, hu, 0.0)
    ext_ref[SUBLANES:SUBLANES + ts, :] = u
    u2 = ext_ref[pl.ds(SUBLANES - 2, ts), :]
    u1 = ext_ref[pl.ds(SUBLANES - 1, ts), :]
    y = u2 * w_ref[0:1, :] + u1 * w_ref[1:2, :] + u * w_ref[2:3, :]
    y_ref[...] = (b_ref[...] * y).astype(y_ref.dtype)
    tail_ref[...] = ext_ref[pl.ds(ts, SUBLANES), :]


def _sc_prompt(p3, conv_w, d, ts, tc):
    bsz, s, _ = p3.shape
    ncb = d // tc
    off_b, off_c, off_x = 3 * ncb, 4 * ncb, 5 * ncb
    hb = ts // SUBLANES

    def halo(off):
        return pl.BlockSpec((None, SUBLANES, tc),
                            lambda b, c, t: (b, jnp.maximum(t * hb - 1, 0), off + c))

    def main(off):
        return pl.BlockSpec((None, ts, tc), lambda b, c, t: (b, t, off + c))

    y, tail = pl.pallas_call(
        functools.partial(_sc_prompt_kernel, ts=ts),
        grid=(bsz, ncb, s // ts),
        in_specs=[main(off_b), main(off_c), main(off_x), halo(off_c), halo(off_x),
                  pl.BlockSpec((SC_CONV, tc), lambda b, c, t: (0, c))],
        out_specs=[pl.BlockSpec((None, ts, tc), lambda b, c, t: (b, t, c)),
                   pl.BlockSpec((None, SUBLANES, tc), lambda b, c, t: (b, 0, c))],
        out_shape=[jax.ShapeDtypeStruct((bsz, s, d), BF16),
                   jax.ShapeDtypeStruct((bsz, SUBLANES, d), F32)],
        scratch_shapes=[pltpu.VMEM((ts + SUBLANES, tc), F32)],
        compiler_params=_cparams("parallel", "parallel", "arbitrary"),
        name="short_conv_prompt",
    )(p3, p3, p3, p3, p3, conv_w)
    return y, tail[:, SUBLANES - (SC_CONV - 1):, :]


def _sc_sample_kernel(b_ref, c_ref, x_ref, s0_ref, s1_ref, w_ref, y_ref, u_ref):
    u = c_ref[...] * x_ref[...]
    y = s0_ref[...] * w_ref[0:1, :] + s1_ref[...] * w_ref[1:2, :] + u * w_ref[2:3, :]
    y_ref[...] = b_ref[...] * y
    u_ref[...] = u


def _sc_sample(p2, state2d, conv_w, d, tc):
    db = p2.shape[0]
    ncb = d // tc

    def col(off):
        return pl.BlockSpec((db, tc), lambda c: (0, off + c))

    return pl.pallas_call(
        _sc_sample_kernel,
        grid=(ncb,),
        in_specs=[col(3 * ncb), col(4 * ncb), col(5 * ncb), col(0), col(ncb),
                  pl.BlockSpec((SC_CONV, tc), lambda c: (0, c))],
        out_specs=[col(0), col(0)],
        out_shape=[jax.ShapeDtypeStruct((db, d), F32), jax.ShapeDtypeStruct((db, d), F32)],
        compiler_params=_cparams("parallel"),
        name="short_conv_sample",
    )(p2, p2, p2, state2d, state2d, conv_w)


def _softplus(v):
    return jnp.maximum(v, 0.0) + jnp.log1p(jnp.exp(-jnp.abs(v)))


def _gated_group_norm(y, xs, z, dexp, ng):
    val = (y + dexp * xs) * _silu(z)
    ms = jnp.mean(val * val, axis=-1, keepdims=True)
    return val * lax.rsqrt(ms + EPS) * ng


def _ssd_prompt_kernel(x_ref, bc_ref, z_ref, dtr_ref, cwx_ref, cwbc_ref, cbx_ref, cbbc_ref, dtb_ref,
                       alog_ref, dexp_ref, ng_ref, y_ref, ht_ref, extx, extbc, xs_s, bc_s, state):
    q = SSM_CHUNK
    c = pl.program_id(1)
    pad = SUBLANES
    taps = SSM_CONV

    @pl.when(c == 0)
    def _():
        extx[0:pad, :] = jnp.zeros((pad, extx.shape[1]), F32)
        extbc[0:pad, :] = jnp.zeros((pad, extbc.shape[1]), F32)
        state[...] = jnp.zeros(state.shape, F32)

    extx[pad:pad + q, :] = x_ref[...]
    extbc[pad:pad + q, :] = bc_ref[...]

    def conv(ext, w_ref, b_ref):
        full = ext[...]
        acc = pltpu.roll(full, taps - 1, 0)[pad:, :] * w_ref[0:1, :]
        for i in range(1, taps - 1):
            acc = acc + pltpu.roll(full, taps - 1 - i, 0)[pad:, :] * w_ref[i:i + 1, :]
        acc = acc + full[pad:, :] * w_ref[taps - 1:taps, :]
        return _silu(acc + b_ref[...])

    xs_s[...] = conv(extx, cwx_ref, cbx_ref)
    bc_s[...] = conv(extbc, cwbc_ref, cbbc_ref)
    extx[0:pad, :] = extx[q:q + pad, :]
    extbc[0:pad, :] = extbc[q:q + pad, :]

    dt = _softplus(dtr_ref[...] + dtb_ref[...])
    a_neg = -jnp.exp(alog_ref[...])
    ii = lax.broadcasted_iota(jnp.int32, (q, q), 0)
    jj = lax.broadcasted_iota(jnp.int32, (q, q), 1)
    causal = ii >= jj
    acum = jnp.dot(causal.astype(F32), dt * a_neg, precision=lax.Precision.HIGHEST,
                   preferred_element_type=F32)
    acum_t = acum.T
    dt_t = dt.T
    lo_half = jj < SSM_HEAD_DIM
    gs = SSM_STATE
    hd = SSM_HEAD_DIM
    hpg = xs_s.shape[1] // (SSM_GROUPS * hd)
    gw = hpg * hd
    n_bc = SSM_GROUPS * gs

    for g in range(SSM_GROUPS):
        bg = bc_s[:, g * gs:(g + 1) * gs].astype(BF16)
        cg = bc_s[:, n_bc + g * gs:n_bc + (g + 1) * gs].astype(BF16)
        cb = _bdot_nt(cg, bg)
        ys = []
        for hp in range(hpg // 2):
            e0 = g * hpg + 2 * hp
            rows = slice(e0 * hd, (e0 + 2) * hd)
            xp = xs_s[:, rows]
            xp16 = xp.astype(BF16)
            cols_a, cols_dt, parts = [], [], []
            for k in range(2):
                e = e0 + k
                col_a = jnp.broadcast_to(acum[:, e:e + 1], (q, q))
                cols_a.append(col_a)
                cols_dt.append(jnp.broadcast_to(dt[:, e:e + 1], (q, q)))
                dec = jnp.exp(jnp.where(causal, col_a - acum_t[e:e + 1, :], NEG_INF))
                wm = cb * dec * dt_t[e:e + 1, :]
                parts.append(jnp.dot(wm.astype(BF16), xp16, preferred_element_type=F32))
            y_diag = jnp.where(lo_half, parts[0], parts[1])
            sel_a = jnp.where(lo_half, cols_a[0], cols_a[1])
            sel_dt = jnp.where(lo_half, cols_dt[0], cols_dt[1])
            hpair = state[rows, :]
            y_off = _bdot_nt(cg, hpair) * jnp.exp(sel_a)
            ys.append(y_diag + y_off)
            last_a = sel_a[q - 1:q, :]
            xw = xp * jnp.exp(last_a - sel_a) * sel_dt
            st = jnp.dot(xw.T.astype(BF16), bg, preferred_element_type=F32)
            cd = jnp.concatenate([jnp.broadcast_to(cols_a[0][q - 1:q, :], (hd, gs)),
                                  jnp.broadcast_to(cols_a[1][q - 1:q, :], (hd, gs))], axis=0)
            state[rows, :] = hpair * jnp.exp(cd) + st
        gsl = slice(g * gw, (g + 1) * gw)
        yg = jnp.concatenate(ys, axis=1)
        y_ref[:, gsl] = _gated_group_norm(yg, xs_s[:, gsl], z_ref[:, gsl], dexp_ref[:, gsl],
                                          ng_ref[:, gsl]).astype(y_ref.dtype)

    @pl.when(c == pl.num_programs(1) - 1)
    def _():
        ht_ref[...] = state[...]


def _ssd_prompt(p3, dt_block, cw, cb, dtb, alog, dexp, ng, d):
    bsz, s, _ = p3.shape
    q = SSM_CHUNK
    heads = d // SSM_HEAD_DIM
    const = lambda shape: pl.BlockSpec(shape, lambda b, c: (0, 0))
    y, ht = pl.pallas_call(
        _ssd_prompt_kernel,
        grid=(bsz, s // q),
        in_specs=[pl.BlockSpec((None, q, d), lambda b, c: (b, c, 7)),
                  pl.BlockSpec((None, q, d), lambda b, c: (b, c, 8)),
                  pl.BlockSpec((None, q, d), lambda b, c: (b, c, 6)),
                  pl.BlockSpec((None, q, LANES), lambda b, c: (b, c, dt_block)),
                  const((SSM_CONV, d)), const((SSM_CONV, d)), const((1, d)), const((1, d)),
                  const((1, LANES)), const((1, LANES)), const((1, d)), const((1, d))],
        out_specs=[pl.BlockSpec((None, q, d), lambda b, c: (b, c, 0)),
                   pl.BlockSpec((None, heads * SSM_HEAD_DIM, SSM_STATE), lambda b, c: (b, 0, 0))],
        out_shape=[jax.ShapeDtypeStruct((bsz, s, d), BF16),
                   jax.ShapeDtypeStruct((bsz, heads * SSM_HEAD_DIM, SSM_STATE), F32)],
        scratch_shapes=[pltpu.VMEM((q + SUBLANES, d), F32), pltpu.VMEM((q + SUBLANES, d), F32),
                        pltpu.VMEM((q, d), F32), pltpu.VMEM((q, d), F32),
                        pltpu.VMEM((heads * SSM_HEAD_DIM, SSM_STATE), F32)],
        compiler_params=_cparams("parallel", "arbitrary"),
        name="ssd_prompt",
    )(p3, p3, p3, p3, cw[:, :d], cw[:, d:], cb[:, :d], cb[:, d:], dtb, alog, dexp, ng)
    return y, ht


def _ssd_sample_kernel(*refs, layer, owns_all_layers):
    (x_ref, bc_ref, z_ref, dtr_ref, s0x, s0bc, s1x, s1bc, s2x, s2bc, h0_ref, cwx_ref, cwbc_ref, cbx_ref,
     cbbc_ref, dtb_ref, alog_ref, dexp_ref, ng_ref) = refs[:19]
    y_ref, ht_all_ref, padx, padbc, padd, pada, y_s = refs[-7:]
    if owns_all_layers:
        ht_ref = ht_all_ref.at[layer]
        for other in range(ht_all_ref.shape[0]):
            if other != layer:
                ht_all_ref[other] = jnp.zeros(ht_all_ref.shape[1:], F32)
    else:
        ht_ref = ht_all_ref
    bt = DEC_TILE
    hd = SSM_HEAD_DIM
    gs = SSM_STATE
    d = x_ref.shape[1]
    n_chunk = d // LANES
    hpg = d // (SSM_GROUPS * hd)
    gw = hpg * hd
    n_bc = SSM_GROUPS * gs

    @pl.when(pl.program_id(0) == 0)
    def _():
        padx[...] = jnp.zeros(padx.shape, F32)
        padbc[...] = jnp.zeros(padbc.shape, F32)
        padd[...] = jnp.zeros(padd.shape, F32)
        pada[...] = jnp.zeros(pada.shape, F32)

    def conv(s0, s1, s2, u, w_ref, b_ref):
        acc = s0[...] * w_ref[0:1, :] + s1[...] * w_ref[1:2, :] + s2[...] * w_ref[2:3, :] + u[...] * w_ref[3:4, :]
        return _silu(acc + b_ref[...])

    padx[0:bt, :] = conv(s0x, s1x, s2x, x_ref, cwx_ref, cbx_ref)
    padbc[0:bt, :] = conv(s0bc, s1bc, s2bc, bc_ref, cwbc_ref, cbbc_ref)
    dt = _softplus(dtr_ref[...] + dtb_ref[...])
    padd[0:bt, :] = dt
    pada[0:bt, :] = jnp.exp(dt * (-jnp.exp(alog_ref[...])))
    dt_t = padd[...].T
    da_t = pada[...].T
    wide = bt * gs
    row_w = lax.broadcasted_iota(jnp.int32, (LANES, wide), 0)
    blk_w = lax.broadcasted_iota(jnp.int32, (LANES, wide), 1) // gs
    row_t = lax.broadcasted_iota(jnp.int32, (wide, LANES), 0) // gs
    lane_t = lax.broadcasted_iota(jnp.int32, (wide, LANES), 1)

    bbig = cbig_t = None
    for ci in range(n_chunk):
        g = (2 * ci) // hpg
        if (2 * ci) % hpg == 0:
            bg = padbc[:, g * gs:(g + 1) * gs]
            bbig = jnp.where(row_w == blk_w, jnp.concatenate([bg] * bt, axis=1), 0.0).astype(BF16)
            cg_t = padbc[:, n_bc + g * gs:n_bc + (g + 1) * gs].T
            cbig_t = jnp.where(row_t == lane_t, jnp.concatenate([cg_t] * bt, axis=0), 0.0).astype(BF16)
        csl = slice(ci * LANES, (ci + 1) * LANES)
        x_t = padx[:, csl].T
        dt_rows = jnp.concatenate([jnp.broadcast_to(dt_t[2 * ci:2 * ci + 1, :], (hd, LANES)),
                                   jnp.broadcast_to(dt_t[2 * ci + 1:2 * ci + 2, :], (hd, LANES))], axis=0)
        inc = jnp.dot((x_t * dt_rows).astype(BF16), bbig, preferred_element_type=F32)
        hs = []
        for k in range(bt):
            da_rows = jnp.concatenate(
                [jnp.broadcast_to(da_t[2 * ci:2 * ci + 1, k:k + 1], (hd, gs)),
                 jnp.broadcast_to(da_t[2 * ci + 1:2 * ci + 2, k:k + 1], (hd, gs))], axis=0)
            hn = h0_ref[k, csl, :] * da_rows + inc[:, k * gs:(k + 1) * gs]
            ht_ref[k, csl, :] = hn
            hs.append(hn.astype(BF16))
        y_t = jnp.dot(jnp.concatenate(hs, axis=1), cbig_t, preferred_element_type=F32)
        y_s[:, csl] = y_t.T[0:bt, :]

    for g in range(SSM_GROUPS):
        gsl = slice(g * gw, (g + 1) * gw)
        y_ref[:, gsl] = _gated_group_norm(y_s[:, gsl], padx[0:bt, gsl], z_ref[:, gsl], dexp_ref[:, gsl],
                                          ng_ref[:, gsl])


def _ssd_sample(p2, dt_block, conv_state2d, h0_all, layer, ht_all, cw, cb, dtb, alog, dexp, ng, d):
    db = p2.shape[0]
    bt = DEC_TILE
    depth, _, hp, _ = h0_all.shape
    first = ht_all is None
    row = lambda off: pl.BlockSpec((bt, d), lambda i: (i, off))
    const = lambda shape: pl.BlockSpec(shape, lambda i: (0, 0))
    in_specs = [row(7), row(8), row(6), pl.BlockSpec((bt, LANES), lambda i: (i, dt_block)),
                row(0), row(1), row(2), row(3), row(4), row(5),
                pl.BlockSpec((None, bt, hp, SSM_STATE), lambda i: (layer, i, 0, 0)),
                const((SSM_CONV, d)), const((SSM_CONV, d)), const((1, d)), const((1, d)),
                const((1, LANES)), const((1, LANES)), const((1, d)), const((1, d))]
    args = [p2, p2, p2, p2, conv_state2d, conv_state2d, conv_state2d, conv_state2d, conv_state2d,
            conv_state2d, h0_all, cw[:, :d], cw[:, d:], cb[:, :d], cb[:, d:], dtb, alog, dexp, ng]
    if first:
        state_out = pl.BlockSpec((depth, bt, hp, SSM_STATE), lambda i: (0, i, 0, 0))
        aliases = {}
    else:
        state_out = pl.BlockSpec((None, bt, hp, SSM_STATE), lambda i: (layer, i, 0, 0))
        in_specs.append(pl.BlockSpec(memory_space=pl.ANY))
        args.append(ht_all)
        aliases = {len(args) - 1: 1}
    y, ht_all = pl.pallas_call(
        functools.partial(_ssd_sample_kernel, layer=layer, owns_all_layers=first),
        grid=(db // bt,),
        in_specs=in_specs,
        out_specs=[pl.BlockSpec((bt, d), lambda i: (i, 0)), state_out],
        out_shape=[jax.ShapeDtypeStruct((db, d), F32),
                   jax.ShapeDtypeStruct((depth, db, hp, SSM_STATE), F32)],
        scratch_shapes=[pltpu.VMEM((LANES, d), F32), pltpu.VMEM((LANES, d), F32),
                        pltpu.VMEM((LANES, LANES), F32), pltpu.VMEM((LANES, LANES), F32),
                        pltpu.VMEM((bt, d), F32)],
        input_output_aliases=aliases,
        compiler_params=_cparams("arbitrary"),
        name="ssd_sample",
    )(*args)
    return y, ht_all


def _t5_bucket_np(dist):
    d = np.asarray(dist, np.int32)
    max_exact = N_BUCKETS // 2
    df = np.maximum(d, 1).astype(np.float32)
    large = max_exact + (np.log(df / np.float32(max_exact)) / np.float32(math.log(MAX_DISTANCE / max_exact))
                         * np.float32(N_BUCKETS - max_exact)).astype(np.int32)
    large = np.minimum(large, N_BUCKETS - 1)
    return np.where(d < max_exact, d, large).astype(np.int32)


def _bias_index_tables():
    blk = ATT_BLOCK
    qi = np.arange(blk)[:, None]
    kj = np.arange(2 * blk)[None, :]
    sdist = qi + blk - kj
    idx_p, idx_d = [], []
    for window, dil in ATT_GROUPS:
        nk = window // dil
        band = (sdist >= 0) & (sdist <= nk)
        idx_p.append(np.where(band, _t5_bucket_np(np.clip(sdist, 0, nk) * dil), -1))
        steps = nk - np.arange(nk)
        idx_d.append(np.broadcast_to(_t5_bucket_np(steps * dil)[:, None], (nk, LANES)))
    return np.stack(idx_p).astype(np.int32), np.stack(idx_d).astype(np.int32)


def _bias_kernel(tab_ref, idxp_ref, idxd_ref, bp_ref, bd_ref):
    h = pl.program_id(0)
    ip = idxp_ref[...]
    idd = idxd_ref[...]
    accp = jnp.full(ip.shape, NEG_INF, F32)
    accd = jnp.full(idd.shape, NEG_INF, F32)
    for b in range(N_BUCKETS):
        v = tab_ref[b, h]
        accp = jnp.where(ip == b, v, accp)
        accd = jnp.where(idd == b, v, accd)
    bp_ref[...] = accp
    bd_ref[...] = accd


def _build_bias(rel_bias):
    idx_p, idx_d = _bias_index_tables()
    n_heads = rel_bias.shape[1]
    blk = ATT_BLOCK
    nk = idx_d.shape[1]
    return pl.pallas_call(
        _bias_kernel,
        grid=(n_heads,),
        in_specs=[pl.BlockSpec(memory_space=pltpu.SMEM),
                  pl.BlockSpec((None, blk, 2 * blk), lambda h: (h // ATT_HPG, 0, 0)),
                  pl.BlockSpec((None, nk, LANES), lambda h: (h // ATT_HPG, 0, 0))],
        out_specs=[pl.BlockSpec((None, blk, 2 * blk), lambda h: (h, 0, 0)),
                   pl.BlockSpec((None, nk, LANES), lambda h: (h, 0, 0))],
        out_shape=[jax.ShapeDtypeStruct((n_heads, blk, 2 * blk), F32),
                   jax.ShapeDtypeStruct((n_heads, nk, LANES), F32)],
        compiler_params=_cparams("arbitrary"),
        name="t5_bias",
    )(rel_bias, jnp.asarray(idx_p), jnp.asarray(idx_d))


ATT_TILE = ATT_BLOCK * max(dil for _, dil in ATT_GROUPS)
MERGE_ROWS = 256
ATT_UNITS = 4


def _merge_three(parts):
    m_all = jnp.maximum(jnp.maximum(parts[0][1], parts[1][1]), parts[2][1])
    w = jnp.exp(parts[0][1] - m_all)
    num = w * parts[0][0]
    den = w * parts[0][2]
    for o, m, l in parts[1:]:
        w = jnp.exp(m - m_all)
        num = num + w * o
        den = den + w * l
    return num / den


def _attn_prompt_kernel(*refs):
    ng = len(ATT_GROUPS)
    blk = ATT_BLOCK
    hd = ATT_HEAD_DIM
    bias_ref = refs[5 * ng]
    y_ref = refs[5 * ng + 1]
    scr = refs[5 * ng + 2:]
    first_tile = pl.program_id(1) == 0
    ones = jnp.ones((2 * blk, hd), BF16)

    def rows_of(start, n, dil):
        return pl.ds(start, n) if dil == 1 else pl.ds(start, n, stride=dil)

    def unit(gi, dil, start, k2, v2, bias2):
        q_ref = refs[5 * gi]
        o_s, m_s, l_s = scr[3 * gi:3 * gi + 3]
        rows = rows_of(start, blk, dil)
        s = _bdot_nt(q_ref[rows, :] * ATT_SCALE, k2) + bias2
        m = jnp.max(s, axis=-1, keepdims=True)
        p = jnp.exp(s - m)
        ol = _bdot(p, jnp.concatenate([v2.astype(BF16), ones], axis=1))
        o_s[rows, :] = ol[:, :hd]
        m_s[rows, :] = jnp.broadcast_to(m, (blk, hd))
        l_s[rows, :] = ol[:, hd:]

    for gi, (window, dil) in enumerate(ATT_GROUPS):
        _, kc_ref, vc_ref, kp_ref, vp_ref = refs[5 * gi:5 * gi + 5]
        span = blk * dil
        n_sub = ATT_TILE // span
        bias = bias_ref[gi]
        bias_head = jnp.concatenate([jnp.where(first_tile, NEG_INF, bias[:, :blk]), bias[:, blk:]], axis=1)

        def head_unit(r, gi=gi, dil=dil, kc_ref=kc_ref, vc_ref=vc_ref, kp_ref=kp_ref, vp_ref=vp_ref,
                      bias_head=bias_head):
            prow, crow = rows_of(r, blk, dil), rows_of(r, blk, dil)
            k2 = jnp.concatenate([kp_ref[prow, :], kc_ref[crow, :]], axis=0)
            v2 = jnp.concatenate([vp_ref[prow, :], vc_ref[crow, :]], axis=0)
            unit(gi, dil, r, k2, v2, bias_head)

        def inner_unit(j, r, gi=gi, dil=dil, span=span, kc_ref=kc_ref, vc_ref=vc_ref, bias=bias):
            both = rows_of((j - 1) * span + r, 2 * blk, dil)
            unit(gi, dil, j * span + r, kc_ref[both, :], vc_ref[both, :], bias)

        if dil >= ATT_UNITS:
            def head_body(it, carry, head_unit=head_unit):
                for k in range(ATT_UNITS):
                    head_unit(it * ATT_UNITS + k)
                return carry
            lax.fori_loop(0, dil // ATT_UNITS, head_body, 0)
        else:
            for r in range(dil):
                head_unit(r)
        if n_sub > 1:
            per_it = max(ATT_UNITS // dil, 1)
            n_it = (n_sub - 1) // per_it

            def inner_body(it, carry, inner_unit=inner_unit, per_it=per_it, dil=dil):
                for k in range(per_it):
                    for r in range(dil):
                        inner_unit(1 + it * per_it + k, r)
                return carry
            lax.fori_loop(0, n_it, inner_body, 0)
            for j in range(1 + n_it * per_it, n_sub):
                for r in range(dil):
                    inner_unit(j, r)

    for c in range(ATT_TILE // MERGE_ROWS):
        rs = slice(c * MERGE_ROWS, (c + 1) * MERGE_ROWS)
        parts = [(scr[3 * gi][rs, :], scr[3 * gi + 1][rs, :], scr[3 * gi + 2][rs, :]) for gi in range(ng)]
        y_ref[rs, :] = _merge_three(parts).astype(y_ref.dtype)


def _attn_prompt(qkv, bias_p, bsz, s):
    hdim = ATT_HEAD_DIM
    ng = len(ATT_GROUPS)
    n_heads = qkv.shape[1] // (3 * hdim)
    tile = ATT_TILE
    assert s % tile == 0
    nt = s // tile
    in_specs, args = [], []
    for gi, (window, dil) in enumerate(ATT_GROUPS):
        span = ATT_BLOCK * dil
        per = tile // span

        def cur(part, gi=gi):
            return pl.BlockSpec((tile, hdim), lambda b, i, h: (b * nt + i, part * n_heads + gi * ATT_HPG + h))

        def prev(part, gi=gi, per=per):
            return pl.BlockSpec((span, hdim),
                                lambda b, i, h: (jnp.maximum((b * nt + i) * per - 1, 0),
                                                 part * n_heads + gi * ATT_HPG + h))

        in_specs += [cur(0), cur(1), cur(2), prev(1), prev(2)]
        args += [qkv] * 5
    in_specs.append(pl.BlockSpec((ng, None, ATT_BLOCK, 2 * ATT_BLOCK), lambda b, i, h: (0, h, 0, 0)))
    args.append(bias_p)
    return pl.pallas_call(
        _attn_prompt_kernel,
        grid=(bsz, nt, ATT_HPG),
        in_specs=in_specs,
        out_specs=pl.BlockSpec((tile, hdim), lambda b, i, h: (b * nt + i, h)),
        out_shape=jax.ShapeDtypeStruct((bsz * s, ATT_HPG * hdim), BF16),
        scratch_shapes=[pltpu.VMEM((tile, hdim), F32)] * (3 * ng),
        compiler_params=_cparams("parallel", "parallel", "arbitrary"),
        name="attn_prompt",
    )(*args)


REDUCE_WAYS = 4
DEC_KEYS = 32


def _reduce_rows(x, op):
    n = x.shape[0]
    part = op(x.reshape((REDUCE_WAYS, n // REDUCE_WAYS) + x.shape[1:]), axis=1)
    return op(part, axis=0)


def _attn_sample_kernel(qkv_ref, c0_ref, c1_ref, c2_ref, bias_ref, bias0_ref, y_ref):
    ng = len(ATT_GROUPS)
    hp = ATT_HPG
    caches = (c0_ref, c1_ref, c2_ref)
    zeros = jnp.zeros((hp, ATT_HEAD_DIM), F32)
    ones = jnp.ones((hp, ATT_HEAD_DIM), F32)

    def body(b, carry):
        parts = []
        for gi in range(ng):
            q = qkv_ref[b, gi]
            k_new = qkv_ref[b, ng + gi]
            v_new = qkv_ref[b, 2 * ng + gi]
            q2 = jnp.concatenate([q * ATT_SCALE, zeros], axis=0)
            s_self = jnp.sum(k_new * q, axis=-1, keepdims=True) * ATT_SCALE + bias0_ref[gi]
            m = jnp.concatenate([s_self, zeros], axis=0)
            l = jnp.concatenate([ones, ones], axis=0)
            o = jnp.concatenate([zeros, v_new], axis=0)
            for c in range(caches[gi].shape[1] // DEC_KEYS):
                ks = slice(c * DEC_KEYS, (c + 1) * DEC_KEYS)
                kv = caches[gi][b, ks]
                s = jnp.sum(kv * q2[None], axis=-1, keepdims=True) + bias_ref[gi, ks]
                m_new = jnp.maximum(m, _reduce_rows(s, jnp.max))
                alpha = jnp.exp(m - m_new)
                p = jnp.exp(s - m_new[None])
                l = alpha * l + _reduce_rows(p, jnp.sum)
                p_on_v = pltpu.roll(p, hp, 1)
                o = pltpu.roll(alpha, hp, 0) * o + _reduce_rows(p_on_v * kv, jnp.sum)
                m = m_new
            parts.append((o[hp:], m[:hp], l[:hp]))
        y_ref[b] = _merge_three(parts)
        return carry

    lax.fori_loop(0, DEC_TILE, body, 0)


def _attn_sample(qkv2, caches, layer, bias_d, bias0):
    db = qkv2.shape[0]
    bt = DEC_TILE
    hdim = ATT_HEAD_DIM
    ng = len(ATT_GROUPS)
    views, specs = [], []
    for (window, dil), cache in zip(ATT_GROUPS, caches):
        nk = window // dil
        depth, _, wb = cache.shape[:3]
        assert wb == window, "cached window must equal the group's window"
        views.append(cache.reshape(depth, db, nk, dil, 2 * ATT_HPG, hdim))
        specs.append(pl.BlockSpec((None, bt, nk, None, 2 * ATT_HPG, hdim),
                                  lambda i: (layer, i, 0, 0, 0, 0)))
    y = pl.pallas_call(
        _attn_sample_kernel,
        grid=(db // bt,),
        in_specs=[pl.BlockSpec((bt, 3 * ng, ATT_HPG, hdim), lambda i: (i, 0, 0, 0))] + specs +
                 [pl.BlockSpec(bias_d.shape, lambda i: (0, 0, 0, 0)),
                  pl.BlockSpec(bias0.shape, lambda i: (0, 0, 0))],
        out_specs=pl.BlockSpec((bt, ATT_HPG, hdim), lambda i: (i, 0, 0)),
        out_shape=jax.ShapeDtypeStruct((db, ATT_HPG, hdim), F32),
        compiler_params=_cparams("parallel"),
        name="attn_sample",
    )(qkv2.reshape(db, 3 * ng, ATT_HPG, hdim), *views, bias_d, bias0)
    return y.reshape(db, ATT_HPG * hdim)


def _branch_kernel(ysc_ref, yssm_ref, yatt_ref, ysc_s, yssm_s, yatt_s, wsc_ref, wssm_ref, watt_ref,
                   g0_ref, g1_ref, g2_ref, g0_s, g1_s, g2_s, o_ref, os_ref):
    def merged(ysc, yssm, yatt, g0, g1, g2, out):
        tn = out.shape[1]
        cw = min(tn, MXU_COLS)
        for ci in range(tn // cw):
            cs = slice(ci * cw, (ci + 1) * cw)
            a = jnp.dot(ysc[...], wsc_ref[:, cs], preferred_element_type=F32)
            b = jnp.dot(yssm[...], wssm_ref[:, cs], preferred_element_type=F32)
            c = jnp.dot(yatt[...], watt_ref[:, cs], preferred_element_type=F32)
            out[:, cs] = (_sigmoid(g0[:, cs]) * a + _sigmoid(g1[:, cs]) * b
                          + _sigmoid(g2[:, cs]) * c).astype(out.dtype)

    merged(ysc_ref, yssm_ref, yatt_ref, g0_ref, g1_ref, g2_ref, o_ref)

    @pl.when(pl.program_id(1) == 0)
    def _():
        merged(ysc_s, yssm_s, yatt_s, g0_s, g1_s, g2_s, os_ref)


def _branch_merge(ys, ys_s, wsc, wssm, watt, layer, p2, p2_s, tm, tn):
    m, d = ys[0].shape
    ms = ys_s[0].shape[0]
    da = ys[2].shape[1]
    nj = d // tn
    rows = lambda width: pl.BlockSpec((tm, width), lambda j, i: (i, 0))
    rows_s = lambda width: pl.BlockSpec((ms, width), lambda j, i: (0, 0))
    wcol = lambda k: pl.BlockSpec((None, k, tn), lambda j, i: (layer, 0, j))
    gate = lambda br: pl.BlockSpec((tm, tn), lambda j, i: (i, br * nj + j))
    gate_s = lambda br: pl.BlockSpec((ms, tn), lambda j, i: (0, br * nj + j))
    return pl.pallas_call(
        _branch_kernel,
        grid=(nj, m // tm),
        in_specs=[rows(d), rows(d), rows(da), rows_s(d), rows_s(d), rows_s(da), wcol(d), wcol(d), wcol(da),
                  gate(0), gate(1), gate(2), gate_s(0), gate_s(1), gate_s(2)],
        out_specs=[pl.BlockSpec((tm, tn), lambda j, i: (i, j)), pl.BlockSpec((ms, tn), lambda j, i: (0, j))],
        out_shape=[jax.ShapeDtypeStruct((m, d), BF16), jax.ShapeDtypeStruct((ms, d), BF16)],
        compiler_params=_cparams("parallel", "arbitrary"),
        name="branch_merge",
    )(*ys, *ys_s, wsc, wssm, watt, p2, p2, p2, p2_s, p2_s, p2_s)


def _outproj_kernel(x_ref, mg_ref, x_s, mg_s, w_ref, g_ref, x1_ref, h2_ref, x1_s, h2_s):
    def project(x, mg, x1_out, h2_out):
        m = x.shape[0]
        rc = min(m, NORM_ROWS)
        for r in range(m // rc):
            rows = slice(r * rc, (r + 1) * rc)
            x1 = x[rows, :] + jnp.dot(mg[rows, :], w_ref[...], preferred_element_type=F32)
            x1_out[rows, :] = x1
            ms = jnp.mean(x1 * x1, axis=-1, keepdims=True)
            h2_out[rows, :] = (x1 * lax.rsqrt(ms + EPS) * g_ref[...]).astype(h2_out.dtype)

    project(x_ref, mg_ref, x1_ref, h2_ref)

    @pl.when(pl.program_id(0) == 0)
    def _():
        project(x_s, mg_s, x1_s, h2_s)


def _outproj(x2d, merged, x2d_s, merged_s, w_out, layer, g2, tm):
    m, d = x2d.shape
    ms = x2d_s.shape[0]
    rows = pl.BlockSpec((tm, d), lambda i: (i, 0))
    rows_s = pl.BlockSpec((ms, d), lambda i: (0, 0))
    return pl.pallas_call(
        _outproj_kernel,
        grid=(m // tm,),
        in_specs=[rows, rows, rows_s, rows_s, pl.BlockSpec((None, d, d), lambda i: (layer, 0, 0)),
                  pl.BlockSpec((1, d), lambda i: (0, 0))],
        out_specs=[rows, rows, rows_s, rows_s],
        out_shape=[jax.ShapeDtypeStruct((m, d), F32), jax.ShapeDtypeStruct((m, d), BF16),
                   jax.ShapeDtypeStruct((ms, d), F32), jax.ShapeDtypeStruct((ms, d), BF16)],
        compiler_params=_cparams("arbitrary"),
        name="outproj_norm",
    )(x2d, merged, x2d_s, merged_s, w_out, g2.reshape(1, d))


def _mlp_kernel(x1_ref, h2_ref, x1_s, h2_s, wu_ref, wd_ref, o_ref, os_ref):
    def accumulate(x1, h2, out):
        @pl.when(pl.program_id(1) == 0)
        def _():
            out[...] = x1[...]

        tf = wu_ref.shape[1]
        cf = min(tf, FF_CHUNK)
        us = []
        for c in range(tf // cf):
            u = jnp.dot(h2[...], wu_ref[:, c * cf:(c + 1) * cf], preferred_element_type=F32)
            us.append(jnp.square(jnp.maximum(u, 0.0)).astype(BF16))
        out[...] += jnp.dot(jnp.concatenate(us, axis=1), wd_ref[...], preferred_element_type=F32)

    accumulate(x1_ref, h2_ref, o_ref)

    @pl.when(pl.program_id(0) == 0)
    def _():
        accumulate(x1_s, h2_s, os_ref)


def _mlp(x1, h2, x1_s, h2_s, w_up, w_down, layer, tm, tf):
    m, d = x1.shape
    ms = x1_s.shape[0]
    f = w_up.shape[2]
    rows = pl.BlockSpec((tm, d), lambda i, j: (i, 0))
    rows_s = pl.BlockSpec((ms, d), lambda i, j: (0, 0))
    return pl.pallas_call(
        _mlp_kernel,
        grid=(m // tm, f // tf),
        in_specs=[rows, rows, rows_s, rows_s, pl.BlockSpec((None, d, tf), lambda i, j: (layer, 0, j)),
                  pl.BlockSpec((None, tf, d), lambda i, j: (layer, j, 0))],
        out_specs=[rows, rows_s],
        out_shape=[jax.ShapeDtypeStruct((m, d), F32), jax.ShapeDtypeStruct((ms, d), F32)],
        compiler_params=_cparams("arbitrary", "arbitrary"),
        name="mlp",
    )(x1, h2, x1_s, h2_s, w_up, w_down)


def _row_tile(m, pref):
    return pref if m % pref == 0 else m


def _pad_lanes(v):
    return jnp.pad(v.astype(F32), (0, LANES - v.shape[0])).reshape(1, LANES)


def _prep_layer(l, shared, norm1_g, sc_conv_w, ssm_conv_w, ssm_conv_b, ssm_dt_bias, ssm_A_log, ssm_D,
                ssm_norm_g, q_norm_g, k_norm_g, norm2_g):
    wt_in = shared["wt_in"]
    d = wt_in.shape[2]
    heads = d // SSM_HEAD_DIM
    aw = ATT_HPG * len(ATT_GROUPS) * ATT_HEAD_DIM
    n_main = N_BRANCH * d + 3 * d + d + (d + 2 * SSM_GROUPS * SSM_STATE)
    assert n_main % LANES == 0 and heads <= LANES
    n_att_heads = aw // ATT_HEAD_DIM
    gains = jnp.concatenate([jnp.tile(q_norm_g[l], n_att_heads), jnp.tile(k_norm_g[l], n_att_heads),
                             jnp.ones((aw,), F32)]).reshape(1, 3 * aw)
    return dict(
        shared, layer=l, n_main=n_main, qkv_row=n_main + heads, n_qkv=3 * aw,
        norm1_g=norm1_g[l], norm2_g=norm2_g[l], qk_gains=gains,
        sc_conv_w=sc_conv_w[l], ssm_conv_w=ssm_conv_w[l], ssm_conv_b=ssm_conv_b[l].reshape(1, -1),
        dt_bias=_pad_lanes(ssm_dt_bias[l]), a_log=_pad_lanes(ssm_A_log[l]),
        d_exp=jnp.repeat(ssm_D[l].astype(F32), SSM_HEAD_DIM).reshape(1, d),
        ssm_norm_g=ssm_norm_g[l].reshape(1, d),
    )


def _in_proj(x2d, x2d_s, w):
    m = x2d.shape[0]
    h = _rmsnorm(x2d, w["norm1_g"], _row_tile(m, 512))
    h_s = _rmsnorm(x2d_s, w["norm1_g"], x2d_s.shape[0])
    p = _matmul_wcast(h, h_s, w["wt_in"], w["layer"], 0, w["n_main"] + MXU_COLS, MXU_COLS, name="in_proj_main")
    qkv = _matmul_qkv(h, h_s, w["wt_in"], w["layer"], w["qkv_row"], w["n_qkv"], w["qk_gains"], MXU_COLS)
    return (p[0], qkv[0]), (p[1], qkv[1])


def _tail(x2d, p, ys, x2d_s, p_s, ys_s, w):
    m = x2d.shape[0]
    l = w["layer"]
    merged, merged_s = _branch_merge(ys, ys_s, w["w_br_sc"], w["w_br_ssm"], w["w_br_att"], l, p, p_s,
                                     _row_tile(m, 1024), 512)
    x1, h2, x1_s, h2_s = _outproj(x2d, merged, x2d_s, merged_s, w["w_out"], l, w["norm2_g"], _row_tile(m, 512))
    return _mlp(x1, h2, x1_s, h2_s, w["w_up"], w["w_down"], l, _row_tile(m, 512), 1024)


def _kv_rows(qkv3, gi, keep):
    aw = qkv3.shape[2] // 3
    gw = ATT_HPG * ATT_HEAD_DIM
    bsz, ln, _ = qkv3.shape
    k = qkv3[:, ln - keep:, aw + gi * gw:aw + (gi + 1) * gw].reshape(bsz, keep, ATT_HPG, ATT_HEAD_DIM)
    v = qkv3[:, ln - keep:, 2 * aw + gi * gw:2 * aw + (gi + 1) * gw].reshape(bsz, keep, ATT_HPG, ATT_HEAD_DIM)
    return jnp.stack([k, v], axis=2)


def _prompt_mixers(proj, w, bias_p, bsz, s, d):
    p, qkv = proj
    p3 = p.reshape(bsz, s, -1)
    qkv3 = qkv.reshape(bsz, s, -1)
    ysc, sc_new = _sc_prompt(p3, w["sc_conv_w"], d, 1024, 1024)
    yssm, ht = _ssd_prompt(p3, w["n_main"] // LANES, w["ssm_conv_w"], w["ssm_conv_b"], w["dt_bias"],
                           w["a_log"], w["d_exp"], w["ssm_norm_g"], d)
    xbc_off = N_BRANCH * d + 3 * d + d
    ssm_conv_new = p3[:, s - (SSM_CONV - 1):, xbc_off:w["n_main"]]
    kv_new = [_kv_rows(qkv3, gi, min(window, s)) for gi, (window, _) in enumerate(ATT_GROUPS)]
    yatt = _attn_prompt(qkv, bias_p, bsz, s)
    heads = d // SSM_HEAD_DIM
    st = (sc_new, ssm_conv_new, ht.reshape(bsz, heads, SSM_HEAD_DIM, SSM_STATE), kv_new[0], kv_new[1], kv_new[2])
    return (ysc.reshape(bsz * s, d), yssm.reshape(bsz * s, d), yatt), st


def _sample_mixers(proj, w, bias_d, bias0, sc_buf, ssm_buf, ssm_h0_all, ssm_ht_all, kv_caches, d):
    p, qkv = proj
    db = p.shape[0]
    l = w["layer"]
    ysc, u_new = _sc_sample(p, sc_buf.reshape(db, -1), w["sc_conv_w"], d, 512)
    sc_new = jnp.concatenate([sc_buf[:, 1:], u_new[:, None, :]], axis=1)
    heads = d // SSM_HEAD_DIM
    depth = ssm_h0_all.shape[0]
    yssm, ssm_ht_all = _ssd_sample(p, w["n_main"] // LANES, ssm_buf.reshape(db, -1),
                                   ssm_h0_all.reshape(depth, db, heads * SSM_HEAD_DIM, SSM_STATE), l, ssm_ht_all,
                                   w["ssm_conv_w"], w["ssm_conv_b"], w["dt_bias"], w["a_log"], w["d_exp"],
                                   w["ssm_norm_g"], d)
    xbc_off = N_BRANCH * d + 3 * d + d
    ssm_conv_new = jnp.concatenate([ssm_buf[:, 1:], p[:, None, xbc_off:w["n_main"]]], axis=1)
    yatt = _attn_sample(qkv, kv_caches, l, bias_d, bias0)
    qkv3 = qkv.reshape(db, 1, -1)
    kv_new = [_kv_rows(qkv3, gi, 1) for gi in range(len(ATT_GROUPS))]
    st = (sc_new, ssm_conv_new, kv_new[0], kv_new[1], kv_new[2])
    return (ysc.astype(BF16), yssm.astype(BF16), yatt.astype(BF16)), st, ssm_ht_all


def kernel(x_prompt, x_sample, state_sc_conv, state_ssm_conv, state_ssm, cache_kv_w128, cache_kv_w512,
           cache_kv_w2048, norm1_g, w_in, sc_conv_w, ssm_conv_w, ssm_conv_b, ssm_dt_bias, ssm_A_log, ssm_D,
           ssm_norm_g, q_norm_g, k_norm_g, rel_bias, w_br_sc, w_br_ssm, w_br_att, w_out, norm2_g, w_up, w_down):
    depth = w_in.shape[0]
    ng = len(ATT_GROUPS)
    bias_p, bias_d = _build_bias(rel_bias)
    bias_p = bias_p.reshape((ng, ATT_HPG) + bias_p.shape[1:])
    bias_d = bias_d.reshape((ng, ATT_HPG) + bias_d.shape[1:]).transpose(0, 2, 1, 3)
    bias_d = jnp.concatenate([bias_d, jnp.zeros_like(bias_d)], axis=2)
    bias0 = jnp.broadcast_to(rel_bias[0].reshape(ng, ATT_HPG, 1), (ng, ATT_HPG, LANES))
    bsz, s, d = x_prompt.shape
    db, t, _ = x_sample.shape
    assert t == 1, "decode kernels handle one new token per sample"
    xp, xs = x_prompt.reshape(bsz * s, d), x_sample.reshape(db, d)
    p_new, s_new, s_ssm = [], [], None
    shared = dict(wt_in=jnp.swapaxes(w_in, 1, 2),
                  w_br_sc=w_br_sc.astype(BF16), w_br_ssm=w_br_ssm.astype(BF16), w_br_att=w_br_att.astype(BF16),
                  w_out=w_out.astype(BF16), w_up=w_up.astype(BF16), w_down=w_down.astype(BF16))
    for l in range(depth):
        w = _prep_layer(l, shared, norm1_g, sc_conv_w, ssm_conv_w, ssm_conv_b, ssm_dt_bias, ssm_A_log, ssm_D,
                        ssm_norm_g, q_norm_g, k_norm_g, norm2_g)
        proj, proj_s = _in_proj(xp, xs, w)
        ys, st = _prompt_mixers(proj, w, bias_p, bsz, s, d)
        p_new.append(st)
        ys_s, st, s_ssm = _sample_mixers(proj_s, w, bias_d, bias0, state_sc_conv[l], state_ssm_conv[l], state_ssm,
                                         s_ssm, (cache_kv_w128, cache_kv_w512, cache_kv_w2048), d)
        s_new.append(st)
        xp, xs = _tail(xp, proj[0], ys, xs, proj_s[0], ys_s, w)
    p_out = [jnp.stack(a) for a in zip(*p_new)]
    s_out = [jnp.stack(a) for a in zip(*s_new)]
    s_out.insert(2, s_ssm.reshape(state_ssm.shape))
    return (xp.reshape(bsz, s, d), xs.reshape(db, 1, d), *p_out, *s_out)
```

```python
import functools
import math

import numpy as np
import jax
import jax.numpy as jnp
from jax import lax
from jax.experimental import pallas as pl
from jax.experimental.pallas import tpu as pltpu

F32 = jnp.float32
BF16 = jnp.bfloat16

N_BRANCH = 3
SC_CONV = 3
SSM_HEAD_DIM = 64
SSM_GROUPS = 8
SSM_STATE = 128
SSM_CONV = 4
SSM_CHUNK = 128
ATT_HEAD_DIM = 128
ATT_GROUPS = ((128, 1), (512, 4), (2048, 16))
ATT_HPG = 4
ATT_BLOCK = 128
ATT_SCALE = 1.0 / math.sqrt(ATT_HEAD_DIM)
N_BUCKETS = 32
MAX_DISTANCE = 2048
EPS = 1e-6

LANES = 128
SUBLANES = 8
MXU_COLS = 256
VMEM_LIMIT_BYTES = 60 * 1024 * 1024

DEC_TILE = 8
NEG_INF = float("-inf")


def _cparams(*sem):
    return pltpu.CompilerParams(dimension_semantics=sem, vmem_limit_bytes=VMEM_LIMIT_BYTES)


def _sigmoid(v):
    return 1.0 / (1.0 + jnp.exp(-v))


def _silu(v):
    return v * _sigmoid(v)


def _bdot(a, b):
    return jnp.dot(a.astype(BF16), b.astype(BF16), preferred_element_type=F32)


def _bdot_nt(a, b):
    return lax.dot_general(a.astype(BF16), b.astype(BF16), (((1,), (1,)), ((), ())),
                           preferred_element_type=F32)


def _rmsnorm_kernel(x_ref, g_ref, o_ref):
    x = x_ref[...]
    ms = jnp.mean(x * x, axis=-1, keepdims=True)
    o_ref[...] = (x * lax.rsqrt(ms + EPS) * g_ref[...]).astype(o_ref.dtype)


def _rmsnorm(x2d, g, tm):
    m, d = x2d.shape
    return pl.pallas_call(
        _rmsnorm_kernel,
        grid=(m // tm,),
        in_specs=[pl.BlockSpec((tm, d), lambda i: (i, 0)),
                  pl.BlockSpec((1, d), lambda i: (0, 0))],
        out_specs=pl.BlockSpec((tm, d), lambda i: (i, 0)),
        out_shape=jax.ShapeDtypeStruct((m, d), BF16),
        compiler_params=_cparams("parallel"),
        name="rmsnorm",
    )(x2d, g.reshape(1, d))


ROW_CHUNK = 1024
NORM_ROWS = 256
FF_CHUNK = 512


def _resident_rows_dot(a_ref, as_ref, w_ref, wb_ref, emit):
    wb_ref[...] = w_ref[...].astype(BF16)
    m = a_ref.shape[0]
    rc = min(m, ROW_CHUNK)
    for r in range(m // rc):
        rows = slice(r * rc, (r + 1) * rc)
        emit(0, rows, _bdot_nt(a_ref[rows, :], wb_ref[...]))
    emit(1, slice(0, as_ref.shape[0]), _bdot_nt(as_ref[...], wb_ref[...]))


def _mm_wcast_kernel(a_ref, as_ref, w_ref, o_ref, os_ref, wb_ref):
    outs = (o_ref, os_ref)

    def emit(which, rows, acc):
        outs[which][rows, :] = acc

    _resident_rows_dot(a_ref, as_ref, w_ref, wb_ref, emit)


def _resident(shape):
    return pl.BlockSpec(shape, lambda j: (0,) * len(shape), pipeline_mode=pl.Buffered(1))


def _matmul_wcast(a, a_s, wt_all, layer, first_block, n_cols, tn, name):
    m, k = a.shape
    ms = a_s.shape[0]
    assert m % min(m, ROW_CHUNK) == 0
    return pl.pallas_call(
        _mm_wcast_kernel,
        grid=(n_cols // tn,),
        in_specs=[_resident((m, k)), _resident((ms, k)),
                  pl.BlockSpec((None, tn, k), lambda j: (layer, first_block + j, 0))],
        out_specs=[pl.BlockSpec((m, tn), lambda j: (0, j)), pl.BlockSpec((ms, tn), lambda j: (0, j))],
        out_shape=[jax.ShapeDtypeStruct((m, n_cols), F32), jax.ShapeDtypeStruct((ms, n_cols), F32)],
        scratch_shapes=[pltpu.VMEM((tn, k), BF16)],
        compiler_params=_cparams("arbitrary"),
        name=name,
    )(a, a_s, wt_all)


def _mm_qkv_kernel(a_ref, as_ref, w_ref, g_ref, o_ref, os_ref, wb_ref, *, n_norm_blocks):
    is_qk = pl.program_id(0) < n_norm_blocks
    outs = (o_ref, os_ref)

    def emit(which, rows, acc):
        for h in range(acc.shape[1] // ATT_HEAD_DIM):
            sl = slice(h * ATT_HEAD_DIM, (h + 1) * ATT_HEAD_DIM)
            t = acc[:, sl]
            ms = jnp.mean(t * t, axis=-1, keepdims=True)
            outs[which][rows, sl] = t * jnp.where(is_qk, lax.rsqrt(ms + EPS), 1.0) * g_ref[:, sl]

    _resident_rows_dot(a_ref, as_ref, w_ref.at[0], wb_ref, emit)


def _matmul_qkv(a, a_s, wt_all, layer, first_row, n, gains, tn):
    m, k = a.shape
    ms = a_s.shape[0]
    assert first_row % SUBLANES == 0 and tn % ATT_HEAD_DIM == 0
    att_width = n // 3
    return pl.pallas_call(
        functools.partial(_mm_qkv_kernel, n_norm_blocks=2 * att_width // tn),
        grid=(n // tn,),
        in_specs=[_resident((m, k)), _resident((ms, k)),
                  pl.BlockSpec((pl.Element(1), pl.Element(tn), pl.Element(k)),
                               lambda j: (layer, pl.multiple_of(first_row + j * tn, SUBLANES), 0)),
                  pl.BlockSpec((1, tn), lambda j: (0, j))],
        out_specs=[pl.BlockSpec((m, tn), lambda j: (0, j)), pl.BlockSpec((ms, tn), lambda j: (0, j))],
        out_shape=[jax.ShapeDtypeStruct((m, n), F32), jax.ShapeDtypeStruct((ms, n), F32)],
        scratch_shapes=[pltpu.VMEM((tn, k), BF16)],
        compiler_params=_cparams("arbitrary"),
        name="matmul_qkv",
    )(a, a_s, wt_all, gains)


def _sc_prompt_kernel(b_ref, c_ref, x_ref, hc_ref, hx_ref, w_ref, y_ref, tail_ref, ext_ref, *, ts):
    t = pl.program_id(2)
    u = c_ref[...] * x_ref[...]
    hu = hc_ref[...] * hx_ref[...]
    ext_ref[0:SUBLANES, :] = jnp.where(t > 0, hu, 0.0)
    ext_ref[SUBLANES:SUBLANES + ts, :] = u
    u2 = ext_ref[pl.ds(SUBLANES - 2, ts), :]
    u1 = ext_ref[pl.ds(SUBLANES - 1, ts), :]
    y = u2 * w_ref[0:1, :] + u1 * w_ref[1:2, :] + u * w_ref[2:3, :]
    y_ref[...] = (b_ref[...] * y).astype(y_ref.dtype)
    tail_ref[...] = ext_ref[pl.ds(ts, SUBLANES), :]


def _sc_prompt(p3, conv_w, d, ts, tc):
    bsz, s, _ = p3.shape
    ncb = d // tc
    off_b, off_c, off_x = 3 * ncb, 4 * ncb, 5 * ncb
    hb = ts // SUBLANES

    def halo(off):
        return pl.BlockSpec((None, SUBLANES, tc),
                            lambda b, c, t: (b, jnp.maximum(t * hb - 1, 0), off + c))

    def main(off):
        return pl.BlockSpec((None, ts, tc), lambda b, c, t: (b, t, off + c))

    y, tail = pl.pallas_call(
        functools.partial(_sc_prompt_kernel, ts=ts),
        grid=(bsz, ncb, s // ts),
        in_specs=[main(off_b), main(off_c), main(off_x), halo(off_c), halo(off_x),
                  pl.BlockSpec((SC_CONV, tc), lambda b, c, t: (0, c))],
        out_specs=[pl.BlockSpec((None, ts, tc), lambda b, c, t: (b, t, c)),
                   pl.BlockSpec((None, SUBLANES, tc), lambda b, c, t: (b, 0, c))],
        out_shape=[jax.ShapeDtypeStruct((bsz, s, d), BF16),
                   jax.ShapeDtypeStruct((bsz, SUBLANES, d), F32)],
        scratch_shapes=[pltpu.VMEM((ts + SUBLANES, tc), F32)],
        compiler_params=_cparams("parallel", "parallel", "arbitrary"),
        name="short_conv_prompt",
    )(p3, p3, p3, p3, p3, conv_w)
    return y, tail[:, SUBLANES - (SC_CONV - 1):, :]


def _sc_sample_kernel(b_ref, c_ref, x_ref, s0_ref, s1_ref, w_ref, y_ref, u_ref):
    u = c_ref[...] * x_ref[...]
    y = s0_ref[...] * w_ref[0:1, :] + s1_ref[...] * w_ref[1:2, :] + u * w_ref[2:3, :]
    y_ref[...] = b_ref[...] * y
    u_ref[...] = u


def _sc_sample(p2, state2d, conv_w, d, tc):
    db = p2.shape[0]
    ncb = d // tc

    def col(off):
        return pl.BlockSpec((db, tc), lambda c: (0, off + c))

    return pl.pallas_call(
        _sc_sample_kernel,
        grid=(ncb,),
        in_specs=[col(3 * ncb), col(4 * ncb), col(5 * ncb), col(0), col(ncb),
                  pl.BlockSpec((SC_CONV, tc), lambda c: (0, c))],
        out_specs=[col(0), col(0)],
        out_shape=[jax.ShapeDtypeStruct((db, d), F32), jax.ShapeDtypeStruct((db, d), F32)],
        compiler_params=_cparams("parallel"),
        name="short_conv_sample",
    )(p2, p2, p2, state2d, state2d, conv_w)


def _softplus(v):
    return jnp.maximum(v, 0.0) + jnp.log1p(jnp.exp(-jnp.abs(v)))


def _gated_group_norm(y, xs, z, dexp, ng):
    val = (y + dexp * xs) * _silu(z)
    ms = jnp.mean(val * val, axis=-1, keepdims=True)
    return val * lax.rsqrt(ms + EPS) * ng


def _ssd_prompt_kernel(x_ref, bc_ref, z_ref, dtr_ref, cwx_ref, cwbc_ref, cbx_ref, cbbc_ref, dtb_ref,
                       alog_ref, dexp_ref, ng_ref, y_ref, ht_ref, extx, extbc, xs_s, bc_s, state):
    q = SSM_CHUNK
    c = pl.program_id(1)
    pad = SUBLANES
    taps = SSM_CONV

    @pl.when(c == 0)
    def _():
        extx[0:pad, :] = jnp.zeros((pad, extx.shape[1]), F32)
        extbc[0:pad, :] = jnp.zeros((pad, extbc.shape[1]), F32)
        state[...] = jnp.zeros(state.shape, F32)

    extx[pad:pad + q, :] = x_ref[...]
    extbc[pad:pad + q, :] = bc_ref[...]

    def conv(ext, w_ref, b_ref):
        full = ext[...]
        acc = pltpu.roll(full, taps - 1, 0)[pad:, :] * w_ref[0:1, :]
        for i in range(1, taps - 1):
            acc = acc + pltpu.roll(full, taps - 1 - i, 0)[pad:, :] * w_ref[i:i + 1, :]
        acc = acc + full[pad:, :] * w_ref[taps - 1:taps, :]
        return _silu(acc + b_ref[...])

    xs_s[...] = conv(extx, cwx_ref, cbx_ref)
    bc_s[...] = conv(extbc, cwbc_ref, cbbc_ref)
    extx[0:pad, :] = extx[q:q + pad, :]
    extbc[0:pad, :] = extbc[q:q + pad, :]

    dt = _softplus(dtr_ref[...] + dtb_ref[...])
    a_neg = -jnp.exp(alog_ref[...])
    ii = lax.broadcasted_iota(jnp.int32, (q, q), 0)
    jj = lax.broadcasted_iota(jnp.int32, (q, q), 1)
    causal = ii >= jj
    acum = jnp.dot(causal.astype(F32), dt * a_neg, precision=lax.Precision.HIGHEST,
                   preferred_element_type=F32)
    acum_t = acum.T
    dt_t = dt.T
    lo_half = jj < SSM_HEAD_DIM
    gs = SSM_STATE
    hd = SSM_HEAD_DIM
    hpg = xs_s.shape[1] // (SSM_GROUPS * hd)
    gw = hpg * hd
    n_bc = SSM_GROUPS * gs

    for g in range(SSM_GROUPS):
        bg = bc_s[:, g * gs:(g + 1) * gs].astype(BF16)
        cg = bc_s[:, n_bc + g * gs:n_bc + (g + 1) * gs].astype(BF16)
        cb = _bdot_nt(cg, bg)
        ys = []
        for hp in range(hpg // 2):
            e0 = g * hpg + 2 * hp
            rows = slice(e0 * hd, (e0 + 2) * hd)
            xp = xs_s[:, rows]
            xp16 = xp.astype(BF16)
            cols_a, cols_dt, parts = [], [], []
            for k in range(2):
                e = e0 + k
                col_a = jnp.broadcast_to(acum[:, e:e + 1], (q, q))
                cols_a.append(col_a)
                cols_dt.append(jnp.broadcast_to(dt[:, e:e + 1], (q, q)))
                dec = jnp.exp(jnp.where(causal, col_a - acum_t[e:e + 1, :], NEG_INF))
                wm = cb * dec * dt_t[e:e + 1, :]
                parts.append(jnp.dot(wm.astype(BF16), xp16, preferred_element_type=F32))
            y_diag = jnp.where(lo_half, parts[0], parts[1])
            sel_a = jnp.where(lo_half, cols_a[0], cols_a[1])
            sel_dt = jnp.where(lo_half, cols_dt[0], cols_dt[1])
            hpair = state[rows, :]
            y_off = _bdot_nt(cg, hpair) * jnp.exp(sel_a)
            ys.append(y_diag + y_off)
            last_a = sel_a[q - 1:q, :]
            xw = xp * jnp.exp(last_a - sel_a) * sel_dt
            st = jnp.dot(xw.T.astype(BF16), bg, preferred_element_type=F32)
            cd = jnp.concatenate([jnp.broadcast_to(cols_a[0][q - 1:q, :], (hd, gs)),
                                  jnp.broadcast_to(cols_a[1][q - 1:q, :], (hd, gs))], axis=0)
            state[rows, :] = hpair * jnp.exp(cd) + st
        gsl = slice(g * gw, (g + 1) * gw)
        yg = jnp.concatenate(ys, axis=1)
        y_ref[:, gsl] = _gated_group_norm(yg, xs_s[:, gsl], z_ref[:, gsl], dexp_ref[:, gsl],
                                          ng_ref[:, gsl]).astype(y_ref.dtype)

    @pl.when(c == pl.num_programs(1) - 1)
    def _():
        ht_ref[...] = state[...]


def _ssd_prompt(p3, dt_block, cw, cb, dtb, alog, dexp, ng, d):
    bsz, s, _ = p3.shape
    q = SSM_CHUNK
    heads = d // SSM_HEAD_DIM
    const = lambda shape: pl.BlockSpec(shape, lambda b, c: (0, 0))
    y, ht = pl.pallas_call(
        _ssd_prompt_kernel,
        grid=(bsz, s // q),
        in_specs=[pl.BlockSpec((None, q, d), lambda b, c: (b, c, 7)),
                  pl.BlockSpec((None, q, d), lambda b, c: (b, c, 8)),
                  pl.BlockSpec((None, q, d), lambda b, c: (b, c, 6)),
                  pl.BlockSpec((None, q, LANES), lambda b, c: (b, c, dt_block)),
                  const((SSM_CONV, d)), const((SSM_CONV, d)), const((1, d)), const((1, d)),
                  const((1, LANES)), const((1, LANES)), const((1, d)), const((1, d))],
        out_specs=[pl.BlockSpec((None, q, d), lambda b, c: (b, c, 0)),
                   pl.BlockSpec((None, heads * SSM_HEAD_DIM, SSM_STATE), lambda b, c: (b, 0, 0))],
        out_shape=[jax.ShapeDtypeStruct((bsz, s, d), BF16),
                   jax.ShapeDtypeStruct((bsz, heads * SSM_HEAD_DIM, SSM_STATE), F32)],
        scratch_shapes=[pltpu.VMEM((q + SUBLANES, d), F32), pltpu.VMEM((q + SUBLANES, d), F32),
                        pltpu.VMEM((q, d), F32), pltpu.VMEM((q, d), F32),
                        pltpu.VMEM((heads * SSM_HEAD_DIM, SSM_STATE), F32)],
        compiler_params=_cparams("parallel", "arbitrary"),
        name="ssd_prompt",
    )(p3, p3, p3, p3, cw[:, :d], cw[:, d:], cb[:, :d], cb[:, d:], dtb, alog, dexp, ng)
    return y, ht


def _ssd_sample_kernel(*refs, layer, owns_all_layers):
    (x_ref, bc_ref, z_ref, dtr_ref, s0x, s0bc, s1x, s1bc, s2x, s2bc, h0_ref, cwx_ref, cwbc_ref, cbx_ref,
     cbbc_ref, dtb_ref, alog_ref, dexp_ref, ng_ref) = refs[:19]
    y_ref, ht_all_ref, padx, padbc, padd, pada, y_s = refs[-7:]
    if owns_all_layers:
        ht_ref = ht_all_ref.at[layer]
        for other in range(ht_all_ref.shape[0]):
            if other != layer:
                ht_all_ref[other] = jnp.zeros(ht_all_ref.shape[1:], F32)
    else:
        ht_ref = ht_all_ref
    bt = DEC_TILE
    hd = SSM_HEAD_DIM
    gs = SSM_STATE
    d = x_ref.shape[1]
    n_chunk = d // LANES
    hpg = d // (SSM_GROUPS * hd)
    gw = hpg * hd
    n_bc = SSM_GROUPS * gs

    @pl.when(pl.program_id(0) == 0)
    def _():
        padx[...] = jnp.zeros(padx.shape, F32)
        padbc[...] = jnp.zeros(padbc.shape, F32)
        padd[...] = jnp.zeros(padd.shape, F32)
        pada[...] = jnp.zeros(pada.shape, F32)

    def conv(s0, s1, s2, u, w_ref, b_ref):
        acc = s0[...] * w_ref[0:1, :] + s1[...] * w_ref[1:2, :] + s2[...] * w_ref[2:3, :] + u[...] * w_ref[3:4, :]
        return _silu(acc + b_ref[...])

    padx[0:bt, :] = conv(s0x, s1x, s2x, x_ref, cwx_ref, cbx_ref)
    padbc[0:bt, :] = conv(s0bc, s1bc, s2bc, bc_ref, cwbc_ref, cbbc_ref)
    dt = _softplus(dtr_ref[...] + dtb_ref[...])
    padd[0:bt, :] = dt
    pada[0:bt, :] = jnp.exp(dt * (-jnp.exp(alog_ref[...])))
    dt_t = padd[...].T
    da_t = pada[...].T
    wide = bt * gs
    row_w = lax.broadcasted_iota(jnp.int32, (LANES, wide), 0)
    blk_w = lax.broadcasted_iota(jnp.int32, (LANES, wide), 1) // gs
    row_t = lax.broadcasted_iota(jnp.int32, (wide, LANES), 0) // gs
    lane_t = lax.broadcasted_iota(jnp.int32, (wide, LANES), 1)

    bbig = cbig_t = None
    for ci in range(n_chunk):
        g = (2 * ci) // hpg
        if (2 * ci) % hpg == 0:
            bg = padbc[:, g * gs:(g + 1) * gs]
            bbig = jnp.where(row_w == blk_w, jnp.concatenate([bg] * bt, axis=1), 0.0).astype(BF16)
            cg_t = padbc[:, n_bc + g * gs:n_bc + (g + 1) * gs].T
            cbig_t = jnp.where(row_t == lane_t, jnp.concatenate([cg_t] * bt, axis=0), 0.0).astype(BF16)
        csl = slice(ci * LANES, (ci + 1) * LANES)
        x_t = padx[:, csl].T
        dt_rows = jnp.concatenate([jnp.broadcast_to(dt_t[2 * ci:2 * ci + 1, :], (hd, LANES)),
                                   jnp.broadcast_to(dt_t[2 * ci + 1:2 * ci + 2, :], (hd, LANES))], axis=0)
        inc = jnp.dot((x_t * dt_rows).astype(BF16), bbig, preferred_element_type=F32)
        hs = []
        for k in range(bt):
            da_rows = jnp.concatenate(
                [jnp.broadcast_to(da_t[2 * ci:2 * ci + 1, k:k + 1], (hd, gs)),
                 jnp.broadcast_to(da_t[2 * ci + 1:2 * ci + 2, k:k + 1], (hd, gs))], axis=0)
            hn = h0_ref[k, csl, :] * da_rows + inc[:, k * gs:(k + 1) * gs]
            ht_ref[k, csl, :] = hn
            hs.append(hn.astype(BF16))
        y_t = jnp.dot(jnp.concatenate(hs, axis=1), cbig_t, preferred_element_type=F32)
        y_s[:, csl] = y_t.T[0:bt, :]

    for g in range(SSM_GROUPS):
        gsl = slice(g * gw, (g + 1) * gw)
        y_ref[:, gsl] = _gated_group_norm(y_s[:, gsl], padx[0:bt, gsl], z_ref[:, gsl], dexp_ref[:, gsl],
                                          ng_ref[:, gsl])


def _ssd_sample(p2, dt_block, conv_state2d, h0_all, layer, ht_all, cw, cb, dtb, alog, dexp, ng, d):
    db = p2.shape[0]
    bt = DEC_TILE
    depth, _, hp, _ = h0_all.shape
    first = ht_all is None
    row = lambda off: pl.BlockSpec((bt, d), lambda i: (i, off))
    const = lambda shape: pl.BlockSpec(shape, lambda i: (0, 0))
    in_specs = [row(7), row(8), row(6), pl.BlockSpec((bt, LANES), lambda i: (i, dt_block)),
                row(0), row(1), row(2), row(3), row(4), row(5),
                pl.BlockSpec((None, bt, hp, SSM_STATE), lambda i: (layer, i, 0, 0)),
                const((SSM_CONV, d)), const((SSM_CONV, d)), const((1, d)), const((1, d)),
                const((1, LANES)), const((1, LANES)), const((1, d)), const((1, d))]
    args = [p2, p2, p2, p2, conv_state2d, conv_state2d, conv_state2d, conv_state2d, conv_state2d,
            conv_state2d, h0_all, cw[:, :d], cw[:, d:], cb[:, :d], cb[:, d:], dtb, alog, dexp, ng]
    if first:
        state_out = pl.BlockSpec((depth, bt, hp, SSM_STATE), lambda i: (0, i, 0, 0))
        aliases = {}
    else:
        state_out = pl.BlockSpec((None, bt, hp, SSM_STATE), lambda i: (layer, i, 0, 0))
        in_specs.append(pl.BlockSpec(memory_space=pl.ANY))
        args.append(ht_all)
        aliases = {len(args) - 1: 1}
    y, ht_all = pl.pallas_call(
        functools.partial(_ssd_sample_kernel, layer=layer, owns_all_layers=first),
        grid=(db // bt,),
        in_specs=in_specs,
        out_specs=[pl.BlockSpec((bt, d), lambda i: (i, 0)), state_out],
        out_shape=[jax.ShapeDtypeStruct((db, d), F32),
                   jax.ShapeDtypeStruct((depth, db, hp, SSM_STATE), F32)],
        scratch_shapes=[pltpu.VMEM((LANES, d), F32), pltpu.VMEM((LANES, d), F32),
                        pltpu.VMEM((LANES, LANES), F32), pltpu.VMEM((LANES, LANES), F32),
                        pltpu.VMEM((bt, d), F32)],
        input_output_aliases=aliases,
        compiler_params=_cparams("arbitrary"),
        name="ssd_sample",
    )(*args)
    return y, ht_all


def _t5_bucket_np(dist):
    d = np.asarray(dist, np.int32)
    max_exact = N_BUCKETS // 2
    df = np.maximum(d, 1).astype(np.float32)
    large = max_exact + (np.log(df / np.float32(max_exact)) / np.float32(math.log(MAX_DISTANCE / max_exact))
                         * np.float32(N_BUCKETS - max_exact)).astype(np.int32)
    large = np.minimum(large, N_BUCKETS - 1)
    return np.where(d < max_exact, d, large).astype(np.int32)


def _bias_index_tables():
    blk = ATT_BLOCK
    qi = np.arange(blk)[:, None]
    kj = np.arange(2 * blk)[None, :]
    sdist = qi + blk - kj
    idx_p, idx_d = [], []
    for window, dil in ATT_GROUPS:
        nk = window // dil
        band = (sdist >= 0) & (sdist <= nk)
        idx_p.append(np.where(band, _t5_bucket_np(np.clip(sdist, 0, nk) * dil), -1))
        steps = nk - np.arange(nk)
        idx_d.append(np.broadcast_to(_t5_bucket_np(steps * dil)[:, None], (nk, LANES)))
    return np.stack(idx_p).astype(np.int32), np.stack(idx_d).astype(np.int32)


def _bias_kernel(tab_ref, idxp_ref, idxd_ref, bp_ref, bd_ref):
    h = pl.program_id(0)
    ip = idxp_ref[...]
    idd = idxd_ref[...]
    accp = jnp.full(ip.shape, NEG_INF, F32)
    accd = jnp.full(idd.shape, NEG_INF, F32)
    for b in range(N_BUCKETS):
        v = tab_ref[b, h]
        accp = jnp.where(ip == b, v, accp)
        accd = jnp.where(idd == b, v, accd)
    bp_ref[...] = accp
    bd_ref[...] = accd


def _build_bias(rel_bias):
    idx_p, idx_d = _bias_index_tables()
    n_heads = rel_bias.shape[1]
    blk = ATT_BLOCK
    nk = idx_d.shape[1]
    return pl.pallas_call(
        _bias_kernel,
        grid=(n_heads,),
        in_specs=[pl.BlockSpec(memory_space=pltpu.SMEM),
                  pl.BlockSpec((None, blk, 2 * blk), lambda h: (h // ATT_HPG, 0, 0)),
                  pl.BlockSpec((None, nk, LANES), lambda h: (h // ATT_HPG, 0, 0))],
        out_specs=[pl.BlockSpec((None, blk, 2 * blk), lambda h: (h, 0, 0)),
                   pl.BlockSpec((None, nk, LANES), lambda h: (h, 0, 0))],
        out_shape=[jax.ShapeDtypeStruct((n_heads, blk, 2 * blk), F32),
                   jax.ShapeDtypeStruct((n_heads, nk, LANES), F32)],
        compiler_params=_cparams("arbitrary"),
        name="t5_bias",
    )(rel_bias, jnp.asarray(idx_p), jnp.asarray(idx_d))


ATT_TILE = ATT_BLOCK * max(dil for _, dil in ATT_GROUPS)
MERGE_ROWS = 256
ATT_UNITS = 4


def _merge_three(parts):
    m_all = jnp.maximum(jnp.maximum(parts[0][1], parts[1][1]), parts[2][1])
    w = jnp.exp(parts[0][1] - m_all)
    num = w * parts[0][0]
    den = w * parts[0][2]
    for o, m, l in parts[1:]:
        w = jnp.exp(m - m_all)
        num = num + w * o
        den = den + w * l
    return num / den


def _attn_prompt_kernel(*refs, n_cast):
    ng = len(ATT_GROUPS)
    blk = ATT_BLOCK
    hd = ATT_HEAD_DIM
    bias_ref = refs[5 * ng]
    w_srcs = refs[5 * ng + 1:5 * ng + 1 + n_cast]
    y_ref = refs[5 * ng + 1 + n_cast]
    w_dsts = refs[5 * ng + 2 + n_cast:5 * ng + 2 + 2 * n_cast]
    scr = refs[5 * ng + 2 + 2 * n_cast:]
    for src, dst in zip(w_srcs, w_dsts):
        dst[...] = src[...].astype(BF16)
    first_tile = pl.program_id(1) == 0
    ones = jnp.ones((2 * blk, hd), BF16)

    def rows_of(start, n, dil):
        return pl.ds(start, n) if dil == 1 else pl.ds(start, n, stride=dil)

    def unit(gi, dil, start, k2, v2, bias2):
        q_ref = refs[5 * gi]
        o_s, m_s, l_s = scr[3 * gi:3 * gi + 3]
        rows = rows_of(start, blk, dil)
        s = _bdot_nt(q_ref[rows, :] * ATT_SCALE, k2) + bias2
        m = jnp.max(s, axis=-1, keepdims=True)
        p = jnp.exp(s - m)
        ol = _bdot(p, jnp.concatenate([v2.astype(BF16), ones], axis=1))
        o_s[rows, :] = ol[:, :hd]
        m_s[rows, :] = jnp.broadcast_to(m, (blk, hd))
        l_s[rows, :] = ol[:, hd:]

    for gi, (window, dil) in enumerate(ATT_GROUPS):
        _, kc_ref, vc_ref, kp_ref, vp_ref = refs[5 * gi:5 * gi + 5]
        span = blk * dil
        n_sub = ATT_TILE // span
        bias = bias_ref[gi]
        bias_head = jnp.concatenate([jnp.where(first_tile, NEG_INF, bias[:, :blk]), bias[:, blk:]], axis=1)

        def head_unit(r, gi=gi, dil=dil, kc_ref=kc_ref, vc_ref=vc_ref, kp_ref=kp_ref, vp_ref=vp_ref,
                      bias_head=bias_head):
            prow, crow = rows_of(r, blk, dil), rows_of(r, blk, dil)
            k2 = jnp.concatenate([kp_ref[prow, :], kc_ref[crow, :]], axis=0)
            v2 = jnp.concatenate([vp_ref[prow, :], vc_ref[crow, :]], axis=0)
            unit(gi, dil, r, k2, v2, bias_head)

        def inner_unit(j, r, gi=gi, dil=dil, span=span, kc_ref=kc_ref, vc_ref=vc_ref, bias=bias):
            both = rows_of((j - 1) * span + r, 2 * blk, dil)
            unit(gi, dil, j * span + r, kc_ref[both, :], vc_ref[both, :], bias)

        if dil >= ATT_UNITS:
            def head_body(it, carry, head_unit=head_unit):
                for k in range(ATT_UNITS):
                    head_unit(it * ATT_UNITS + k)
                return carry
            lax.fori_loop(0, dil // ATT_UNITS, head_body, 0)
        else:
            for r in range(dil):
                head_unit(r)
        if n_sub > 1:
            per_it = max(ATT_UNITS // dil, 1)
            n_it = (n_sub - 1) // per_it

            def inner_body(it, carry, inner_unit=inner_unit, per_it=per_it, dil=dil):
                for k in range(per_it):
                    for r in range(dil):
                        inner_unit(1 + it * per_it + k, r)
                return carry
            lax.fori_loop(0, n_it, inner_body, 0)
            for j in range(1 + n_it * per_it, n_sub):
                for r in range(dil):
                    inner_unit(j, r)

    for c in range(ATT_TILE // MERGE_ROWS):
        rs = slice(c * MERGE_ROWS, (c + 1) * MERGE_ROWS)
        parts = [(scr[3 * gi][rs, :], scr[3 * gi + 1][rs, :], scr[3 * gi + 2][rs, :]) for gi in range(ng)]
        y_ref[rs, :] = _merge_three(parts).astype(y_ref.dtype)


def _attn_prompt(qkv, bias_p, bsz, s, layer, cast_ws):
    hdim = ATT_HEAD_DIM
    ng = len(ATT_GROUPS)
    n_heads = qkv.shape[1] // (3 * hdim)
    tile = ATT_TILE
    assert s % tile == 0
    nt = s // tile
    in_specs, args = [], []
    for gi, (window, dil) in enumerate(ATT_GROUPS):
        span = ATT_BLOCK * dil
        per = tile // span

        def cur(part, gi=gi):
            return pl.BlockSpec((tile, hdim), lambda b, i, h: (b * nt + i, part * n_heads + gi * ATT_HPG + h))

        def prev(part, gi=gi, per=per):
            return pl.BlockSpec((span, hdim),
                                lambda b, i, h: (jnp.maximum((b * nt + i) * per - 1, 0),
                                                 part * n_heads + gi * ATT_HPG + h))

        in_specs += [cur(0), cur(1), cur(2), prev(1), prev(2)]
        args += [qkv] * 5
    in_specs.append(pl.BlockSpec((ng, None, ATT_BLOCK, 2 * ATT_BLOCK), lambda b, i, h: (0, h, 0, 0)))
    args.append(bias_p)
    steps = bsz * nt * ATT_HPG
    out_specs = [pl.BlockSpec((tile, hdim), lambda b, i, h: (b * nt + i, h))]
    out_shape = [jax.ShapeDtypeStruct((bsz * s, ATT_HPG * hdim), BF16)]
    for w_all, axis in cast_ws:
        _, r, c = w_all.shape
        slab = w_all.shape[axis] // steps
        assert slab * steps == w_all.shape[axis] and slab % (LANES if axis == 2 else 2 * SUBLANES) == 0
        shape = (None, slab, c) if axis == 1 else (None, r, slab)

        def at(first, axis=axis):
            step = lambda b, i, h: (b * nt + i) * ATT_HPG + h
            if axis == 1:
                return lambda b, i, h: (first, step(b, i, h), 0)
            return lambda b, i, h: (first, 0, step(b, i, h))

        in_specs.append(pl.BlockSpec(shape, at(layer)))
        args.append(w_all)
        out_specs.append(pl.BlockSpec(shape, at(0)))
        out_shape.append(jax.ShapeDtypeStruct((1, r, c), BF16))
    outs = pl.pallas_call(
        functools.partial(_attn_prompt_kernel, n_cast=len(cast_ws)),
        grid=(bsz, nt, ATT_HPG),
        in_specs=in_specs,
        out_specs=out_specs,
        out_shape=out_shape,
        scratch_shapes=[pltpu.VMEM((tile, hdim), F32)] * (3 * ng),
        compiler_params=_cparams("arbitrary", "arbitrary", "arbitrary"),
        name="attn_prompt",
    )(*args)
    return outs[0], outs[1:]


REDUCE_WAYS = 4
DEC_KEYS = 32


def _reduce_rows(x, op):
    n = x.shape[0]
    part = op(x.reshape((REDUCE_WAYS, n // REDUCE_WAYS) + x.shape[1:]), axis=1)
    return op(part, axis=0)


def _attn_sample_kernel(qkv_ref, c0_ref, c1_ref, c2_ref, bias_ref, bias0_ref, y_ref):
    ng = len(ATT_GROUPS)
    hp = ATT_HPG
    caches = (c0_ref, c1_ref, c2_ref)
    zeros = jnp.zeros((hp, ATT_HEAD_DIM), F32)
    ones = jnp.ones((hp, ATT_HEAD_DIM), F32)

    def body(b, carry):
        parts = []
        for gi in range(ng):
            q = qkv_ref[b, gi]
            k_new = qkv_ref[b, ng + gi]
            v_new = qkv_ref[b, 2 * ng + gi]
            q2 = jnp.concatenate([q * ATT_SCALE, zeros], axis=0)
            s_self = jnp.sum(k_new * q, axis=-1, keepdims=True) * ATT_SCALE + bias0_ref[gi]
            m = jnp.concatenate([s_self, zeros], axis=0)
            l = jnp.concatenate([ones, ones], axis=0)
            o = jnp.concatenate([zeros, v_new], axis=0)
            for c in range(caches[gi].shape[1] // DEC_KEYS):
                ks = slice(c * DEC_KEYS, (c + 1) * DEC_KEYS)
                kv = caches[gi][b, ks]
                s = jnp.sum(kv * q2[None], axis=-1, keepdims=True) + bias_ref[gi, ks]
                m_new = jnp.maximum(m, _reduce_rows(s, jnp.max))
                alpha = jnp.exp(m - m_new)
                p = jnp.exp(s - m_new[None])
                l = alpha * l + _reduce_rows(p, jnp.sum)
                p_on_v = pltpu.roll(p, hp, 1)
                o = pltpu.roll(alpha, hp, 0) * o + _reduce_rows(p_on_v * kv, jnp.sum)
                m = m_new
            parts.append((o[hp:], m[:hp], l[:hp]))
        y_ref[b] = _merge_three(parts)
        return carry

    lax.fori_loop(0, DEC_TILE, body, 0)


def _attn_sample(qkv2, caches, layer, bias_d, bias0):
    db = qkv2.shape[0]
    bt = DEC_TILE
    hdim = ATT_HEAD_DIM
    ng = len(ATT_GROUPS)
    views, specs = [], []
    for (window, dil), cache in zip(ATT_GROUPS, caches):
        nk = window // dil
        depth, _, wb = cache.shape[:3]
        assert wb == window, "cached window must equal the group's window"
        views.append(cache.reshape(depth, db, nk, dil, 2 * ATT_HPG, hdim))
        specs.append(pl.BlockSpec((None, bt, nk, None, 2 * ATT_HPG, hdim),
                                  lambda i: (layer, i, 0, 0, 0, 0)))
    y = pl.pallas_call(
        _attn_sample_kernel,
        grid=(db // bt,),
        in_specs=[pl.BlockSpec((bt, 3 * ng, ATT_HPG, hdim), lambda i: (i, 0, 0, 0))] + specs +
                 [pl.BlockSpec(bias_d.shape, lambda i: (0, 0, 0, 0)),
                  pl.BlockSpec(bias0.shape, lambda i: (0, 0, 0))],
        out_specs=pl.BlockSpec((bt, ATT_HPG, hdim), lambda i: (i, 0, 0)),
        out_shape=jax.ShapeDtypeStruct((db, ATT_HPG, hdim), F32),
        compiler_params=_cparams("parallel"),
        name="attn_sample",
    )(qkv2.reshape(db, 3 * ng, ATT_HPG, hdim), *views, bias_d, bias0)
    return y.reshape(db, ATT_HPG * hdim)


def _branch_kernel(ysc_ref, yssm_ref, yatt_ref, ysc_s, yssm_s, yatt_s, wsc_ref, wssm_ref, watt_ref,
                   g0_ref, g1_ref, g2_ref, g0_s, g1_s, g2_s, wu_ref, wd_ref, o_ref, os_ref, wu_bf_ref, wd_bf_ref):
    wu_bf_ref[...] = wu_ref[...].astype(BF16)
    wd_bf_ref[...] = wd_ref[...].astype(BF16)

    def merged(ysc, yssm, yatt, g0, g1, g2, out):
        tn = out.shape[1]
        cw = min(tn, MXU_COLS)
        for ci in range(tn // cw):
            cs = slice(ci * cw, (ci + 1) * cw)
            a = jnp.dot(ysc[...], wsc_ref[:, cs], preferred_element_type=F32)
            b = jnp.dot(yssm[...], wssm_ref[:, cs], preferred_element_type=F32)
            c = jnp.dot(yatt[...], watt_ref[:, cs], preferred_element_type=F32)
            out[:, cs] = (_sigmoid(g0[:, cs]) * a + _sigmoid(g1[:, cs]) * b
                          + _sigmoid(g2[:, cs]) * c).astype(out.dtype)

    merged(ysc_ref, yssm_ref, yatt_ref, g0_ref, g1_ref, g2_ref, o_ref)

    @pl.when(pl.program_id(1) == 0)
    def _():
        merged(ysc_s, yssm_s, yatt_s, g0_s, g1_s, g2_s, os_ref)


def _branch_merge(ys, ys_s, wsc, wssm, watt, layer, w_up, w_down, mlp_layer, p2, p2_s, tm, tn):
    m, d = ys[0].shape
    ms = ys_s[0].shape[0]
    da = ys[2].shape[1]
    f = w_up.shape[2]
    nj, ni = d // tn, m // tm
    slab = f // (nj * ni)
    assert slab * nj * ni == f and slab % LANES == 0
    rows = lambda width: pl.BlockSpec((tm, width), lambda j, i: (i, 0))
    rows_s = lambda width: pl.BlockSpec((ms, width), lambda j, i: (0, 0))
    wcol = lambda k: pl.BlockSpec((None, k, tn), lambda j, i: (layer, 0, j))
    gate = lambda br: pl.BlockSpec((tm, tn), lambda j, i: (i, br * nj + j))
    gate_s = lambda br: pl.BlockSpec((ms, tn), lambda j, i: (0, br * nj + j))
    return pl.pallas_call(
        _branch_kernel,
        grid=(nj, ni),
        in_specs=[rows(d), rows(d), rows(da), rows_s(d), rows_s(d), rows_s(da), wcol(d), wcol(d), wcol(da),
                  gate(0), gate(1), gate(2), gate_s(0), gate_s(1), gate_s(2),
                  pl.BlockSpec((None, d, slab), lambda j, i: (mlp_layer, 0, j * ni + i)),
                  pl.BlockSpec((None, slab, d), lambda j, i: (mlp_layer, j * ni + i, 0))],
        out_specs=[pl.BlockSpec((tm, tn), lambda j, i: (i, j)), pl.BlockSpec((ms, tn), lambda j, i: (0, j)),
                   pl.BlockSpec((None, d, slab), lambda j, i: (0, 0, j * ni + i)),
                   pl.BlockSpec((None, slab, d), lambda j, i: (0, j * ni + i, 0))],
        out_shape=[jax.ShapeDtypeStruct((m, d), BF16), jax.ShapeDtypeStruct((ms, d), BF16),
                   jax.ShapeDtypeStruct((1, d, f), BF16), jax.ShapeDtypeStruct((1, f, d), BF16)],
        compiler_params=_cparams("arbitrary", "arbitrary"),
        name="branch_merge",
    )(*ys, *ys_s, wsc, wssm, watt, p2, p2, p2, p2_s, p2_s, p2_s, w_up, w_down)


def _outproj_kernel(x_ref, mg_ref, x_s, mg_s, w_ref, g_ref, x1_ref, h2_ref, x1_s, h2_s):
    def project(x, mg, x1_out, h2_out):
        m = x.shape[0]
        rc = min(m, NORM_ROWS)
        for r in range(m // rc):
            rows = slice(r * rc, (r + 1) * rc)
            x1 = x[rows, :] + jnp.dot(mg[rows, :], w_ref[...], preferred_element_type=F32)
            x1_out[rows, :] = x1
            ms = jnp.mean(x1 * x1, axis=-1, keepdims=True)
            h2_out[rows, :] = (x1 * lax.rsqrt(ms + EPS) * g_ref[...]).astype(h2_out.dtype)

    project(x_ref, mg_ref, x1_ref, h2_ref)

    @pl.when(pl.program_id(0) == 0)
    def _():
        project(x_s, mg_s, x1_s, h2_s)


def _outproj(x2d, merged, x2d_s, merged_s, w_out, layer, g2, tm):
    m, d = x2d.shape
    ms = x2d_s.shape[0]
    rows = pl.BlockSpec((tm, d), lambda i: (i, 0))
    rows_s = pl.BlockSpec((ms, d), lambda i: (0, 0))
    return pl.pallas_call(
        _outproj_kernel,
        grid=(m // tm,),
        in_specs=[rows, rows, rows_s, rows_s, pl.BlockSpec((None, d, d), lambda i: (layer, 0, 0)),
                  pl.BlockSpec((1, d), lambda i: (0, 0))],
        out_specs=[rows, rows, rows_s, rows_s],
        out_shape=[jax.ShapeDtypeStruct((m, d), F32), jax.ShapeDtypeStruct((m, d), BF16),
                   jax.ShapeDtypeStruct((ms, d), F32), jax.ShapeDtypeStruct((ms, d), BF16)],
        compiler_params=_cparams("arbitrary"),
        name="outproj_norm",
    )(x2d, merged, x2d_s, merged_s, w_out, g2.reshape(1, d))


def _mlp_kernel(x1_ref, h2_ref, x1_s, h2_s, wu_ref, wd_ref, o_ref, os_ref):
    def accumulate(x1, h2, out):
        @pl.when(pl.program_id(1) == 0)
        def _():
            out[...] = x1[...]

        tf = wu_ref.shape[1]
        cf = min(tf, FF_CHUNK)
        us = []
        for c in range(tf // cf):
            u = jnp.dot(h2[...], wu_ref[:, c * cf:(c + 1) * cf], preferred_element_type=F32)
            us.append(jnp.square(jnp.maximum(u, 0.0)).astype(BF16))
        out[...] += jnp.dot(jnp.concatenate(us, axis=1), wd_ref[...], preferred_element_type=F32)

    accumulate(x1_ref, h2_ref, o_ref)

    @pl.when(pl.program_id(0) == 0)
    def _():
        accumulate(x1_s, h2_s, os_ref)


def _mlp(x1, h2, x1_s, h2_s, w_up, w_down, layer, tm, tf):
    m, d = x1.shape
    ms = x1_s.shape[0]
    f = w_up.shape[2]
    rows = pl.BlockSpec((tm, d), lambda i, j: (i, 0))
    rows_s = pl.BlockSpec((ms, d), lambda i, j: (0, 0))
    return pl.pallas_call(
        _mlp_kernel,
        grid=(m // tm, f // tf),
        in_specs=[rows, rows, rows_s, rows_s, pl.BlockSpec((None, d, tf), lambda i, j: (layer, 0, j)),
                  pl.BlockSpec((None, tf, d), lambda i, j: (layer, j, 0))],
        out_specs=[rows, rows_s],
        out_shape=[jax.ShapeDtypeStruct((m, d), F32), jax.ShapeDtypeStruct((ms, d), F32)],
        compiler_params=_cparams("arbitrary", "arbitrary"),
        name="mlp",
    )(x1, h2, x1_s, h2_s, w_up, w_down)


def _row_tile(m, pref):
    return pref if m % pref == 0 else m


def _pad_lanes(v):
    return jnp.pad(v.astype(F32), (0, LANES - v.shape[0])).reshape(1, LANES)


def _prep_layer(l, shared, norm1_g, sc_conv_w, ssm_conv_w, ssm_conv_b, ssm_dt_bias, ssm_A_log, ssm_D,
                ssm_norm_g, q_norm_g, k_norm_g, norm2_g):
    wt_in = shared["wt_in"]
    d = wt_in.shape[2]
    heads = d // SSM_HEAD_DIM
    aw = ATT_HPG * len(ATT_GROUPS) * ATT_HEAD_DIM
    n_main = N_BRANCH * d + 3 * d + d + (d + 2 * SSM_GROUPS * SSM_STATE)
    assert n_main % LANES == 0 and heads <= LANES
    n_att_heads = aw // ATT_HEAD_DIM
    gains = jnp.concatenate([jnp.tile(q_norm_g[l], n_att_heads), jnp.tile(k_norm_g[l], n_att_heads),
                             jnp.ones((aw,), F32)]).reshape(1, 3 * aw)
    return dict(
        shared, layer=l, n_main=n_main, qkv_row=n_main + heads, n_qkv=3 * aw,
        norm1_g=norm1_g[l], norm2_g=norm2_g[l], qk_gains=gains,
        sc_conv_w=sc_conv_w[l], ssm_conv_w=ssm_conv_w[l], ssm_conv_b=ssm_conv_b[l].reshape(1, -1),
        dt_bias=_pad_lanes(ssm_dt_bias[l]), a_log=_pad_lanes(ssm_A_log[l]),
        d_exp=jnp.repeat(ssm_D[l].astype(F32), SSM_HEAD_DIM).reshape(1, d),
        ssm_norm_g=ssm_norm_g[l].reshape(1, d),
    )


def _in_proj(x2d, x2d_s, w):
    m = x2d.shape[0]
    h = _rmsnorm(x2d, w["norm1_g"], _row_tile(m, 512))
    h_s = _rmsnorm(x2d_s, w["norm1_g"], x2d_s.shape[0])
    p = _matmul_wcast(h, h_s, w["wt_in"], w["layer"], 0, w["n_main"] + MXU_COLS, MXU_COLS, name="in_proj_main")
    qkv = _matmul_qkv(h, h_s, w["wt_in"], w["layer"], w["qkv_row"], w["n_qkv"], w["qk_gains"], MXU_COLS)
    return (p[0], qkv[0]), (p[1], qkv[1])


def _tail(x2d, p, ys, x2d_s, p_s, ys_s, w, w_bf):
    m = x2d.shape[0]
    w_br_sc, w_br_ssm, w_br_att, w_out = w_bf
    merged, merged_s, w_up, w_down = _branch_merge(ys, ys_s, w_br_sc, w_br_ssm, w_br_att, 0,
                                                   w["w_up"], w["w_down"], w["layer"], p, p_s,
                                                   _row_tile(m, 1024), 512)
    x1, h2, x1_s, h2_s = _outproj(x2d, merged, x2d_s, merged_s, w_out, 0, w["norm2_g"], _row_tile(m, 512))
    return _mlp(x1, h2, x1_s, h2_s, w_up, w_down, 0, _row_tile(m, 512), 1024)


def _kv_rows(qkv3, gi, keep):
    aw = qkv3.shape[2] // 3
    gw = ATT_HPG * ATT_HEAD_DIM
    bsz, ln, _ = qkv3.shape
    k = qkv3[:, ln - keep:, aw + gi * gw:aw + (gi + 1) * gw].reshape(bsz, keep, ATT_HPG, ATT_HEAD_DIM)
    v = qkv3[:, ln - keep:, 2 * aw + gi * gw:2 * aw + (gi + 1) * gw].reshape(bsz, keep, ATT_HPG, ATT_HEAD_DIM)
    return jnp.stack([k, v], axis=2)


def _prompt_mixers(proj, w, bias_p, bsz, s, d):
    p, qkv = proj
    p3 = p.reshape(bsz, s, -1)
    qkv3 = qkv.reshape(bsz, s, -1)
    ysc, sc_new = _sc_prompt(p3, w["sc_conv_w"], d, 1024, 1024)
    yssm, ht = _ssd_prompt(p3, w["n_main"] // LANES, w["ssm_conv_w"], w["ssm_conv_b"], w["dt_bias"],
                           w["a_log"], w["d_exp"], w["ssm_norm_g"], d)
    xbc_off = N_BRANCH * d + 3 * d + d
    ssm_conv_new = p3[:, s - (SSM_CONV - 1):, xbc_off:w["n_main"]]
    kv_new = [_kv_rows(qkv3, gi, min(window, s)) for gi, (window, _) in enumerate(ATT_GROUPS)]
    yatt, w_bf = _attn_prompt(qkv, bias_p, bsz, s, w["layer"],
                              [(w["w_br_sc"], 2), (w["w_br_ssm"], 2), (w["w_br_att"], 1), (w["w_out"], 2)])
    heads = d // SSM_HEAD_DIM
    st = (sc_new, ssm_conv_new, ht.reshape(bsz, heads, SSM_HEAD_DIM, SSM_STATE), kv_new[0], kv_new[1], kv_new[2])
    return (ysc.reshape(bsz * s, d), yssm.reshape(bsz * s, d), yatt), st, w_bf


def _sample_mixers(proj, w, bias_d, bias0, sc_buf, ssm_buf, ssm_h0_all, ssm_ht_all, kv_caches, d):
    p, qkv = proj
    db = p.shape[0]
    l = w["layer"]
    ysc, u_new = _sc_sample(p, sc_buf.reshape(db, -1), w["sc_conv_w"], d, 512)
    sc_new = jnp.concatenate([sc_buf[:, 1:], u_new[:, None, :]], axis=1)
    heads = d // SSM_HEAD_DIM
    depth = ssm_h0_all.shape[0]
    yssm, ssm_ht_all = _ssd_sample(p, w["n_main"] // LANES, ssm_buf.reshape(db, -1),
                                   ssm_h0_all.reshape(depth, db, heads * SSM_HEAD_DIM, SSM_STATE), l, ssm_ht_all,
                                   w["ssm_conv_w"], w["ssm_conv_b"], w["dt_bias"], w["a_log"], w["d_exp"],
                                   w["ssm_norm_g"], d)
    xbc_off = N_BRANCH * d + 3 * d + d
    ssm_conv_new = jnp.concatenate([ssm_buf[:, 1:], p[:, None, xbc_off:w["n_main"]]], axis=1)
    yatt = _attn_sample(qkv, kv_caches, l, bias_d, bias0)
    qkv3 = qkv.reshape(db, 1, -1)
    kv_new = [_kv_rows(qkv3, gi, 1) for gi in range(len(ATT_GROUPS))]
    st = (sc_new, ssm_conv_new, kv_new[0], kv_new[1], kv_new[2])
    return (ysc.astype(BF16), yssm.astype(BF16), yatt.astype(BF16)), st, ssm_ht_all


def kernel(x_prompt, x_sample, state_sc_conv, state_ssm_conv, state_ssm, cache_kv_w128, cache_kv_w512,
           cache_kv_w2048, norm1_g, w_in, sc_conv_w, ssm_conv_w, ssm_conv_b, ssm_dt_bias, ssm_A_log, ssm_D,
           ssm_norm_g, q_norm_g, k_norm_g, rel_bias, w_br_sc, w_br_ssm, w_br_att, w_out, norm2_g, w_up, w_down):
    depth = w_in.shape[0]
    ng = len(ATT_GROUPS)
    bias_p, bias_d = _build_bias(rel_bias)
    bias_p = bias_p.reshape((ng, ATT_HPG) + bias_p.shape[1:])
    bias_d = bias_d.reshape((ng, ATT_HPG) + bias_d.shape[1:]).transpose(0, 2, 1, 3)
    bias_d = jnp.concatenate([bias_d, jnp.zeros_like(bias_d)], axis=2)
    bias0 = jnp.broadcast_to(rel_bias[0].reshape(ng, ATT_HPG, 1), (ng, ATT_HPG, LANES))
    bsz, s, d = x_prompt.shape
    db, t, _ = x_sample.shape
    assert t == 1, "decode kernels handle one new token per sample"
    xp, xs = x_prompt.reshape(bsz * s, d), x_sample.reshape(db, d)
    p_new, s_new, s_ssm = [], [], None
    shared = dict(wt_in=jnp.swapaxes(w_in, 1, 2), w_br_sc=w_br_sc, w_br_ssm=w_br_ssm, w_br_att=w_br_att,
                  w_out=w_out, w_up=w_up, w_down=w_down)
    for l in range(depth):
        w = _prep_layer(l, shared, norm1_g, sc_conv_w, ssm_conv_w, ssm_conv_b, ssm_dt_bias, ssm_A_log, ssm_D,
                        ssm_norm_g, q_norm_g, k_norm_g, norm2_g)
        proj, proj_s = _in_proj(xp, xs, w)
        ys, st, w_bf = _prompt_mixers(proj, w, bias_p, bsz, s, d)
        p_new.append(st)
        ys_s, st, s_ssm = _sample_mixers(proj_s, w, bias_d, bias0, state_sc_conv[l], state_ssm_conv[l], state_ssm,
                                         s_ssm, (cache_kv_w128, cache_kv_w512, cache_kv_w2048), d)
        s_new.append(st)
        xp, xs = _tail(xp, proj[0], ys, xs, proj_s[0], ys_s, w, w_bf)
    p_out = [jnp.stack(a) for a in zip(*p_new)]
    s_out = [jnp.stack(a) for a in zip(*s_new)]
    s_out.insert(2, s_ssm.reshape(state_ssm.shape))
    return (xp.reshape(bsz, s, d), xs.reshape(db, 1, d), *p_out, *s_out)
```

```python
import functools
import math

import numpy as np
import jax
import jax.numpy as jnp
from jax import lax
from jax.experimental import pallas as pl
from jax.experimental.pallas import tpu as pltpu

F32 = jnp.float32
BF16 = jnp.bfloat16

N_BRANCH = 3
SC_CONV = 3
SSM_HEAD_DIM = 64
SSM_GROUPS = 8
SSM_STATE = 128
SSM_CONV = 4
SSM_CHUNK = 128
ATT_HEAD_DIM = 128
ATT_GROUPS = ((128, 1), (512, 4), (2048, 16))
ATT_HPG = 4
ATT_BLOCK = 128
ATT_SCALE = 1.0 / math.sqrt(ATT_HEAD_DIM)
N_BUCKETS = 32
MAX_DISTANCE = 2048
EPS = 1e-6

LANES = 128
SUBLANES = 8
MXU_COLS = 256
VMEM_LIMIT_BYTES = 60 * 1024 * 1024

DEC_TILE = 8
NEG_INF = float("-inf")


def _cparams(*sem):
    return pltpu.CompilerParams(dimension_semantics=sem, vmem_limit_bytes=VMEM_LIMIT_BYTES)


def _sigmoid(v):
    return 1.0 / (1.0 + jnp.exp(-v))


def _silu(v):
    return v * _sigmoid(v)


def _bdot(a, b):
    return jnp.dot(a.astype(BF16), b.astype(BF16), preferred_element_type=F32)


def _bdot_nt(a, b):
    return lax.dot_general(a.astype(BF16), b.astype(BF16), (((1,), (1,)), ((), ())),
                           preferred_element_type=F32)


def _rmsnorm_kernel(x_ref, g_ref, o_ref):
    x = x_ref[...]
    ms = jnp.mean(x * x, axis=-1, keepdims=True)
    o_ref[...] = (x * lax.rsqrt(ms + EPS) * g_ref[...]).astype(o_ref.dtype)


def _rmsnorm(x2d, g, tm):
    m, d = x2d.shape
    return pl.pallas_call(
        _rmsnorm_kernel,
        grid=(m // tm,),
        in_specs=[pl.BlockSpec((tm, d), lambda i: (i, 0)),
                  pl.BlockSpec((1, d), lambda i: (0, 0))],
        out_specs=pl.BlockSpec((tm, d), lambda i: (i, 0)),
        out_shape=jax.ShapeDtypeStruct((m, d), BF16),
        compiler_params=_cparams("parallel"),
        name="rmsnorm",
    )(x2d, g.reshape(1, d))


ROW_CHUNK = 1024
NORM_ROWS = 256
FF_CHUNK = 512


def _resident_rows_dot(a_ref, as_ref, w_ref, wb_ref, emit):
    wb_ref[...] = w_ref[...].astype(BF16)
    m = a_ref.shape[0]
    rc = min(m, ROW_CHUNK)
    for r in range(m // rc):
        rows = slice(r * rc, (r + 1) * rc)
        emit(0, rows, _bdot_nt(a_ref[rows, :], wb_ref[...]))
    emit(1, slice(0, as_ref.shape[0]), _bdot_nt(as_ref[...], wb_ref[...]))


def _mm_wcast_kernel(a_ref, as_ref, w_ref, o_ref, os_ref, wb_ref):
    outs = (o_ref, os_ref)

    def emit(which, rows, acc):
        outs[which][rows, :] = acc

    _resident_rows_dot(a_ref, as_ref, w_ref, wb_ref, emit)


def _resident(shape):
    return pl.BlockSpec(shape, lambda j: (0,) * len(shape), pipeline_mode=pl.Buffered(1))


def _matmul_wcast(a, a_s, wt_all, layer, first_block, n_cols, tn, name):
    m, k = a.shape
    ms = a_s.shape[0]
    assert m % min(m, ROW_CHUNK) == 0
    return pl.pallas_call(
        _mm_wcast_kernel,
        grid=(n_cols // tn,),
        in_specs=[_resident((m, k)), _resident((ms, k)),
                  pl.BlockSpec((None, tn, k), lambda j: (layer, first_block + j, 0))],
        out_specs=[pl.BlockSpec((m, tn), lambda j: (0, j)), pl.BlockSpec((ms, tn), lambda j: (0, j))],
        out_shape=[jax.ShapeDtypeStruct((m, n_cols), F32), jax.ShapeDtypeStruct((ms, n_cols), F32)],
        scratch_shapes=[pltpu.VMEM((tn, k), BF16)],
        compiler_params=_cparams("arbitrary"),
        name=name,
    )(a, a_s, wt_all)


def _mm_qkv_kernel(a_ref, as_ref, w_ref, g_ref, o_ref, os_ref, wb_ref, *, n_norm_blocks):
    is_qk = pl.program_id(0) < n_norm_blocks
    outs = (o_ref, os_ref)

    def emit(which, rows, acc):
        for h in range(acc.shape[1] // ATT_HEAD_DIM):
            sl = slice(h * ATT_HEAD_DIM, (h + 1) * ATT_HEAD_DIM)
            t = acc[:, sl]
            ms = jnp.mean(t * t, axis=-1, keepdims=True)
            outs[which][rows, sl] = t * jnp.where(is_qk, lax.rsqrt(ms + EPS), 1.0) * g_ref[:, sl]

    _resident_rows_dot(a_ref, as_ref, w_ref.at[0], wb_ref, emit)


def _matmul_qkv(a, a_s, wt_all, layer, first_row, n, gains, tn):
    m, k = a.shape
    ms = a_s.shape[0]
    assert first_row % SUBLANES == 0 and tn % ATT_HEAD_DIM == 0
    att_width = n // 3
    return pl.pallas_call(
        functools.partial(_mm_qkv_kernel, n_norm_blocks=2 * att_width // tn),
        grid=(n // tn,),
        in_specs=[_resident((m, k)), _resident((ms, k)),
                  pl.BlockSpec((pl.Element(1), pl.Element(tn), pl.Element(k)),
                               lambda j: (layer, pl.multiple_of(first_row + j * tn, SUBLANES), 0)),
                  pl.BlockSpec((1, tn), lambda j: (0, j))],
        out_specs=[pl.BlockSpec((m, tn), lambda j: (0, j)), pl.BlockSpec((ms, tn), lambda j: (0, j))],
        out_shape=[jax.ShapeDtypeStruct((m, n), F32), jax.ShapeDtypeStruct((ms, n), F32)],
        scratch_shapes=[pltpu.VMEM((tn, k), BF16)],
        compiler_params=_cparams("arbitrary"),
        name="matmul_qkv",
    )(a, a_s, wt_all, gains)


def _sc_prompt_kernel(b_ref, c_ref, x_ref, hc_ref, hx_ref, w_ref, y_ref, tail_ref, ext_ref, *, ts):
    t = pl.program_id(2)
    u = c_ref[...] * x_ref[...]
    hu = hc_ref[...] * hx_ref[...]
    ext_ref[0:SUBLANES, :] = jnp.where(t > 0, hu, 0.0)
    ext_ref[SUBLANES:SUBLANES + ts, :] = u
    u2 = ext_ref[pl.ds(SUBLANES - 2, ts), :]
    u1 = ext_ref[pl.ds(SUBLANES - 1, ts), :]
    y = u2 * w_ref[0:1, :] + u1 * w_ref[1:2, :] + u * w_ref[2:3, :]
    y_ref[...] = (b_ref[...] * y).astype(y_ref.dtype)
    tail_ref[...] = ext_ref[pl.ds(ts, SUBLANES), :]


def _sc_prompt(p3, conv_w, d, ts, tc):
    bsz, s, _ = p3.shape
    ncb = d // tc
    off_b, off_c, off_x = 3 * ncb, 4 * ncb, 5 * ncb
    hb = ts // SUBLANES

    def halo(off):
        return pl.BlockSpec((None, SUBLANES, tc),
                            lambda b, c, t: (b, jnp.maximum(t * hb - 1, 0), off + c))

    def main(off):
        return pl.BlockSpec((None, ts, tc), lambda b, c, t: (b, t, off + c))

    y, tail = pl.pallas_call(
        functools.partial(_sc_prompt_kernel, ts=ts),
        grid=(bsz, ncb, s // ts),
        in_specs=[main(off_b), main(off_c), main(off_x), halo(off_c), halo(off_x),
                  pl.BlockSpec((SC_CONV, tc), lambda b, c, t: (0, c))],
        out_specs=[pl.BlockSpec((None, ts, tc), lambda b, c, t: (b, t, c)),
                   pl.BlockSpec((None, SUBLANES, tc), lambda b, c, t: (b, 0, c))],
        out_shape=[jax.ShapeDtypeStruct((bsz, s, d), BF16),
                   jax.ShapeDtypeStruct((bsz, SUBLANES, d), F32)],
        scratch_shapes=[pltpu.VMEM((ts + SUBLANES, tc), F32)],
        compiler_params=_cparams("parallel", "parallel", "arbitrary"),
        name="short_conv_prompt",
    )(p3, p3, p3, p3, p3, conv_w)
    return y, tail[:, SUBLANES - (SC_CONV - 1):, :]


def _sc_sample_kernel(b_ref, c_ref, x_ref, s0_ref, s1_ref, w_ref, y_ref, u_ref):
    u = c_ref[...] * x_ref[...]
    y = s0_ref[...] * w_ref[0:1, :] + s1_ref[...] * w_ref[1:2, :] + u * w_ref[2:3, :]
    y_ref[...] = b_ref[...] * y
    u_ref[...] = u


def _sc_sample(p2, state2d, conv_w, d, tc):
    db = p2.shape[0]
    ncb = d // tc

    def col(off):
        return pl.BlockSpec((db, tc), lambda c: (0, off + c))

    return pl.pallas_call(
        _sc_sample_kernel,
        grid=(ncb,),
        in_specs=[col(3 * ncb), col(4 * ncb), col(5 * ncb), col(0), col(ncb),
                  pl.BlockSpec((SC_CONV, tc), lambda c: (0, c))],
        out_specs=[col(0), col(0)],
        out_shape=[jax.ShapeDtypeStruct((db, d), F32), jax.ShapeDtypeStruct((db, d), F32)],
        compiler_params=_cparams("parallel"),
        name="short_conv_sample",
    )(p2, p2, p2, state2d, state2d, conv_w)


def _softplus(v):
    return jnp.maximum(v, 0.0) + jnp.log1p(jnp.exp(-jnp.abs(v)))


def _gated_group_norm(y, xs, z, dexp, ng):
    val = (y + dexp * xs) * _silu(z)
    ms = jnp.mean(val * val, axis=-1, keepdims=True)
    return val * lax.rsqrt(ms + EPS) * ng


def _ssd_prompt_kernel(x_ref, bc_ref, z_ref, dtr_ref, cwx_ref, cwbc_ref, cbx_ref, cbbc_ref, dtb_ref,
                       alog_ref, dexp_ref, ng_ref, wu_ref, wd_ref, y_ref, ht_ref, wu_bf_ref, wd_bf_ref,
                       extx, extbc, xs_s, bc_s, state):
    wu_bf_ref[...] = wu_ref[...].astype(BF16)
    wd_bf_ref[...] = wd_ref[...].astype(BF16)
    q = SSM_CHUNK
    c = pl.program_id(1)
    pad = SUBLANES
    taps = SSM_CONV

    @pl.when(c == 0)
    def _():
        extx[0:pad, :] = jnp.zeros((pad, extx.shape[1]), F32)
        extbc[0:pad, :] = jnp.zeros((pad, extbc.shape[1]), F32)
        state[...] = jnp.zeros(state.shape, F32)

    extx[pad:pad + q, :] = x_ref[...]
    extbc[pad:pad + q, :] = bc_ref[...]

    def conv(ext, w_ref, b_ref):
        full = ext[...]
        acc = pltpu.roll(full, taps - 1, 0)[pad:, :] * w_ref[0:1, :]
        for i in range(1, taps - 1):
            acc = acc + pltpu.roll(full, taps - 1 - i, 0)[pad:, :] * w_ref[i:i + 1, :]
        acc = acc + full[pad:, :] * w_ref[taps - 1:taps, :]
        return _silu(acc + b_ref[...])

    xs_s[...] = conv(extx, cwx_ref, cbx_ref)
    bc_s[...] = conv(extbc, cwbc_ref, cbbc_ref)
    extx[0:pad, :] = extx[q:q + pad, :]
    extbc[0:pad, :] = extbc[q:q + pad, :]

    dt = _softplus(dtr_ref[...] + dtb_ref[...])
    a_neg = -jnp.exp(alog_ref[...])
    ii = lax.broadcasted_iota(jnp.int32, (q, q), 0)
    jj = lax.broadcasted_iota(jnp.int32, (q, q), 1)
    causal = ii >= jj
    acum = jnp.dot(causal.astype(F32), dt * a_neg, precision=lax.Precision.HIGHEST,
                   preferred_element_type=F32)
    acum_t = acum.T
    dt_t = dt.T
    lo_half = jj < SSM_HEAD_DIM
    gs = SSM_STATE
    hd = SSM_HEAD_DIM
    hpg = xs_s.shape[1] // (SSM_GROUPS * hd)
    gw = hpg * hd
    n_bc = SSM_GROUPS * gs

    for g in range(SSM_GROUPS):
        bg = bc_s[:, g * gs:(g + 1) * gs].astype(BF16)
        cg = bc_s[:, n_bc + g * gs:n_bc + (g + 1) * gs].astype(BF16)
        cb = _bdot_nt(cg, bg)
        ys = []
        for hp in range(hpg // 2):
            e0 = g * hpg + 2 * hp
            rows = slice(e0 * hd, (e0 + 2) * hd)
            xp = xs_s[:, rows]
            xp16 = xp.astype(BF16)
            cols_a, cols_dt, parts = [], [], []
            for k in range(2):
                e = e0 + k
                col_a = jnp.broadcast_to(acum[:, e:e + 1], (q, q))
                cols_a.append(col_a)
                cols_dt.append(jnp.broadcast_to(dt[:, e:e + 1], (q, q)))
                dec = jnp.exp(jnp.where(causal, col_a - acum_t[e:e + 1, :], NEG_INF))
                wm = cb * dec * dt_t[e:e + 1, :]
                parts.append(jnp.dot(wm.astype(BF16), xp16, preferred_element_type=F32))
            y_diag = jnp.where(lo_half, parts[0], parts[1])
            sel_a = jnp.where(lo_half, cols_a[0], cols_a[1])
            sel_dt = jnp.where(lo_half, cols_dt[0], cols_dt[1])
            hpair = state[rows, :]
            y_off = _bdot_nt(cg, hpair) * jnp.exp(sel_a)
            ys.append(y_diag + y_off)
            last_a = sel_a[q - 1:q, :]
            xw = xp * jnp.exp(last_a - sel_a) * sel_dt
            st = jnp.dot(xw.T.astype(BF16), bg, preferred_element_type=F32)
            cd = jnp.concatenate([jnp.broadcast_to(cols_a[0][q - 1:q, :], (hd, gs)),
                                  jnp.broadcast_to(cols_a[1][q - 1:q, :], (hd, gs))], axis=0)
            state[rows, :] = hpair * jnp.exp(cd) + st
        gsl = slice(g * gw, (g + 1) * gw)
        yg = jnp.concatenate(ys, axis=1)
        y_ref[:, gsl] = _gated_group_norm(yg, xs_s[:, gsl], z_ref[:, gsl], dexp_ref[:, gsl],
                                          ng_ref[:, gsl]).astype(y_ref.dtype)

    @pl.when(c == pl.num_programs(1) - 1)
    def _():
        ht_ref[...] = state[...]


def _ssd_prompt(p3, dt_block, cw, cb, dtb, alog, dexp, ng, d, w_up, w_down, layer):
    bsz, s, _ = p3.shape
    q = SSM_CHUNK
    nc = s // q
    heads = d // SSM_HEAD_DIM
    f = w_up.shape[2]
    slab = f // (bsz * nc)
    assert slab * bsz * nc == f and slab % LANES == 0
    const = lambda shape: pl.BlockSpec(shape, lambda b, c: (0, 0))
    y, ht, wu_bf, wd_bf = pl.pallas_call(
        _ssd_prompt_kernel,
        grid=(bsz, nc),
        in_specs=[pl.BlockSpec((None, q, d), lambda b, c: (b, c, 7)),
                  pl.BlockSpec((None, q, d), lambda b, c: (b, c, 8)),
                  pl.BlockSpec((None, q, d), lambda b, c: (b, c, 6)),
                  pl.BlockSpec((None, q, LANES), lambda b, c: (b, c, dt_block)),
                  const((SSM_CONV, d)), const((SSM_CONV, d)), const((1, d)), const((1, d)),
                  const((1, LANES)), const((1, LANES)), const((1, d)), const((1, d)),
                  pl.BlockSpec((None, d, slab), lambda b, c: (layer, 0, b * nc + c)),
                  pl.BlockSpec((None, slab, d), lambda b, c: (layer, b * nc + c, 0))],
        out_specs=[pl.BlockSpec((None, q, d), lambda b, c: (b, c, 0)),
                   pl.BlockSpec((None, heads * SSM_HEAD_DIM, SSM_STATE), lambda b, c: (b, 0, 0)),
                   pl.BlockSpec((None, d, slab), lambda b, c: (0, 0, b * nc + c)),
                   pl.BlockSpec((None, slab, d), lambda b, c: (0, b * nc + c, 0))],
        out_shape=[jax.ShapeDtypeStruct((bsz, s, d), BF16),
                   jax.ShapeDtypeStruct((bsz, heads * SSM_HEAD_DIM, SSM_STATE), F32),
                   jax.ShapeDtypeStruct((1, d, f), BF16), jax.ShapeDtypeStruct((1, f, d), BF16)],
        scratch_shapes=[pltpu.VMEM((q + SUBLANES, d), F32), pltpu.VMEM((q + SUBLANES, d), F32),
                        pltpu.VMEM((q, d), F32), pltpu.VMEM((q, d), F32),
                        pltpu.VMEM((heads * SSM_HEAD_DIM, SSM_STATE), F32)],
        compiler_params=_cparams("arbitrary", "arbitrary"),
        name="ssd_prompt",
    )(p3, p3, p3, p3, cw[:, :d], cw[:, d:], cb[:, :d], cb[:, d:], dtb, alog, dexp, ng, w_up, w_down)
    return y, ht, (wu_bf, wd_bf)


def _ssd_sample_kernel(*refs, layer, owns_all_layers):
    (x_ref, bc_ref, z_ref, dtr_ref, s0x, s0bc, s1x, s1bc, s2x, s2bc, h0_ref, cwx_ref, cwbc_ref, cbx_ref,
     cbbc_ref, dtb_ref, alog_ref, dexp_ref, ng_ref) = refs[:19]
    y_ref, ht_all_ref, padx, padbc, padd, pada, y_s = refs[-7:]
    if owns_all_layers:
        ht_ref = ht_all_ref.at[layer]
        for other in range(ht_all_ref.shape[0]):
            if other != layer:
                ht_all_ref[other] = jnp.zeros(ht_all_ref.shape[1:], F32)
    else:
        ht_ref = ht_all_ref
    bt = DEC_TILE
    hd = SSM_HEAD_DIM
    gs = SSM_STATE
    d = x_ref.shape[1]
    n_chunk = d // LANES
    hpg = d // (SSM_GROUPS * hd)
    gw = hpg * hd
    n_bc = SSM_GROUPS * gs

    @pl.when(pl.program_id(0) == 0)
    def _():
        padx[...] = jnp.zeros(padx.shape, F32)
        padbc[...] = jnp.zeros(padbc.shape, F32)
        padd[...] = jnp.zeros(padd.shape, F32)
        pada[...] = jnp.zeros(pada.shape, F32)

    def conv(s0, s1, s2, u, w_ref, b_ref):
        acc = s0[...] * w_ref[0:1, :] + s1[...] * w_ref[1:2, :] + s2[...] * w_ref[2:3, :] + u[...] * w_ref[3:4, :]
        return _silu(acc + b_ref[...])

    padx[0:bt, :] = conv(s0x, s1x, s2x, x_ref, cwx_ref, cbx_ref)
    padbc[0:bt, :] = conv(s0bc, s1bc, s2bc, bc_ref, cwbc_ref, cbbc_ref)
    dt = _softplus(dtr_ref[...] + dtb_ref[...])
    padd[0:bt, :] = dt
    pada[0:bt, :] = jnp.exp(dt * (-jnp.exp(alog_ref[...])))
    dt_t = padd[...].T
    da_t = pada[...].T
    wide = bt * gs
    row_w = lax.broadcasted_iota(jnp.int32, (LANES, wide), 0)
    blk_w = lax.broadcasted_iota(jnp.int32, (LANES, wide), 1) // gs
    row_t = lax.broadcasted_iota(jnp.int32, (wide, LANES), 0) // gs
    lane_t = lax.broadcasted_iota(jnp.int32, (wide, LANES), 1)

    bbig = cbig_t = None
    for ci in range(n_chunk):
        g = (2 * ci) // hpg
        if (2 * ci) % hpg == 0:
            bg = padbc[:, g * gs:(g + 1) * gs]
            bbig = jnp.where(row_w == blk_w, jnp.concatenate([bg] * bt, axis=1), 0.0).astype(BF16)
            cg_t = padbc[:, n_bc + g * gs:n_bc + (g + 1) * gs].T
            cbig_t = jnp.where(row_t == lane_t, jnp.concatenate([cg_t] * bt, axis=0), 0.0).astype(BF16)
        csl = slice(ci * LANES, (ci + 1) * LANES)
        x_t = padx[:, csl].T
        dt_rows = jnp.concatenate([jnp.broadcast_to(dt_t[2 * ci:2 * ci + 1, :], (hd, LANES)),
                                   jnp.broadcast_to(dt_t[2 * ci + 1:2 * ci + 2, :], (hd, LANES))], axis=0)
        inc = jnp.dot((x_t * dt_rows).astype(BF16), bbig, preferred_element_type=F32)
        hs = []
        for k in range(bt):
            da_rows = jnp.concatenate(
                [jnp.broadcast_to(da_t[2 * ci:2 * ci + 1, k:k + 1], (hd, gs)),
                 jnp.broadcast_to(da_t[2 * ci + 1:2 * ci + 2, k:k + 1], (hd, gs))], axis=0)
            hn = h0_ref[k, csl, :] * da_rows + inc[:, k * gs:(k + 1) * gs]
            ht_ref[k, csl, :] = hn
            hs.append(hn.astype(BF16))
        y_t = jnp.dot(jnp.concatenate(hs, axis=1), cbig_t, preferred_element_type=F32)
        y_s[:, csl] = y_t.T[0:bt, :]

    for g in range(SSM_GROUPS):
        gsl = slice(g * gw, (g + 1) * gw)
        y_ref[:, gsl] = _gated_group_norm(y_s[:, gsl], padx[0:bt, gsl], z_ref[:, gsl], dexp_ref[:, gsl],
                                          ng_ref[:, gsl])


def _ssd_sample(p2, dt_block, conv_state2d, h0_all, layer, ht_all, cw, cb, dtb, alog, dexp, ng, d):
    db = p2.shape[0]
    bt = DEC_TILE
    depth, _, hp, _ = h0_all.shape
    first = ht_all is None
    row = lambda off: pl.BlockSpec((bt, d), lambda i: (i, off))
    const = lambda shape: pl.BlockSpec(shape, lambda i: (0, 0))
    in_specs = [row(7), row(8), row(6), pl.BlockSpec((bt, LANES), lambda i: (i, dt_block)),
                row(0), row(1), row(2), row(3), row(4), row(5),
                pl.BlockSpec((None, bt, hp, SSM_STATE), lambda i: (layer, i, 0, 0)),
                const((SSM_CONV, d)), const((SSM_CONV, d)), const((1, d)), const((1, d)),
                const((1, LANES)), const((1, LANES)), const((1, d)), const((1, d))]
    args = [p2, p2, p2, p2, conv_state2d, conv_state2d, conv_state2d, conv_state2d, conv_state2d,
            conv_state2d, h0_all, cw[:, :d], cw[:, d:], cb[:, :d], cb[:, d:], dtb, alog, dexp, ng]
    if first:
        state_out = pl.BlockSpec((depth, bt, hp, SSM_STATE), lambda i: (0, i, 0, 0))
        aliases = {}
    else:
        state_out = pl.BlockSpec((None, bt, hp, SSM_STATE), lambda i: (layer, i, 0, 0))
        in_specs.append(pl.BlockSpec(memory_space=pl.ANY))
        args.append(ht_all)
        aliases = {len(args) - 1: 1}
    y, ht_all = pl.pallas_call(
        functools.partial(_ssd_sample_kernel, layer=layer, owns_all_layers=first),
        grid=(db // bt,),
        in_specs=in_specs,
        out_specs=[pl.BlockSpec((bt, d), lambda i: (i, 0)), state_out],
        out_shape=[jax.ShapeDtypeStruct((db, d), F32),
                   jax.ShapeDtypeStruct((depth, db, hp, SSM_STATE), F32)],
        scratch_shapes=[pltpu.VMEM((LANES, d), F32), pltpu.VMEM((LANES, d), F32),
                        pltpu.VMEM((LANES, LANES), F32), pltpu.VMEM((LANES, LANES), F32),
                        pltpu.VMEM((bt, d), F32)],
        input_output_aliases=aliases,
        compiler_params=_cparams("arbitrary"),
        name="ssd_sample",
    )(*args)
    return y, ht_all


def _t5_bucket_np(dist):
    d = np.asarray(dist, np.int32)
    max_exact = N_BUCKETS // 2
    df = np.maximum(d, 1).astype(np.float32)
    large = max_exact + (np.log(df / np.float32(max_exact)) / np.float32(math.log(MAX_DISTANCE / max_exact))
                         * np.float32(N_BUCKETS - max_exact)).astype(np.int32)
    large = np.minimum(large, N_BUCKETS - 1)
    return np.where(d < max_exact, d, large).astype(np.int32)


def _bias_index_tables():
    blk = ATT_BLOCK
    qi = np.arange(blk)[:, None]
    kj = np.arange(2 * blk)[None, :]
    sdist = qi + blk - kj
    idx_p, idx_d = [], []
    for window, dil in ATT_GROUPS:
        nk = window // dil
        band = (sdist >= 0) & (sdist <= nk)
        idx_p.append(np.where(band, _t5_bucket_np(np.clip(sdist, 0, nk) * dil), -1))
        steps = nk - np.arange(nk)
        idx_d.append(np.broadcast_to(_t5_bucket_np(steps * dil)[:, None], (nk, LANES)))
    return np.stack(idx_p).astype(np.int32), np.stack(idx_d).astype(np.int32)


def _bias_kernel(tab_ref, idxp_ref, idxd_ref, bp_ref, bd_ref):
    h = pl.program_id(0)
    ip = idxp_ref[...]
    idd = idxd_ref[...]
    accp = jnp.full(ip.shape, NEG_INF, F32)
    accd = jnp.full(idd.shape, NEG_INF, F32)
    for b in range(N_BUCKETS):
        v = tab_ref[b, h]
        accp = jnp.where(ip == b, v, accp)
        accd = jnp.where(idd == b, v, accd)
    bp_ref[...] = accp
    bd_ref[...] = accd


def _build_bias(rel_bias):
    idx_p, idx_d = _bias_index_tables()
    n_heads = rel_bias.shape[1]
    blk = ATT_BLOCK
    nk = idx_d.shape[1]
    return pl.pallas_call(
        _bias_kernel,
        grid=(n_heads,),
        in_specs=[pl.BlockSpec(memory_space=pltpu.SMEM),
                  pl.BlockSpec((None, blk, 2 * blk), lambda h: (h // ATT_HPG, 0, 0)),
                  pl.BlockSpec((None, nk, LANES), lambda h: (h // ATT_HPG, 0, 0))],
        out_specs=[pl.BlockSpec((None, blk, 2 * blk), lambda h: (h, 0, 0)),
                   pl.BlockSpec((None, nk, LANES), lambda h: (h, 0, 0))],
        out_shape=[jax.ShapeDtypeStruct((n_heads, blk, 2 * blk), F32),
                   jax.ShapeDtypeStruct((n_heads, nk, LANES), F32)],
        compiler_params=_cparams("arbitrary"),
        name="t5_bias",
    )(rel_bias, jnp.asarray(idx_p), jnp.asarray(idx_d))


ATT_TILE = ATT_BLOCK * max(dil for _, dil in ATT_GROUPS)
MERGE_ROWS = 256
ATT_UNITS = 4


def _merge_three(parts):
    m_all = jnp.maximum(jnp.maximum(parts[0][1], parts[1][1]), parts[2][1])
    w = jnp.exp(parts[0][1] - m_all)
    num = w * parts[0][0]
    den = w * parts[0][2]
    for o, m, l in parts[1:]:
        w = jnp.exp(m - m_all)
        num = num + w * o
        den = den + w * l
    return num / den


def _attn_prompt_kernel(*refs, n_cast):
    ng = len(ATT_GROUPS)
    blk = ATT_BLOCK
    hd = ATT_HEAD_DIM
    bias_ref = refs[5 * ng]
    w_srcs = refs[5 * ng + 1:5 * ng + 1 + n_cast]
    y_ref = refs[5 * ng + 1 + n_cast]
    w_dsts = refs[5 * ng + 2 + n_cast:5 * ng + 2 + 2 * n_cast]
    scr = refs[5 * ng + 2 + 2 * n_cast:]
    for src, dst in zip(w_srcs, w_dsts):
        dst[...] = src[...].astype(BF16)
    first_tile = pl.program_id(1) == 0
    ones = jnp.ones((2 * blk, hd), BF16)

    def rows_of(start, n, dil):
        return pl.ds(start, n) if dil == 1 else pl.ds(start, n, stride=dil)

    def unit(gi, dil, start, k2, v2, bias2):
        q_ref = refs[5 * gi]
        o_s, m_s, l_s = scr[3 * gi:3 * gi + 3]
        rows = rows_of(start, blk, dil)
        s = _bdot_nt(q_ref[rows, :] * ATT_SCALE, k2) + bias2
        m = jnp.max(s, axis=-1, keepdims=True)
        p = jnp.exp(s - m)
        ol = _bdot(p, jnp.concatenate([v2.astype(BF16), ones], axis=1))
        o_s[rows, :] = ol[:, :hd]
        m_s[rows, :] = jnp.broadcast_to(m, (blk, hd))
        l_s[rows, :] = ol[:, hd:]

    for gi, (window, dil) in enumerate(ATT_GROUPS):
        _, kc_ref, vc_ref, kp_ref, vp_ref = refs[5 * gi:5 * gi + 5]
        span = blk * dil
        n_sub = ATT_TILE // span
        bias = bias_ref[gi]
        bias_head = jnp.concatenate([jnp.where(first_tile, NEG_INF, bias[:, :blk]), bias[:, blk:]], axis=1)

        def head_unit(r, gi=gi, dil=dil, kc_ref=kc_ref, vc_ref=vc_ref, kp_ref=kp_ref, vp_ref=vp_ref,
                      bias_head=bias_head):
            prow, crow = rows_of(r, blk, dil), rows_of(r, blk, dil)
            k2 = jnp.concatenate([kp_ref[prow, :], kc_ref[crow, :]], axis=0)
            v2 = jnp.concatenate([vp_ref[prow, :], vc_ref[crow, :]], axis=0)
            unit(gi, dil, r, k2, v2, bias_head)

        def inner_unit(j, r, gi=gi, dil=dil, span=span, kc_ref=kc_ref, vc_ref=vc_ref, bias=bias):
            both = rows_of((j - 1) * span + r, 2 * blk, dil)
            unit(gi, dil, j * span + r, kc_ref[both, :], vc_ref[both, :], bias)

        if dil >= ATT_UNITS:
            def head_body(it, carry, head_unit=head_unit):
                for k in range(ATT_UNITS):
                    head_unit(it * ATT_UNITS + k)
                return carry
            lax.fori_loop(0, dil // ATT_UNITS, head_body, 0)
        else:
            for r in range(dil):
                head_unit(r)
        if n_sub > 1:
            per_it = max(ATT_UNITS // dil, 1)
            n_it = (n_sub - 1) // per_it

            def inner_body(it, carry, inner_unit=inner_unit, per_it=per_it, dil=dil):
                for k in range(per_it):
                    for r in range(dil):
                        inner_unit(1 + it * per_it + k, r)
                return carry
            lax.fori_loop(0, n_it, inner_body, 0)
            for j in range(1 + n_it * per_it, n_sub):
                for r in range(dil):
                    inner_unit(j, r)

    for c in range(ATT_TILE // MERGE_ROWS):
        rs = slice(c * MERGE_ROWS, (c + 1) * MERGE_ROWS)
        parts = [(scr[3 * gi][rs, :], scr[3 * gi + 1][rs, :], scr[3 * gi + 2][rs, :]) for gi in range(ng)]
        y_ref[rs, :] = _merge_three(parts).astype(y_ref.dtype)


def _attn_prompt(qkv, bias_p, bsz, s, layer, cast_ws):
    hdim = ATT_HEAD_DIM
    ng = len(ATT_GROUPS)
    n_heads = qkv.shape[1] // (3 * hdim)
    tile = ATT_TILE
    assert s % tile == 0
    nt = s // tile
    in_specs, args = [], []
    for gi, (window, dil) in enumerate(ATT_GROUPS):
        span = ATT_BLOCK * dil
        per = tile // span

        def cur(part, gi=gi):
            return pl.BlockSpec((tile, hdim), lambda b, i, h: (b * nt + i, part * n_heads + gi * ATT_HPG + h))

        def prev(part, gi=gi, per=per):
            return pl.BlockSpec((span, hdim),
                                lambda b, i, h: (jnp.maximum((b * nt + i) * per - 1, 0),
                                                 part * n_heads + gi * ATT_HPG + h))

        in_specs += [cur(0), cur(1), cur(2), prev(1), prev(2)]
        args += [qkv] * 5
    in_specs.append(pl.BlockSpec((ng, None, ATT_BLOCK, 2 * ATT_BLOCK), lambda b, i, h: (0, h, 0, 0)))
    args.append(bias_p)
    steps = bsz * nt * ATT_HPG
    out_specs = [pl.BlockSpec((tile, hdim), lambda b, i, h: (b * nt + i, h))]
    out_shape = [jax.ShapeDtypeStruct((bsz * s, ATT_HPG * hdim), BF16)]
    for w_all, axis in cast_ws:
        _, r, c = w_all.shape
        slab = w_all.shape[axis] // steps
        assert slab * steps == w_all.shape[axis] and slab % (LANES if axis == 2 else 2 * SUBLANES) == 0
        shape = (None, slab, c) if axis == 1 else (None, r, slab)

        def at(first, axis=axis):
            step = lambda b, i, h: (b * nt + i) * ATT_HPG + h
            if axis == 1:
                return lambda b, i, h: (first, step(b, i, h), 0)
            return lambda b, i, h: (first, 0, step(b, i, h))

        in_specs.append(pl.BlockSpec(shape, at(layer)))
        args.append(w_all)
        out_specs.append(pl.BlockSpec(shape, at(0)))
        out_shape.append(jax.ShapeDtypeStruct((1, r, c), BF16))
    outs = pl.pallas_call(
        functools.partial(_attn_prompt_kernel, n_cast=len(cast_ws)),
        grid=(bsz, nt, ATT_HPG),
        in_specs=in_specs,
        out_specs=out_specs,
        out_shape=out_shape,
        scratch_shapes=[pltpu.VMEM((tile, hdim), F32)] * (3 * ng),
        compiler_params=_cparams("arbitrary", "arbitrary", "arbitrary"),
        name="attn_prompt",
    )(*args)
    return outs[0], outs[1:]


REDUCE_WAYS = 4
DEC_KEYS = 32


def _reduce_rows(x, op):
    n = x.shape[0]
    part = op(x.reshape((REDUCE_WAYS, n // REDUCE_WAYS) + x.shape[1:]), axis=1)
    return op(part, axis=0)


def _attn_sample_kernel(qkv_ref, c0_ref, c1_ref, c2_ref, bias_ref, bias0_ref, y_ref):
    ng = len(ATT_GROUPS)
    hp = ATT_HPG
    caches = (c0_ref, c1_ref, c2_ref)
    zeros = jnp.zeros((hp, ATT_HEAD_DIM), F32)
    ones = jnp.ones((hp, ATT_HEAD_DIM), F32)

    def body(b, carry):
        parts = []
        for gi in range(ng):
            q = qkv_ref[b, gi]
            k_new = qkv_ref[b, ng + gi]
            v_new = qkv_ref[b, 2 * ng + gi]
            q2 = jnp.concatenate([q * ATT_SCALE, zeros], axis=0)
            s_self = jnp.sum(k_new * q, axis=-1, keepdims=True) * ATT_SCALE + bias0_ref[gi]
            m = jnp.concatenate([s_self, zeros], axis=0)
            l = jnp.concatenate([ones, ones], axis=0)
            o = jnp.concatenate([zeros, v_new], axis=0)
            for c in range(caches[gi].shape[1] // DEC_KEYS):
                ks = slice(c * DEC_KEYS, (c + 1) * DEC_KEYS)
                kv = caches[gi][b, ks]
                s = jnp.sum(kv * q2[None], axis=-1, keepdims=True) + bias_ref[gi, ks]
                m_new = jnp.maximum(m, _reduce_rows(s, jnp.max))
                alpha = jnp.exp(m - m_new)
                p = jnp.exp(s - m_new[None])
                l = alpha * l + _reduce_rows(p, jnp.sum)
                p_on_v = pltpu.roll(p, hp, 1)
                o = pltpu.roll(alpha, hp, 0) * o + _reduce_rows(p_on_v * kv, jnp.sum)
                m = m_new
            parts.append((o[hp:], m[:hp], l[:hp]))
        y_ref[b] = _merge_three(parts)
        return carry

    lax.fori_loop(0, DEC_TILE, body, 0)


def _attn_sample(qkv2, caches, layer, bias_d, bias0):
    db = qkv2.shape[0]
    bt = DEC_TILE
    hdim = ATT_HEAD_DIM
    ng = len(ATT_GROUPS)
    views, specs = [], []
    for (window, dil), cache in zip(ATT_GROUPS, caches):
        nk = window // dil
        depth, _, wb = cache.shape[:3]
        assert wb == window, "cached window must equal the group's window"
        views.append(cache.reshape(depth, db, nk, dil, 2 * ATT_HPG, hdim))
        specs.append(pl.BlockSpec((None, bt, nk, None, 2 * ATT_HPG, hdim),
                                  lambda i: (layer, i, 0, 0, 0, 0)))
    y = pl.pallas_call(
        _attn_sample_kernel,
        grid=(db // bt,),
        in_specs=[pl.BlockSpec((bt, 3 * ng, ATT_HPG, hdim), lambda i: (i, 0, 0, 0))] + specs +
                 [pl.BlockSpec(bias_d.shape, lambda i: (0, 0, 0, 0)),
                  pl.BlockSpec(bias0.shape, lambda i: (0, 0, 0))],
        out_specs=pl.BlockSpec((bt, ATT_HPG, hdim), lambda i: (i, 0, 0)),
        out_shape=jax.ShapeDtypeStruct((db, ATT_HPG, hdim), F32),
        compiler_params=_cparams("parallel"),
        name="attn_sample",
    )(qkv2.reshape(db, 3 * ng, ATT_HPG, hdim), *views, bias_d, bias0)
    return y.reshape(db, ATT_HPG * hdim)


def _branch_kernel(ysc_ref, yssm_ref, yatt_ref, ysc_s, yssm_s, yatt_s, wsc_ref, wssm_ref, watt_ref,
                   g0_ref, g1_ref, g2_ref, g0_s, g1_s, g2_s, o_ref, os_ref):
    def merged(ysc, yssm, yatt, g0, g1, g2, out):
        tn = out.shape[1]
        cw = min(tn, MXU_COLS)
        for ci in range(tn // cw):
            cs = slice(ci * cw, (ci + 1) * cw)
            a = jnp.dot(ysc[...], wsc_ref[:, cs], preferred_element_type=F32)
            b = jnp.dot(yssm[...], wssm_ref[:, cs], preferred_element_type=F32)
            c = jnp.dot(yatt[...], watt_ref[:, cs], preferred_element_type=F32)
            out[:, cs] = (_sigmoid(g0[:, cs]) * a + _sigmoid(g1[:, cs]) * b
                          + _sigmoid(g2[:, cs]) * c).astype(out.dtype)

    merged(ysc_ref, yssm_ref, yatt_ref, g0_ref, g1_ref, g2_ref, o_ref)

    @pl.when(pl.program_id(1) == 0)
    def _():
        merged(ysc_s, yssm_s, yatt_s, g0_s, g1_s, g2_s, os_ref)


def _branch_merge(ys, ys_s, wsc, wssm, watt, layer, p2, p2_s, tm, tn):
    m, d = ys[0].shape
    ms = ys_s[0].shape[0]
    da = ys[2].shape[1]
    nj, ni = d // tn, m // tm
    rows = lambda width: pl.BlockSpec((tm, width), lambda j, i: (i, 0))
    rows_s = lambda width: pl.BlockSpec((ms, width), lambda j, i: (0, 0))
    wcol = lambda k: pl.BlockSpec((None, k, tn), lambda j, i: (layer, 0, j))
    gate = lambda br: pl.BlockSpec((tm, tn), lambda j, i: (i, br * nj + j))
    gate_s = lambda br: pl.BlockSpec((ms, tn), lambda j, i: (0, br * nj + j))
    return pl.pallas_call(
        _branch_kernel,
        grid=(nj, ni),
        in_specs=[rows(d), rows(d), rows(da), rows_s(d), rows_s(d), rows_s(da), wcol(d), wcol(d), wcol(da),
                  gate(0), gate(1), gate(2), gate_s(0), gate_s(1), gate_s(2)],
        out_specs=[pl.BlockSpec((tm, tn), lambda j, i: (i, j)), pl.BlockSpec((ms, tn), lambda j, i: (0, j))],
        out_shape=[jax.ShapeDtypeStruct((m, d), BF16), jax.ShapeDtypeStruct((ms, d), BF16)],
        compiler_params=_cparams("parallel", "arbitrary"),
        name="branch_merge",
    )(*ys, *ys_s, wsc, wssm, watt, p2, p2, p2, p2_s, p2_s, p2_s)


def _outproj_kernel(x_ref, mg_ref, x_s, mg_s, w_ref, g_ref, x1_ref, h2_ref, x1_s, h2_s):
    def project(x, mg, x1_out, h2_out):
        m = x.shape[0]
        rc = min(m, NORM_ROWS)
        for r in range(m // rc):
            rows = slice(r * rc, (r + 1) * rc)
            x1 = x[rows, :] + jnp.dot(mg[rows, :], w_ref[...], preferred_element_type=F32)
            x1_out[rows, :] = x1
            ms = jnp.mean(x1 * x1, axis=-1, keepdims=True)
            h2_out[rows, :] = (x1 * lax.rsqrt(ms + EPS) * g_ref[...]).astype(h2_out.dtype)

    project(x_ref, mg_ref, x1_ref, h2_ref)

    @pl.when(pl.program_id(0) == 0)
    def _():
        project(x_s, mg_s, x1_s, h2_s)


def _outproj(x2d, merged, x2d_s, merged_s, w_out, layer, g2, tm):
    m, d = x2d.shape
    ms = x2d_s.shape[0]
    rows = pl.BlockSpec((tm, d), lambda i: (i, 0))
    rows_s = pl.BlockSpec((ms, d), lambda i: (0, 0))
    return pl.pallas_call(
        _outproj_kernel,
        grid=(m // tm,),
        in_specs=[rows, rows, rows_s, rows_s, pl.BlockSpec((None, d, d), lambda i: (layer, 0, 0)),
                  pl.BlockSpec((1, d), lambda i: (0, 0))],
        out_specs=[rows, rows, rows_s, rows_s],
        out_shape=[jax.ShapeDtypeStruct((m, d), F32), jax.ShapeDtypeStruct((m, d), BF16),
                   jax.ShapeDtypeStruct((ms, d), F32), jax.ShapeDtypeStruct((ms, d), BF16)],
        compiler_params=_cparams("arbitrary"),
        name="outproj_norm",
    )(x2d, merged, x2d_s, merged_s, w_out, g2.reshape(1, d))


def _mlp_kernel(x1_ref, h2_ref, x1_s, h2_s, wu_ref, wd_ref, o_ref, os_ref):
    def accumulate(x1, h2, out):
        @pl.when(pl.program_id(1) == 0)
        def _():
            out[...] = x1[...]

        tf = wu_ref.shape[1]
        cf = min(tf, FF_CHUNK)
        us = []
        for c in range(tf // cf):
            u = jnp.dot(h2[...], wu_ref[:, c * cf:(c + 1) * cf], preferred_element_type=F32)
            us.append(jnp.square(jnp.maximum(u, 0.0)).astype(BF16))
        out[...] += jnp.dot(jnp.concatenate(us, axis=1), wd_ref[...], preferred_element_type=F32)

    accumulate(x1_ref, h2_ref, o_ref)

    @pl.when(pl.program_id(0) == 0)
    def _():
        accumulate(x1_s, h2_s, os_ref)


def _mlp(x1, h2, x1_s, h2_s, w_up, w_down, layer, tm, tf):
    m, d = x1.shape
    ms = x1_s.shape[0]
    f = w_up.shape[2]
    rows = pl.BlockSpec((tm, d), lambda i, j: (i, 0))
    rows_s = pl.BlockSpec((ms, d), lambda i, j: (0, 0))
    return pl.pallas_call(
        _mlp_kernel,
        grid=(m // tm, f // tf),
        in_specs=[rows, rows, rows_s, rows_s, pl.BlockSpec((None, d, tf), lambda i, j: (layer, 0, j)),
                  pl.BlockSpec((None, tf, d), lambda i, j: (layer, j, 0))],
        out_specs=[rows, rows_s],
        out_shape=[jax.ShapeDtypeStruct((m, d), F32), jax.ShapeDtypeStruct((ms, d), F32)],
        compiler_params=_cparams("arbitrary", "arbitrary"),
        name="mlp",
    )(x1, h2, x1_s, h2_s, w_up, w_down)


def _row_tile(m, pref):
    return pref if m % pref == 0 else m


def _pad_lanes(v):
    return jnp.pad(v.astype(F32), (0, LANES - v.shape[0])).reshape(1, LANES)


def _prep_layer(l, shared, norm1_g, sc_conv_w, ssm_conv_w, ssm_conv_b, ssm_dt_bias, ssm_A_log, ssm_D,
                ssm_norm_g, q_norm_g, k_norm_g, norm2_g):
    wt_in = shared["wt_in"]
    d = wt_in.shape[2]
    heads = d // SSM_HEAD_DIM
    aw = ATT_HPG * len(ATT_GROUPS) * ATT_HEAD_DIM
    n_main = N_BRANCH * d + 3 * d + d + (d + 2 * SSM_GROUPS * SSM_STATE)
    assert n_main % LANES == 0 and heads <= LANES
    n_att_heads = aw // ATT_HEAD_DIM
    gains = jnp.concatenate([jnp.tile(q_norm_g[l], n_att_heads), jnp.tile(k_norm_g[l], n_att_heads),
                             jnp.ones((aw,), F32)]).reshape(1, 3 * aw)
    return dict(
        shared, layer=l, n_main=n_main, qkv_row=n_main + heads, n_qkv=3 * aw,
        norm1_g=norm1_g[l], norm2_g=norm2_g[l], qk_gains=gains,
        sc_conv_w=sc_conv_w[l], ssm_conv_w=ssm_conv_w[l], ssm_conv_b=ssm_conv_b[l].reshape(1, -1),
        dt_bias=_pad_lanes(ssm_dt_bias[l]), a_log=_pad_lanes(ssm_A_log[l]),
        d_exp=jnp.repeat(ssm_D[l].astype(F32), SSM_HEAD_DIM).reshape(1, d),
        ssm_norm_g=ssm_norm_g[l].reshape(1, d),
    )


def _in_proj(x2d, x2d_s, w):
    m = x2d.shape[0]
    h = _rmsnorm(x2d, w["norm1_g"], _row_tile(m, 512))
    h_s = _rmsnorm(x2d_s, w["norm1_g"], x2d_s.shape[0])
    p = _matmul_wcast(h, h_s, w["wt_in"], w["layer"], 0, w["n_main"] + MXU_COLS, MXU_COLS, name="in_proj_main")
    qkv = _matmul_qkv(h, h_s, w["wt_in"], w["layer"], w["qkv_row"], w["n_qkv"], w["qk_gains"], MXU_COLS)
    return (p[0], qkv[0]), (p[1], qkv[1])


def _tail(x2d, p, ys, x2d_s, p_s, ys_s, w, w_bf):
    m = x2d.shape[0]
    w_br_sc, w_br_ssm, w_br_att, w_out, w_up, w_down = w_bf
    merged, merged_s = _branch_merge(ys, ys_s, w_br_sc, w_br_ssm, w_br_att, 0, p, p_s, _row_tile(m, 1024), 512)
    x1, h2, x1_s, h2_s = _outproj(x2d, merged, x2d_s, merged_s, w_out, 0, w["norm2_g"], _row_tile(m, 512))
    return _mlp(x1, h2, x1_s, h2_s, w_up, w_down, 0, _row_tile(m, 512), 1024)


def _kv_rows(qkv3, gi, keep):
    aw = qkv3.shape[2] // 3
    gw = ATT_HPG * ATT_HEAD_DIM
    bsz, ln, _ = qkv3.shape
    k = qkv3[:, ln - keep:, aw + gi * gw:aw + (gi + 1) * gw].reshape(bsz, keep, ATT_HPG, ATT_HEAD_DIM)
    v = qkv3[:, ln - keep:, 2 * aw + gi * gw:2 * aw + (gi + 1) * gw].reshape(bsz, keep, ATT_HPG, ATT_HEAD_DIM)
    return jnp.stack([k, v], axis=2)


def _prompt_mixers(proj, w, bias_p, bsz, s, d):
    p, qkv = proj
    p3 = p.reshape(bsz, s, -1)
    qkv3 = qkv.reshape(bsz, s, -1)
    ysc, sc_new = _sc_prompt(p3, w["sc_conv_w"], d, 1024, 1024)
    yssm, ht, w_mlp = _ssd_prompt(p3, w["n_main"] // LANES, w["ssm_conv_w"], w["ssm_conv_b"], w["dt_bias"],
                                  w["a_log"], w["d_exp"], w["ssm_norm_g"], d, w["w_up"], w["w_down"], w["layer"])
    xbc_off = N_BRANCH * d + 3 * d + d
    ssm_conv_new = p3[:, s - (SSM_CONV - 1):, xbc_off:w["n_main"]]
    kv_new = [_kv_rows(qkv3, gi, min(window, s)) for gi, (window, _) in enumerate(ATT_GROUPS)]
    yatt, w_bf = _attn_prompt(qkv, bias_p, bsz, s, w["layer"],
                              [(w["w_br_sc"], 2), (w["w_br_ssm"], 2), (w["w_br_att"], 1), (w["w_out"], 2)])
    heads = d // SSM_HEAD_DIM
    st = (sc_new, ssm_conv_new, ht.reshape(bsz, heads, SSM_HEAD_DIM, SSM_STATE), kv_new[0], kv_new[1], kv_new[2])
    return (ysc.reshape(bsz * s, d), yssm.reshape(bsz * s, d), yatt), st, tuple(w_bf) + w_mlp


def _sample_mixers(proj, w, bias_d, bias0, sc_buf, ssm_buf, ssm_h0_all, ssm_ht_all, kv_caches, d):
    p, qkv = proj
    db = p.shape[0]
    l = w["layer"]
    ysc, u_new = _sc_sample(p, sc_buf.reshape(db, -1), w["sc_conv_w"], d, 512)
    sc_new = jnp.concatenate([sc_buf[:, 1:], u_new[:, None, :]], axis=1)
    heads = d // SSM_HEAD_DIM
    depth = ssm_h0_all.shape[0]
    yssm, ssm_ht_all = _ssd_sample(p, w["n_main"] // LANES, ssm_buf.reshape(db, -1),
                                   ssm_h0_all.reshape(depth, db, heads * SSM_HEAD_DIM, SSM_STATE), l, ssm_ht_all,
                                   w["ssm_conv_w"], w["ssm_conv_b"], w["dt_bias"], w["a_log"], w["d_exp"],
                                   w["ssm_norm_g"], d)
    xbc_off = N_BRANCH * d + 3 * d + d
    ssm_conv_new = jnp.concatenate([ssm_buf[:, 1:], p[:, None, xbc_off:w["n_main"]]], axis=1)
    yatt = _attn_sample(qkv, kv_caches, l, bias_d, bias0)
    qkv3 = qkv.reshape(db, 1, -1)
    kv_new = [_kv_rows(qkv3, gi, 1) for gi in range(len(ATT_GROUPS))]
    st = (sc_new, ssm_conv_new, kv_new[0], kv_new[1], kv_new[2])
    return (ysc.astype(BF16), yssm.astype(BF16), yatt.astype(BF16)), st, ssm_ht_all


def kernel(x_prompt, x_sample, state_sc_conv, state_ssm_conv, state_ssm, cache_kv_w128, cache_kv_w512,
           cache_kv_w2048, norm1_g, w_in, sc_conv_w, ssm_conv_w, ssm_conv_b, ssm_dt_bias, ssm_A_log, ssm_D,
           ssm_norm_g, q_norm_g, k_norm_g, rel_bias, w_br_sc, w_br_ssm, w_br_att, w_out, norm2_g, w_up, w_down):
    depth = w_in.shape[0]
    ng = len(ATT_GROUPS)
    bias_p, bias_d = _build_bias(rel_bias)
    bias_p = bias_p.reshape((ng, ATT_HPG) + bias_p.shape[1:])
    bias_d = bias_d.reshape((ng, ATT_HPG) + bias_d.shape[1:]).transpose(0, 2, 1, 3)
    bias_d = jnp.concatenate([bias_d, jnp.zeros_like(bias_d)], axis=2)
    bias0 = jnp.broadcast_to(rel_bias[0].reshape(ng, ATT_HPG, 1), (ng, ATT_HPG, LANES))
    bsz, s, d = x_prompt.shape
    db, t, _ = x_sample.shape
    assert t == 1, "decode kernels handle one new token per sample"
    xp, xs = x_prompt.reshape(bsz * s, d), x_sample.reshape(db, d)
    p_new, s_new, s_ssm = [], [], None
    shared = dict(wt_in=jnp.swapaxes(w_in, 1, 2), w_br_sc=w_br_sc, w_br_ssm=w_br_ssm, w_br_att=w_br_att,
                  w_out=w_out, w_up=w_up, w_down=w_down)
    for l in range(depth):
        w = _prep_layer(l, shared, norm1_g, sc_conv_w, ssm_conv_w, ssm_conv_b, ssm_dt_bias, ssm_A_log, ssm_D,
                        ssm_norm_g, q_norm_g, k_norm_g, norm2_g)
        proj, proj_s = _in_proj(xp, xs, w)
        ys, st, w_bf = _prompt_mixers(proj, w, bias_p, bsz, s, d)
        p_new.append(st)
        ys_s, st, s_ssm = _sample_mixers(proj_s, w, bias_d, bias0, state_sc_conv[l], state_ssm_conv[l], state_ssm,
                                         s_ssm, (cache_kv_w128, cache_kv_w512, cache_kv_w2048), d)
        s_new.append(st)
        xp, xs = _tail(xp, proj[0], ys, xs, proj_s[0], ys_s, w, w_bf)
    p_out = [jnp.stack(a) for a in zip(*p_new)]
    s_out = [jnp.stack(a) for a in zip(*s_new)]
    s_out.insert(2, s_ssm.reshape(state_ssm.shape))
    return (xp.reshape(bsz, s, d), xs.reshape(db, 1, d), *p_out, *s_out)
```

```python
import functools
import math

import numpy as np
import jax
import jax.numpy as jnp
from jax import lax
from jax.experimental import pallas as pl
from jax.experimental.pallas import tpu as pltpu

F32 = jnp.float32
BF16 = jnp.bfloat16

N_BRANCH = 3
SC_CONV = 3
SSM_HEAD_DIM = 64
SSM_GROUPS = 8
SSM_STATE = 128
SSM_CONV = 4
SSM_CHUNK = 128
ATT_HEAD_DIM = 128
ATT_GROUPS = ((128, 1), (512, 4), (2048, 16))
ATT_HPG = 4
ATT_BLOCK = 128
ATT_SCALE = 1.0 / math.sqrt(ATT_HEAD_DIM)
N_BUCKETS = 32
MAX_DISTANCE = 2048
EPS = 1e-6

LANES = 128
SUBLANES = 8
MXU_COLS = 256
VMEM_LIMIT_BYTES = 60 * 1024 * 1024

DEC_TILE = 8
NEG_INF = float("-inf")


def _cparams(*sem):
    return pltpu.CompilerParams(dimension_semantics=sem, vmem_limit_bytes=VMEM_LIMIT_BYTES)


def _sigmoid(v):
    return 1.0 / (1.0 + jnp.exp(-v))


def _silu(v):
    return v * _sigmoid(v)


def _bdot(a, b):
    return jnp.dot(a.astype(BF16), b.astype(BF16), preferred_element_type=F32)


def _bdot_nt(a, b):
    return lax.dot_general(a.astype(BF16), b.astype(BF16), (((1,), (1,)), ((), ())),
                           preferred_element_type=F32)


def _rmsnorm_kernel(x_ref, g_ref, o_ref):
    x = x_ref[...]
    ms = jnp.mean(x * x, axis=-1, keepdims=True)
    o_ref[...] = (x * lax.rsqrt(ms + EPS) * g_ref[...]).astype(o_ref.dtype)


def _rmsnorm(x2d, g, tm):
    m, d = x2d.shape
    return pl.pallas_call(
        _rmsnorm_kernel,
        grid=(m // tm,),
        in_specs=[pl.BlockSpec((tm, d), lambda i: (i, 0)),
                  pl.BlockSpec((1, d), lambda i: (0, 0))],
        out_specs=pl.BlockSpec((tm, d), lambda i: (i, 0)),
        out_shape=jax.ShapeDtypeStruct((m, d), BF16),
        compiler_params=_cparams("parallel"),
        name="rmsnorm",
    )(x2d, g.reshape(1, d))


ROW_CHUNK = 1024

NORM_ROWS = 512
BRANCH_TILE = (1024, 512)
MLP_TILE = (512, 1024)
SC_TILE = (1024, 1024)
SC_SAMPLE_COLS = 512


def _resident_rows_dot(a_ref, as_ref, w_ref, wb_ref, emit):
    wb_ref[...] = w_ref[...].astype(BF16)
    m = a_ref.shape[0]
    rc = min(m, ROW_CHUNK)
    for r in range(m // rc):
        rows = slice(r * rc, (r + 1) * rc)
        emit(0, rows, _bdot_nt(a_ref[rows, :], wb_ref[...]))
    emit(1, slice(0, as_ref.shape[0]), _bdot_nt(as_ref[...], wb_ref[...]))


def _mm_wcast_kernel(a_ref, as_ref, w_ref, o_ref, os_ref, wb_ref):
    outs = (o_ref, os_ref)

    def emit(which, rows, acc):
        outs[which][rows, :] = acc

    _resident_rows_dot(a_ref, as_ref, w_ref, wb_ref, emit)


def _resident(shape):
    return pl.BlockSpec(shape, lambda j: (0,) * len(shape), pipeline_mode=pl.Buffered(1))


def _matmul_wcast(a, a_s, wt_all, layer, first_block, n_cols, tn, name):
    m, k = a.shape
    ms = a_s.shape[0]
    assert m % min(m, ROW_CHUNK) == 0
    return pl.pallas_call(
        _mm_wcast_kernel,
        grid=(n_cols // tn,),
        in_specs=[_resident((m, k)), _resident((ms, k)),
                  pl.BlockSpec((None, tn, k), lambda j: (layer, first_block + j, 0))],
        out_specs=[pl.BlockSpec((m, tn), lambda j: (0, j)), pl.BlockSpec((ms, tn), lambda j: (0, j))],
        out_shape=[jax.ShapeDtypeStruct((m, n_cols), F32), jax.ShapeDtypeStruct((ms, n_cols), F32)],
        scratch_shapes=[pltpu.VMEM((tn, k), BF16)],
        compiler_params=_cparams("arbitrary"),
        name=name,
    )(a, a_s, wt_all)


def _mm_qkv_kernel(a_ref, as_ref, w_ref, g_ref, o_ref, os_ref, wb_ref, *, n_norm_blocks):
    is_qk = pl.program_id(0) < n_norm_blocks
    outs = (o_ref, os_ref)

    def emit(which, rows, acc):
        for h in range(acc.shape[1] // ATT_HEAD_DIM):
            sl = slice(h * ATT_HEAD_DIM, (h + 1) * ATT_HEAD_DIM)
            t = acc[:, sl]
            ms = jnp.mean(t * t, axis=-1, keepdims=True)
            outs[which][rows, sl] = t * jnp.where(is_qk, lax.rsqrt(ms + EPS), 1.0) * g_ref[:, sl]

    _resident_rows_dot(a_ref, as_ref, w_ref.at[0], wb_ref, emit)


def _matmul_qkv(a, a_s, wt_all, layer, first_row, n, gains, tn):
    m, k = a.shape
    ms = a_s.shape[0]
    assert first_row % SUBLANES == 0 and tn % ATT_HEAD_DIM == 0
    att_width = n // 3
    return pl.pallas_call(
        functools.partial(_mm_qkv_kernel, n_norm_blocks=2 * att_width // tn),
        grid=(n // tn,),
        in_specs=[_resident((m, k)), _resident((ms, k)),
                  pl.BlockSpec((pl.Element(1), pl.Element(tn), pl.Element(k)),
                               lambda j: (layer, pl.multiple_of(first_row + j * tn, SUBLANES), 0)),
                  pl.BlockSpec((1, tn), lambda j: (0, j))],
        out_specs=[pl.BlockSpec((m, tn), lambda j: (0, j)), pl.BlockSpec((ms, tn), lambda j: (0, j))],
        out_shape=[jax.ShapeDtypeStruct((m, n), F32), jax.ShapeDtypeStruct((ms, n), F32)],
        scratch_shapes=[pltpu.VMEM((tn, k), BF16)],
        compiler_params=_cparams("arbitrary"),
        name="matmul_qkv",
    )(a, a_s, wt_all, gains)


def _sc_prompt_kernel(b_ref, c_ref, x_ref, hc_ref, hx_ref, w_ref, y_ref, tail_ref, ext_ref, *, ts):
    t = pl.program_id(2)
    u = c_ref[...] * x_ref[...]
    hu = hc_ref[...] * hx_ref[...]
    ext_ref[0:SUBLANES, :] = jnp.where(t > 0, hu, 0.0)
    ext_ref[SUBLANES:SUBLANES + ts, :] = u
    u2 = ext_ref[pl.ds(SUBLANES - 2, ts), :]
    u1 = ext_ref[pl.ds(SUBLANES - 1, ts), :]
    y = u2 * w_ref[0:1, :] + u1 * w_ref[1:2, :] + u * w_ref[2:3, :]
    y_ref[...] = (b_ref[...] * y).astype(y_ref.dtype)
    tail_ref[...] = ext_ref[pl.ds(ts, SUBLANES), :]


def _sc_prompt(p3, conv_w, d, ts, tc):
    bsz, s, _ = p3.shape
    ncb = d // tc
    off_b, off_c, off_x = 3 * ncb, 4 * ncb, 5 * ncb
    hb = ts // SUBLANES

    def halo(off):
        return pl.BlockSpec((None, SUBLANES, tc),
                            lambda b, c, t: (b, jnp.maximum(t * hb - 1, 0), off + c))

    def main(off):
        return pl.BlockSpec((None, ts, tc), lambda b, c, t: (b, t, off + c))

    y, tail = pl.pallas_call(
        functools.partial(_sc_prompt_kernel, ts=ts),
        grid=(bsz, ncb, s // ts),
        in_specs=[main(off_b), main(off_c), main(off_x), halo(off_c), halo(off_x),
                  pl.BlockSpec((SC_CONV, tc), lambda b, c, t: (0, c))],
        out_specs=[pl.BlockSpec((None, ts, tc), lambda b, c, t: (b, t, c)),
                   pl.BlockSpec((None, SUBLANES, tc), lambda b, c, t: (b, 0, c))],
        out_shape=[jax.ShapeDtypeStruct((bsz, s, d), BF16),
                   jax.ShapeDtypeStruct((bsz, SUBLANES, d), F32)],
        scratch_shapes=[pltpu.VMEM((ts + SUBLANES, tc), F32)],
        compiler_params=_cparams("parallel", "parallel", "arbitrary"),
        name="short_conv_prompt",
    )(p3, p3, p3, p3, p3, conv_w)
    return y, tail[:, SUBLANES - (SC_CONV - 1):, :]


def _sc_sample_kernel(b_ref, c_ref, x_ref, s0_ref, s1_ref, w_ref, y_ref, u_ref):
    u = c_ref[...] * x_ref[...]
    y = s0_ref[...] * w_ref[0:1, :] + s1_ref[...] * w_ref[1:2, :] + u * w_ref[2:3, :]
    y_ref[...] = b_ref[...] * y
    u_ref[...] = u


def _sc_sample(p2, state2d, conv_w, d, tc):
    db = p2.shape[0]
    ncb = d // tc

    def col(off):
        return pl.BlockSpec((db, tc), lambda c: (0, off + c))

    return pl.pallas_call(
        _sc_sample_kernel,
        grid=(ncb,),
        in_specs=[col(3 * ncb), col(4 * ncb), col(5 * ncb), col(0), col(ncb),
                  pl.BlockSpec((SC_CONV, tc), lambda c: (0, c))],
        out_specs=[col(0), col(0)],
        out_shape=[jax.ShapeDtypeStruct((db, d), F32), jax.ShapeDtypeStruct((db, d), F32)],
        compiler_params=_cparams("parallel"),
        name="short_conv_sample",
    )(p2, p2, p2, state2d, state2d, conv_w)


def _softplus(v):
    return jnp.maximum(v, 0.0) + jnp.log1p(jnp.exp(-jnp.abs(v)))


def _gated_group_norm(y, xs, z, dexp, ng):
    val = (y + dexp * xs) * _silu(z)
    ms = jnp.mean(val * val, axis=-1, keepdims=True)
    return val * lax.rsqrt(ms + EPS) * ng


def _ssd_prompt_kernel(x_ref, bc_ref, z_ref, dtr_ref, cwx_ref, cwbc_ref, cbx_ref, cbbc_ref, dtb_ref,
                       alog_ref, dexp_ref, ng_ref, wu_ref, wd_ref, y_ref, ht_ref, wu_bf_ref, wd_bf_ref,
                       extx, extbc, xs_s, bc_s, state):
    wu_bf_ref[...] = wu_ref[...].astype(BF16)
    wd_bf_ref[...] = wd_ref[...].astype(BF16)
    q = SSM_CHUNK
    c = pl.program_id(1)
    pad = SUBLANES
    taps = SSM_CONV

    @pl.when(c == 0)
    def _():
        extx[0:pad, :] = jnp.zeros((pad, extx.shape[1]), F32)
        extbc[0:pad, :] = jnp.zeros((pad, extbc.shape[1]), F32)
        state[...] = jnp.zeros(state.shape, F32)

    extx[pad:pad + q, :] = x_ref[...]
    extbc[pad:pad + q, :] = bc_ref[...]

    def conv(ext, w_ref, b_ref):
        full = ext[...]
        acc = pltpu.roll(full, taps - 1, 0)[pad:, :] * w_ref[0:1, :]
        for i in range(1, taps - 1):
            acc = acc + pltpu.roll(full, taps - 1 - i, 0)[pad:, :] * w_ref[i:i + 1, :]
        acc = acc + full[pad:, :] * w_ref[taps - 1:taps, :]
        return _silu(acc + b_ref[...])

    xs_s[...] = conv(extx, cwx_ref, cbx_ref)
    bc_s[...] = conv(extbc, cwbc_ref, cbbc_ref)
    extx[0:pad, :] = extx[q:q + pad, :]
    extbc[0:pad, :] = extbc[q:q + pad, :]

    dt = _softplus(dtr_ref[...] + dtb_ref[...])
    a_neg = -jnp.exp(alog_ref[...])
    ii = lax.broadcasted_iota(jnp.int32, (q, q), 0)
    jj = lax.broadcasted_iota(jnp.int32, (q, q), 1)
    causal = ii >= jj
    acum = jnp.dot(causal.astype(F32), dt * a_neg, precision=lax.Precision.HIGHEST,
                   preferred_element_type=F32)
    acum_t = acum.T
    dt_t = dt.T
    lo_half = jj < SSM_HEAD_DIM
    gs = SSM_STATE
    hd = SSM_HEAD_DIM
    hpg = xs_s.shape[1] // (SSM_GROUPS * hd)
    gw = hpg * hd
    n_bc = SSM_GROUPS * gs

    for g in range(SSM_GROUPS):
        bg = bc_s[:, g * gs:(g + 1) * gs].astype(BF16)
        cg = bc_s[:, n_bc + g * gs:n_bc + (g + 1) * gs].astype(BF16)
        cb = _bdot_nt(cg, bg)
        ys = []
        for hp in range(hpg // 2):
            e0 = g * hpg + 2 * hp
            rows = slice(e0 * hd, (e0 + 2) * hd)
            xp = xs_s[:, rows]
            xp16 = xp.astype(BF16)
            cols_a, cols_dt, parts = [], [], []
            for k in range(2):
                e = e0 + k
                col_a = jnp.broadcast_to(acum[:, e:e + 1], (q, q))
                cols_a.append(col_a)
                cols_dt.append(jnp.broadcast_to(dt[:, e:e + 1], (q, q)))
                dec = jnp.exp(jnp.where(causal, col_a - acum_t[e:e + 1, :], NEG_INF))
                wm = cb * dec * dt_t[e:e + 1, :]
                parts.append(jnp.dot(wm.astype(BF16), xp16, preferred_element_type=F32))
            y_diag = jnp.where(lo_half, parts[0], parts[1])
            sel_a = jnp.where(lo_half, cols_a[0], cols_a[1])
            sel_dt = jnp.where(lo_half, cols_dt[0], cols_dt[1])
            hpair = state[rows, :]
            y_off = _bdot_nt(cg, hpair) * jnp.exp(sel_a)
            ys.append(y_diag + y_off)
            last_a = sel_a[q - 1:q, :]
            xw = xp * jnp.exp(last_a - sel_a) * sel_dt
            st = jnp.dot(xw.T.astype(BF16), bg, preferred_element_type=F32)
            cd = jnp.concatenate([jnp.broadcast_to(cols_a[0][q - 1:q, :], (hd, gs)),
                                  jnp.broadcast_to(cols_a[1][q - 1:q, :], (hd, gs))], axis=0)
            state[rows, :] = hpair * jnp.exp(cd) + st
        gsl = slice(g * gw, (g + 1) * gw)
        yg = jnp.concatenate(ys, axis=1)
        y_ref[:, gsl] = _gated_group_norm(yg, xs_s[:, gsl], z_ref[:, gsl], dexp_ref[:, gsl],
                                          ng_ref[:, gsl]).astype(y_ref.dtype)

    @pl.when(c == pl.num_programs(1) - 1)
    def _():
        ht_ref[...] = state[...]


def _ssd_prompt(p3, dt_block, cw, cb, dtb, alog, dexp, ng, d, w_up, w_down, layer):
    bsz, s, _ = p3.shape
    q = SSM_CHUNK
    nc = s // q
    heads = d // SSM_HEAD_DIM
    f = w_up.shape[2]
    slab = f // (bsz * nc)
    assert slab * bsz * nc == f and slab % LANES == 0
    const = lambda shape: pl.BlockSpec(shape, lambda b, c: (0, 0))
    y, ht, wu_bf, wd_bf = pl.pallas_call(
        _ssd_prompt_kernel,
        grid=(bsz, nc),
        in_specs=[pl.BlockSpec((None, q, d), lambda b, c: (b, c, 7)),
                  pl.BlockSpec((None, q, d), lambda b, c: (b, c, 8)),
                  pl.BlockSpec((None, q, d), lambda b, c: (b, c, 6)),
                  pl.BlockSpec((None, q, LANES), lambda b, c: (b, c, dt_block)),
                  const((SSM_CONV, d)), const((SSM_CONV, d)), const((1, d)), const((1, d)),
                  const((1, LANES)), const((1, LANES)), const((1, d)), const((1, d)),
                  pl.BlockSpec((None, d, slab), lambda b, c: (layer, 0, b * nc + c)),
                  pl.BlockSpec((None, slab, d), lambda b, c: (layer, b * nc + c, 0))],
        out_specs=[pl.BlockSpec((None, q, d), lambda b, c: (b, c, 0)),
                   pl.BlockSpec((None, heads * SSM_HEAD_DIM, SSM_STATE), lambda b, c: (b, 0, 0)),
                   pl.BlockSpec((None, d, slab), lambda b, c: (0, 0, b * nc + c)),
                   pl.BlockSpec((None, slab, d), lambda b, c: (0, b * nc + c, 0))],
        out_shape=[jax.ShapeDtypeStruct((bsz, s, d), BF16),
                   jax.ShapeDtypeStruct((bsz, heads * SSM_HEAD_DIM, SSM_STATE), F32),
                   jax.ShapeDtypeStruct((1, d, f), BF16), jax.ShapeDtypeStruct((1, f, d), BF16)],
        scratch_shapes=[pltpu.VMEM((q + SUBLANES, d), F32), pltpu.VMEM((q + SUBLANES, d), F32),
                        pltpu.VMEM((q, d), F32), pltpu.VMEM((q, d), F32),
                        pltpu.VMEM((heads * SSM_HEAD_DIM, SSM_STATE), F32)],
        compiler_params=_cparams("arbitrary", "arbitrary"),
        name="ssd_prompt",
    )(p3, p3, p3, p3, cw[:, :d], cw[:, d:], cb[:, :d], cb[:, d:], dtb, alog, dexp, ng, w_up, w_down)
    return y, ht, (wu_bf, wd_bf)


def _ssd_sample_kernel(*refs, layer, owns_all_layers):
    (x_ref, bc_ref, z_ref, dtr_ref, s0x, s0bc, s1x, s1bc, s2x, s2bc, h0_ref, cwx_ref, cwbc_ref, cbx_ref,
     cbbc_ref, dtb_ref, alog_ref, dexp_ref, ng_ref) = refs[:19]
    y_ref, ht_all_ref, padx, padbc, padd, pada, y_s = refs[-7:]
    if owns_all_layers:
        ht_ref = ht_all_ref.at[layer]
        for other in range(ht_all_ref.shape[0]):
            if other != layer:
                ht_all_ref[other] = jnp.zeros(ht_all_ref.shape[1:], F32)
    else:
        ht_ref = ht_all_ref
    bt = DEC_TILE
    hd = SSM_HEAD_DIM
    gs = SSM_STATE
    d = x_ref.shape[1]
    n_chunk = d // LANES
    hpg = d // (SSM_GROUPS * hd)
    gw = hpg * hd
    n_bc = SSM_GROUPS * gs

    @pl.when(pl.program_id(0) == 0)
    def _():
        padx[...] = jnp.zeros(padx.shape, F32)
        padbc[...] = jnp.zeros(padbc.shape, F32)
        padd[...] = jnp.zeros(padd.shape, F32)
        pada[...] = jnp.zeros(pada.shape, F32)

    def conv(s0, s1, s2, u, w_ref, b_ref):
        acc = s0[...] * w_ref[0:1, :] + s1[...] * w_ref[1:2, :] + s2[...] * w_ref[2:3, :] + u[...] * w_ref[3:4, :]
        return _silu(acc + b_ref[...])

    padx[0:bt, :] = conv(s0x, s1x, s2x, x_ref, cwx_ref, cbx_ref)
    padbc[0:bt, :] = conv(s0bc, s1bc, s2bc, bc_ref, cwbc_ref, cbbc_ref)
    dt = _softplus(dtr_ref[...] + dtb_ref[...])
    padd[0:bt, :] = dt
    pada[0:bt, :] = jnp.exp(dt * (-jnp.exp(alog_ref[...])))
    dt_t = padd[...].T
    da_t = pada[...].T
    wide = bt * gs
    row_w = lax.broadcasted_iota(jnp.int32, (LANES, wide), 0)
    blk_w = lax.broadcasted_iota(jnp.int32, (LANES, wide), 1) // gs
    row_t = lax.broadcasted_iota(jnp.int32, (wide, LANES), 0) // gs
    lane_t = lax.broadcasted_iota(jnp.int32, (wide, LANES), 1)

    bbig = cbig_t = None
    for ci in range(n_chunk):
        g = (2 * ci) // hpg
        if (2 * ci) % hpg == 0:
            bg = padbc[:, g * gs:(g + 1) * gs]
            bbig = jnp.where(row_w == blk_w, jnp.concatenate([bg] * bt, axis=1), 0.0).astype(BF16)
            cg_t = padbc[:, n_bc + g * gs:n_bc + (g + 1) * gs].T
            cbig_t = jnp.where(row_t == lane_t, jnp.concatenate([cg_t] * bt, axis=0), 0.0).astype(BF16)
        csl = slice(ci * LANES, (ci + 1) * LANES)
        x_t = padx[:, csl].T
        dt_rows = jnp.concatenate([jnp.broadcast_to(dt_t[2 * ci:2 * ci + 1, :], (hd, LANES)),
                                   jnp.broadcast_to(dt_t[2 * ci + 1:2 * ci + 2, :], (hd, LANES))], axis=0)
        inc = jnp.dot((x_t * dt_rows).astype(BF16), bbig, preferred_element_type=F32)
        hs = []
        for k in range(bt):
            da_rows = jnp.concatenate(
                [jnp.broadcast_to(da_t[2 * ci:2 * ci + 1, k:k + 1], (hd, gs)),
                 jnp.broadcast_to(da_t[2 * ci + 1:2 * ci + 2, k:k + 1], (hd, gs))], axis=0)
            hn = h0_ref[k, csl, :] * da_rows + inc[:, k * gs:(k + 1) * gs]
            ht_ref[k, csl, :] = hn
            hs.append(hn.astype(BF16))
        y_t = jnp.dot(jnp.concatenate(hs, axis=1), cbig_t, preferred_element_type=F32)
        y_s[:, csl] = y_t.T[0:bt, :]

    for g in range(SSM_GROUPS):
        gsl = slice(g * gw, (g + 1) * gw)
        y_ref[:, gsl] = _gated_group_norm(y_s[:, gsl], padx[0:bt, gsl], z_ref[:, gsl], dexp_ref[:, gsl],
                                          ng_ref[:, gsl])


def _ssd_sample(p2, dt_block, conv_state2d, h0_all, layer, ht_all, cw, cb, dtb, alog, dexp, ng, d):
    db = p2.shape[0]
    bt = DEC_TILE
    depth, _, hp, _ = h0_all.shape
    first = ht_all is None
    row = lambda off: pl.BlockSpec((bt, d), lambda i: (i, off))
    const = lambda shape: pl.BlockSpec(shape, lambda i: (0, 0))
    in_specs = [row(7), row(8), row(6), pl.BlockSpec((bt, LANES), lambda i: (i, dt_block)),
                row(0), row(1), row(2), row(3), row(4), row(5),
                pl.BlockSpec((None, bt, hp, SSM_STATE), lambda i: (layer, i, 0, 0)),
                const((SSM_CONV, d)), const((SSM_CONV, d)), const((1, d)), const((1, d)),
                const((1, LANES)), const((1, LANES)), const((1, d)), const((1, d))]
    args = [p2, p2, p2, p2, conv_state2d, conv_state2d, conv_state2d, conv_state2d, conv_state2d,
            conv_state2d, h0_all, cw[:, :d], cw[:, d:], cb[:, :d], cb[:, d:], dtb, alog, dexp, ng]
    if first:
        state_out = pl.BlockSpec((depth, bt, hp, SSM_STATE), lambda i: (0, i, 0, 0))
        aliases = {}
    else:
        state_out = pl.BlockSpec((None, bt, hp, SSM_STATE), lambda i: (layer, i, 0, 0))
        in_specs.append(pl.BlockSpec(memory_space=pl.ANY))
        args.append(ht_all)
        aliases = {len(args) - 1: 1}
    y, ht_all = pl.pallas_call(
        functools.partial(_ssd_sample_kernel, layer=layer, owns_all_layers=first),
        grid=(db // bt,),
        in_specs=in_specs,
        out_specs=[pl.BlockSpec((bt, d), lambda i: (i, 0)), state_out],
        out_shape=[jax.ShapeDtypeStruct((db, d), F32),
                   jax.ShapeDtypeStruct((depth, db, hp, SSM_STATE), F32)],
        scratch_shapes=[pltpu.VMEM((LANES, d), F32), pltpu.VMEM((LANES, d), F32),
                        pltpu.VMEM((LANES, LANES), F32), pltpu.VMEM((LANES, LANES), F32),
                        pltpu.VMEM((bt, d), F32)],
        input_output_aliases=aliases,
        compiler_params=_cparams("arbitrary"),
        name="ssd_sample",
    )(*args)
    return y, ht_all


def _t5_bucket_np(dist):
    d = np.asarray(dist, np.int32)
    max_exact = N_BUCKETS // 2
    df = np.maximum(d, 1).astype(np.float32)
    large = max_exact + (np.log(df / np.float32(max_exact)) / np.float32(math.log(MAX_DISTANCE / max_exact))
                         * np.float32(N_BUCKETS - max_exact)).astype(np.int32)
    large = np.minimum(large, N_BUCKETS - 1)
    return np.where(d < max_exact, d, large).astype(np.int32)


def _bias_index_tables():
    blk = ATT_BLOCK
    qi = np.arange(blk)[:, None]
    kj = np.arange(2 * blk)[None, :]
    sdist = qi + blk - kj
    idx_p, idx_d = [], []
    for window, dil in ATT_GROUPS:
        nk = window // dil
        band = (sdist >= 0) & (sdist <= nk)
        idx_p.append(np.where(band, _t5_bucket_np(np.clip(sdist, 0, nk) * dil), -1))
        steps = nk - np.arange(nk)
        idx_d.append(np.broadcast_to(_t5_bucket_np(steps * dil)[:, None], (nk, LANES)))
    return np.stack(idx_p).astype(np.int32), np.stack(idx_d).astype(np.int32)


def _bias_kernel(tab_ref, idxp_ref, idxd_ref, bp_ref, bd_ref):
    h = pl.program_id(0)
    ip = idxp_ref[...]
    idd = idxd_ref[...]
    accp = jnp.full(ip.shape, NEG_INF, F32)
    accd = jnp.full(idd.shape, NEG_INF, F32)
    for b in range(N_BUCKETS):
        v = tab_ref[b, h]
        accp = jnp.where(ip == b, v, accp)
        accd = jnp.where(idd == b, v, accd)
    bp_ref[...] = accp
    bd_ref[...] = accd


def _build_bias(rel_bias):
    idx_p, idx_d = _bias_index_tables()
    n_heads = rel_bias.shape[1]
    blk = ATT_BLOCK
    nk = idx_d.shape[1]
    return pl.pallas_call(
        _bias_kernel,
        grid=(n_heads,),
        in_specs=[pl.BlockSpec(memory_space=pltpu.SMEM),
                  pl.BlockSpec((None, blk, 2 * blk), lambda h: (h // ATT_HPG, 0, 0)),
                  pl.BlockSpec((None, nk, LANES), lambda h: (h // ATT_HPG, 0, 0))],
        out_specs=[pl.BlockSpec((None, blk, 2 * blk), lambda h: (h, 0, 0)),
                   pl.BlockSpec((None, nk, LANES), lambda h: (h, 0, 0))],
        out_shape=[jax.ShapeDtypeStruct((n_heads, blk, 2 * blk), F32),
                   jax.ShapeDtypeStruct((n_heads, nk, LANES), F32)],
        compiler_params=_cparams("arbitrary"),
        name="t5_bias",
    )(rel_bias, jnp.asarray(idx_p), jnp.asarray(idx_d))


ATT_TILE = ATT_BLOCK * max(dil for _, dil in ATT_GROUPS)
MERGE_ROWS = 256
ATT_UNITS = 4


def _merge_three(parts):
    m_all = jnp.maximum(jnp.maximum(parts[0][1], parts[1][1]), parts[2][1])
    w = jnp.exp(parts[0][1] - m_all)
    num = w * parts[0][0]
    den = w * parts[0][2]
    for o, m, l in parts[1:]:
        w = jnp.exp(m - m_all)
        num = num + w * o
        den = den + w * l
    return num / den


def _attn_prompt_kernel(*refs, n_cast):
    ng = len(ATT_GROUPS)
    blk = ATT_BLOCK
    hd = ATT_HEAD_DIM
    bias_ref = refs[5 * ng]
    w_srcs = refs[5 * ng + 1:5 * ng + 1 + n_cast]
    y_ref = refs[5 * ng + 1 + n_cast]
    w_dsts = refs[5 * ng + 2 + n_cast:5 * ng + 2 + 2 * n_cast]
    scr = refs[5 * ng + 2 + 2 * n_cast:]
    for src, dst in zip(w_srcs, w_dsts):
        dst[...] = src[...].astype(BF16)
    first_tile = pl.program_id(1) == 0
    ones = jnp.ones((2 * blk, hd), BF16)

    def rows_of(start, n, dil):
        return pl.ds(start, n) if dil == 1 else pl.ds(start, n, stride=dil)

    def unit(gi, dil, start, k2, v2, bias2):
        q_ref = refs[5 * gi]
        o_s, m_s, l_s = scr[3 * gi:3 * gi + 3]
        rows = rows_of(start, blk, dil)
        s = _bdot_nt(q_ref[rows, :] * ATT_SCALE, k2) + bias2
        m = jnp.max(s, axis=-1, keepdims=True)
        p = jnp.exp(s - m)
        ol = _bdot(p, jnp.concatenate([v2.astype(BF16), ones], axis=1))
        o_s[rows, :] = ol[:, :hd]
        m_s[rows, :] = jnp.broadcast_to(m, (blk, hd))
        l_s[rows, :] = ol[:, hd:]

    for gi, (window, dil) in enumerate(ATT_GROUPS):
        _, kc_ref, vc_ref, kp_ref, vp_ref = refs[5 * gi:5 * gi + 5]
        span = blk * dil
        n_sub = ATT_TILE // span
        bias = bias_ref[gi]
        bias_head = jnp.concatenate([jnp.where(first_tile, NEG_INF, bias[:, :blk]), bias[:, blk:]], axis=1)

        def head_unit(r, gi=gi, dil=dil, kc_ref=kc_ref, vc_ref=vc_ref, kp_ref=kp_ref, vp_ref=vp_ref,
                      bias_head=bias_head):
            prow, crow = rows_of(r, blk, dil), rows_of(r, blk, dil)
            k2 = jnp.concatenate([kp_ref[prow, :], kc_ref[crow, :]], axis=0)
            v2 = jnp.concatenate([vp_ref[prow, :], vc_ref[crow, :]], axis=0)
            unit(gi, dil, r, k2, v2, bias_head)

        def inner_unit(j, r, gi=gi, dil=dil, span=span, kc_ref=kc_ref, vc_ref=vc_ref, bias=bias):
            both = rows_of((j - 1) * span + r, 2 * blk, dil)
            unit(gi, dil, j * span + r, kc_ref[both, :], vc_ref[both, :], bias)

        if dil >= ATT_UNITS:
            def head_body(it, carry, head_unit=head_unit):
                for k in range(ATT_UNITS):
                    head_unit(it * ATT_UNITS + k)
                return carry
            lax.fori_loop(0, dil // ATT_UNITS, head_body, 0)
        else:
            for r in range(dil):
                head_unit(r)
        if n_sub > 1:
            per_it = max(ATT_UNITS // dil, 1)
            n_it = (n_sub - 1) // per_it

            def inner_body(it, carry, inner_unit=inner_unit, per_it=per_it, dil=dil):
                for k in range(per_it):
                    for r in range(dil):
                        inner_unit(1 + it * per_it + k, r)
                return carry
            lax.fori_loop(0, n_it, inner_body, 0)
            for j in range(1 + n_it * per_it, n_sub):
                for r in range(dil):
                    inner_unit(j, r)

    for c in range(ATT_TILE // MERGE_ROWS):
        rs = slice(c * MERGE_ROWS, (c + 1) * MERGE_ROWS)
        parts = [(scr[3 * gi][rs, :], scr[3 * gi + 1][rs, :], scr[3 * gi + 2][rs, :]) for gi in range(ng)]
        y_ref[rs, :] = _merge_three(parts).astype(y_ref.dtype)


def _attn_prompt(qkv, bias_p, bsz, s, layer, cast_ws):
    hdim = ATT_HEAD_DIM
    ng = len(ATT_GROUPS)
    n_heads = qkv.shape[1] // (3 * hdim)
    tile = ATT_TILE
    assert s % tile == 0
    nt = s // tile
    in_specs, args = [], []
    for gi, (window, dil) in enumerate(ATT_GROUPS):
        span = ATT_BLOCK * dil
        per = tile // span

        def cur(part, gi=gi):
            return pl.BlockSpec((tile, hdim), lambda b, i, h: (b * nt + i, part * n_heads + gi * ATT_HPG + h))

        def prev(part, gi=gi, per=per):
            return pl.BlockSpec((span, hdim),
                                lambda b, i, h: (jnp.maximum((b * nt + i) * per - 1, 0),
                                                 part * n_heads + gi * ATT_HPG + h))

        in_specs += [cur(0), cur(1), cur(2), prev(1), prev(2)]
        args += [qkv] * 5
    in_specs.append(pl.BlockSpec((ng, None, ATT_BLOCK, 2 * ATT_BLOCK), lambda b, i, h: (0, h, 0, 0)))
    args.append(bias_p)
    steps = bsz * nt * ATT_HPG
    out_specs = [pl.BlockSpec((tile, hdim), lambda b, i, h: (b * nt + i, h))]
    out_shape = [jax.ShapeDtypeStruct((bsz * s, ATT_HPG * hdim), BF16)]
    for w_all, axis in cast_ws:
        _, r, c = w_all.shape
        slab = w_all.shape[axis] // steps
        assert slab * steps == w_all.shape[axis] and slab % (LANES if axis == 2 else 2 * SUBLANES) == 0
        shape = (None, slab, c) if axis == 1 else (None, r, slab)

        def at(first, axis=axis):
            step = lambda b, i, h: (b * nt + i) * ATT_HPG + h
            if axis == 1:
                return lambda b, i, h: (first, step(b, i, h), 0)
            return lambda b, i, h: (first, 0, step(b, i, h))

        in_specs.append(pl.BlockSpec(shape, at(layer)))
        args.append(w_all)
        out_specs.append(pl.BlockSpec(shape, at(0)))
        out_shape.append(jax.ShapeDtypeStruct((1, r, c), BF16))
    outs = pl.pallas_call(
        functools.partial(_attn_prompt_kernel, n_cast=len(cast_ws)),
        grid=(bsz, nt, ATT_HPG),
        in_specs=in_specs,
        out_specs=out_specs,
        out_shape=out_shape,
        scratch_shapes=[pltpu.VMEM((tile, hdim), F32)] * (3 * ng),
        compiler_params=_cparams("arbitrary", "arbitrary", "arbitrary"),
        name="attn_prompt",
    )(*args)
    return outs[0], outs[1:]


REDUCE_WAYS = 4
DEC_KEYS = 32


def _reduce_rows(x, op):
    n = x.shape[0]
    part = op(x.reshape((REDUCE_WAYS, n // REDUCE_WAYS) + x.shape[1:]), axis=1)
    return op(part, axis=0)


def _attn_sample_kernel(qkv_ref, c0_ref, c1_ref, c2_ref, bias_ref, bias0_ref, y_ref):
    ng = len(ATT_GROUPS)
    hp = ATT_HPG
    caches = (c0_ref, c1_ref, c2_ref)
    zeros = jnp.zeros((hp, ATT_HEAD_DIM), F32)
    ones = jnp.ones((hp, ATT_HEAD_DIM), F32)

    def body(b, carry):
        parts = []
        for gi in range(ng):
            q = qkv_ref[b, gi]
            k_new = qkv_ref[b, ng + gi]
            v_new = qkv_ref[b, 2 * ng + gi]
            q2 = jnp.concatenate([q * ATT_SCALE, zeros], axis=0)
            s_self = jnp.sum(k_new * q, axis=-1, keepdims=True) * ATT_SCALE + bias0_ref[gi]
            m = jnp.concatenate([s_self, zeros], axis=0)
            l = jnp.concatenate([ones, ones], axis=0)
            o = jnp.concatenate([zeros, v_new], axis=0)
            for c in range(caches[gi].shape[1] // DEC_KEYS):
                ks = slice(c * DEC_KEYS, (c + 1) * DEC_KEYS)
                kv = caches[gi][b, ks]
                s = jnp.sum(kv * q2[None], axis=-1, keepdims=True) + bias_ref[gi, ks]
                m_new = jnp.maximum(m, _reduce_rows(s, jnp.max))
                alpha = jnp.exp(m - m_new)
                p = jnp.exp(s - m_new[None])
                l = alpha * l + _reduce_rows(p, jnp.sum)
                p_on_v = pltpu.roll(p, hp, 1)
                o = pltpu.roll(alpha, hp, 0) * o + _reduce_rows(p_on_v * kv, jnp.sum)
                m = m_new
            parts.append((o[hp:], m[:hp], l[:hp]))
        y_ref[b] = _merge_three(parts)
        return carry

    lax.fori_loop(0, DEC_TILE, body, 0)


def _attn_sample(qkv2, caches, layer, bias_d, bias0):
    db = qkv2.shape[0]
    bt = DEC_TILE
    hdim = ATT_HEAD_DIM
    ng = len(ATT_GROUPS)
    views, specs = [], []
    for (window, dil), cache in zip(ATT_GROUPS, caches):
        nk = window // dil
        depth, _, wb = cache.shape[:3]
        assert wb == window, "cached window must equal the group's window"
        views.append(cache.reshape(depth, db, nk, dil, 2 * ATT_HPG, hdim))
        specs.append(pl.BlockSpec((None, bt, nk, None, 2 * ATT_HPG, hdim),
                                  lambda i: (layer, i, 0, 0, 0, 0)))
    y = pl.pallas_call(
        _attn_sample_kernel,
        grid=(db // bt,),
        in_specs=[pl.BlockSpec((bt, 3 * ng, ATT_HPG, hdim), lambda i: (i, 0, 0, 0))] + specs +
                 [pl.BlockSpec(bias_d.shape, lambda i: (0, 0, 0, 0)),
                  pl.BlockSpec(bias0.shape, lambda i: (0, 0, 0))],
        out_specs=pl.BlockSpec((bt, ATT_HPG, hdim), lambda i: (i, 0, 0)),
        out_shape=jax.ShapeDtypeStruct((db, ATT_HPG, hdim), F32),
        compiler_params=_cparams("parallel"),
        name="attn_sample",
    )(qkv2.reshape(db, 3 * ng, ATT_HPG, hdim), *views, bias_d, bias0)
    return y.reshape(db, ATT_HPG * hdim)


def _branch_kernel(ysc_ref, yssm_ref, yatt_ref, ysc_s, yssm_s, yatt_s, wsc_ref, wssm_ref, watt_ref,
                   g0_ref, g1_ref, g2_ref, g0_s, g1_s, g2_s, o_ref, os_ref):
    def merged(ysc, yssm, yatt, g0, g1, g2, out):
        a = jnp.dot(ysc[...], wsc_ref[...], preferred_element_type=F32)
        b = jnp.dot(yssm[...], wssm_ref[...], preferred_element_type=F32)
        c = jnp.dot(yatt[...], watt_ref[...], preferred_element_type=F32)
        out[...] = (_sigmoid(g0[...]) * a + _sigmoid(g1[...]) * b + _sigmoid(g2[...]) * c).astype(out.dtype)

    merged(ysc_ref, yssm_ref, yatt_ref, g0_ref, g1_ref, g2_ref, o_ref)

    @pl.when(pl.program_id(1) == 0)
    def _():
        merged(ysc_s, yssm_s, yatt_s, g0_s, g1_s, g2_s, os_ref)


def _branch_merge(ys, ys_s, wsc, wssm, watt, layer, p2, p2_s, tm, tn):
    m, d = ys[0].shape
    ms = ys_s[0].shape[0]
    da = ys[2].shape[1]
    nj, ni = d // tn, m // tm
    rows = lambda width: pl.BlockSpec((tm, width), lambda j, i: (i, 0))
    rows_s = lambda width: pl.BlockSpec((ms, width), lambda j, i: (0, 0))
    wcol = lambda k: pl.BlockSpec((None, k, tn), lambda j, i: (layer, 0, j))
    gate = lambda br: pl.BlockSpec((tm, tn), lambda j, i: (i, br * nj + j))
    gate_s = lambda br: pl.BlockSpec((ms, tn), lambda j, i: (0, br * nj + j))
    return pl.pallas_call(
        _branch_kernel,
        grid=(nj, ni),
        in_specs=[rows(d), rows(d), rows(da), rows_s(d), rows_s(d), rows_s(da), wcol(d), wcol(d), wcol(da),
                  gate(0), gate(1), gate(2), gate_s(0), gate_s(1), gate_s(2)],
        out_specs=[pl.BlockSpec((tm, tn), lambda j, i: (i, j)), pl.BlockSpec((ms, tn), lambda j, i: (0, j))],
        out_shape=[jax.ShapeDtypeStruct((m, d), BF16), jax.ShapeDtypeStruct((ms, d), BF16)],
        compiler_params=_cparams("parallel", "arbitrary"),
        name="branch_merge",
    )(*ys, *ys_s, wsc, wssm, watt, p2, p2, p2, p2_s, p2_s, p2_s)


def _outproj_kernel(x_ref, mg_ref, x_s, mg_s, w_ref, g_ref, x1_ref, h2_ref, x1_s, h2_s):
    def project(x, mg, x1_out, h2_out):
        x1 = x[...] + jnp.dot(mg[...], w_ref[...], preferred_element_type=F32)
        x1_out[...] = x1
        ms = jnp.mean(x1 * x1, axis=-1, keepdims=True)
        h2_out[...] = (x1 * lax.rsqrt(ms + EPS) * g_ref[...]).astype(h2_out.dtype)

    project(x_ref, mg_ref, x1_ref, h2_ref)

    @pl.when(pl.program_id(0) == 0)
    def _():
        project(x_s, mg_s, x1_s, h2_s)


def _outproj(x2d, merged, x2d_s, merged_s, w_out, layer, g2, tm):
    m, d = x2d.shape
    ms = x2d_s.shape[0]
    rows = pl.BlockSpec((tm, d), lambda i: (i, 0))
    rows_s = pl.BlockSpec((ms, d), lambda i: (0, 0))
    return pl.pallas_call(
        _outproj_kernel,
        grid=(m // tm,),
        in_specs=[rows, rows, rows_s, rows_s, pl.BlockSpec((None, d, d), lambda i: (layer, 0, 0)),
                  pl.BlockSpec((1, d), lambda i: (0, 0))],
        out_specs=[rows, rows, rows_s, rows_s],
        out_shape=[jax.ShapeDtypeStruct((m, d), F32), jax.ShapeDtypeStruct((m, d), BF16),
                   jax.ShapeDtypeStruct((ms, d), F32), jax.ShapeDtypeStruct((ms, d), BF16)],
        compiler_params=_cparams("arbitrary"),
        name="outproj_norm",
    )(x2d, merged, x2d_s, merged_s, w_out, g2.reshape(1, d))


def _mlp_kernel(x1_ref, h2_ref, x1_s, h2_s, wu_ref, wd_ref, o_ref, os_ref):
    def accumulate(x1, h2, out):
        @pl.when(pl.program_id(1) == 0)
        def _():
            out[...] = x1[...]

        u = jnp.dot(h2[...], wu_ref[...], preferred_element_type=F32)
        u = jnp.square(jnp.maximum(u, 0.0)).astype(BF16)
        out[...] += jnp.dot(u, wd_ref[...], preferred_element_type=F32)

    accumulate(x1_ref, h2_ref, o_ref)

    @pl.when(pl.program_id(0) == 0)
    def _():
        accumulate(x1_s, h2_s, os_ref)


def _mlp(x1, h2, x1_s, h2_s, w_up, w_down, layer, tm, tf):
    m, d = x1.shape
    ms = x1_s.shape[0]
    f = w_up.shape[2]
    rows = pl.BlockSpec((tm, d), lambda i, j: (i, 0))
    rows_s = pl.BlockSpec((ms, d), lambda i, j: (0, 0))
    return pl.pallas_call(
        _mlp_kernel,
        grid=(m // tm, f // tf),
        in_specs=[rows, rows, rows_s, rows_s, pl.BlockSpec((None, d, tf), lambda i, j: (layer, 0, j)),
                  pl.BlockSpec((None, tf, d), lambda i, j: (layer, j, 0))],
        out_specs=[rows, rows_s],
        out_shape=[jax.ShapeDtypeStruct((m, d), F32), jax.ShapeDtypeStruct((ms, d), F32)],
        compiler_params=_cparams("arbitrary", "arbitrary"),
        name="mlp",
    )(x1, h2, x1_s, h2_s, w_up, w_down)


def _row_tile(m, pref):
    return pref if m % pref == 0 else m


def _pad_lanes(v):
    return jnp.pad(v.astype(F32), (0, LANES - v.shape[0])).reshape(1, LANES)


def _prep_layer(l, shared, norm1_g, sc_conv_w, ssm_conv_w, ssm_conv_b, ssm_dt_bias, ssm_A_log, ssm_D,
                ssm_norm_g, q_norm_g, k_norm_g, norm2_g):
    wt_in = shared["wt_in"]
    d = wt_in.shape[2]
    heads = d // SSM_HEAD_DIM
    aw = ATT_HPG * len(ATT_GROUPS) * ATT_HEAD_DIM
    n_main = N_BRANCH * d + 3 * d + d + (d + 2 * SSM_GROUPS * SSM_STATE)
    assert n_main % LANES == 0 and heads <= LANES
    n_att_heads = aw // ATT_HEAD_DIM
    gains = jnp.concatenate([jnp.tile(q_norm_g[l], n_att_heads), jnp.tile(k_norm_g[l], n_att_heads),
                             jnp.ones((aw,), F32)]).reshape(1, 3 * aw)
    return dict(
        shared, layer=l, n_main=n_main, qkv_row=n_main + heads, n_qkv=3 * aw,
        norm1_g=norm1_g[l], norm2_g=norm2_g[l], qk_gains=gains,
        sc_conv_w=sc_conv_w[l], ssm_conv_w=ssm_conv_w[l], ssm_conv_b=ssm_conv_b[l].reshape(1, -1),
        dt_bias=_pad_lanes(ssm_dt_bias[l]), a_log=_pad_lanes(ssm_A_log[l]),
        d_exp=jnp.repeat(ssm_D[l].astype(F32), SSM_HEAD_DIM).reshape(1, d),
        ssm_norm_g=ssm_norm_g[l].reshape(1, d),
    )


def _in_proj(x2d, x2d_s, w):
    m = x2d.shape[0]
    h = _rmsnorm(x2d, w["norm1_g"], _row_tile(m, NORM_ROWS))
    h_s = _rmsnorm(x2d_s, w["norm1_g"], x2d_s.shape[0])
    p = _matmul_wcast(h, h_s, w["wt_in"], w["layer"], 0, w["n_main"] + MXU_COLS, MXU_COLS, name="in_proj_main")
    qkv = _matmul_qkv(h, h_s, w["wt_in"], w["layer"], w["qkv_row"], w["n_qkv"], w["qk_gains"], MXU_COLS)
    return (p[0], qkv[0]), (p[1], qkv[1])


def _tail(x2d, p, ys, x2d_s, p_s, ys_s, w, w_bf):
    m = x2d.shape[0]
    w_br_sc, w_br_ssm, w_br_att, w_out, w_up, w_down = w_bf
    merged, merged_s = _branch_merge(ys, ys_s, w_br_sc, w_br_ssm, w_br_att, 0, p, p_s,
                                     _row_tile(m, BRANCH_TILE[0]), BRANCH_TILE[1])
    x1, h2, x1_s, h2_s = _outproj(x2d, merged, x2d_s, merged_s, w_out, 0, w["norm2_g"], _row_tile(m, NORM_ROWS))
    return _mlp(x1, h2, x1_s, h2_s, w_up, w_down, 0, _row_tile(m, MLP_TILE[0]), MLP_TILE[1])


def _kv_rows(qkv3, gi, keep):
    aw = qkv3.shape[2] // 3
    gw = ATT_HPG * ATT_HEAD_DIM
    bsz, ln, _ = qkv3.shape
    k = qkv3[:, ln - keep:, aw + gi * gw:aw + (gi + 1) * gw].reshape(bsz, keep, ATT_HPG, ATT_HEAD_DIM)
    v = qkv3[:, ln - keep:, 2 * aw + gi * gw:2 * aw + (gi + 1) * gw].reshape(bsz, keep, ATT_HPG, ATT_HEAD_DIM)
    return jnp.stack([k, v], axis=2)


def _prompt_mixers(proj, w, bias_p, bsz, s, d):
    p, qkv = proj
    p3 = p.reshape(bsz, s, -1)
    qkv3 = qkv.reshape(bsz, s, -1)
    ysc, sc_new = _sc_prompt(p3, w["sc_conv_w"], d, *SC_TILE)
    yssm, ht, w_mlp = _ssd_prompt(p3, w["n_main"] // LANES, w["ssm_conv_w"], w["ssm_conv_b"], w["dt_bias"],
                                  w["a_log"], w["d_exp"], w["ssm_norm_g"], d, w["w_up"], w["w_down"], w["layer"])
    xbc_off = N_BRANCH * d + 3 * d + d
    ssm_conv_new = p3[:, s - (SSM_CONV - 1):, xbc_off:w["n_main"]]
    kv_new = [_kv_rows(qkv3, gi, min(window, s)) for gi, (window, _) in enumerate(ATT_GROUPS)]
    yatt, w_bf = _attn_prompt(qkv, bias_p, bsz, s, w["layer"],
                              [(w["w_br_sc"], 2), (w["w_br_ssm"], 2), (w["w_br_att"], 1), (w["w_out"], 2)])
    heads = d // SSM_HEAD_DIM
    st = (sc_new, ssm_conv_new, ht.reshape(bsz, heads, SSM_HEAD_DIM, SSM_STATE), kv_new[0], kv_new[1], kv_new[2])
    return (ysc.reshape(bsz * s, d), yssm.reshape(bsz * s, d), yatt), st, tuple(w_bf) + w_mlp


def _sample_mixers(proj, w, bias_d, bias0, sc_buf, ssm_buf, ssm_h0_all, ssm_ht_all, kv_caches, d):
    p, qkv = proj
    db = p.shape[0]
    l = w["layer"]
    ysc, u_new = _sc_sample(p, sc_buf.reshape(db, -1), w["sc_conv_w"], d, SC_SAMPLE_COLS)
    sc_new = jnp.concatenate([sc_buf[:, 1:], u_new[:, None, :]], axis=1)
    heads = d // SSM_HEAD_DIM
    depth = ssm_h0_all.shape[0]
    yssm, ssm_ht_all = _ssd_sample(p, w["n_main"] // LANES, ssm_buf.reshape(db, -1),
                                   ssm_h0_all.reshape(depth, db, heads * SSM_HEAD_DIM, SSM_STATE), l, ssm_ht_all,
                                   w["ssm_conv_w"], w["ssm_conv_b"], w["dt_bias"], w["a_log"], w["d_exp"],
                                   w["ssm_norm_g"], d)
    xbc_off = N_BRANCH * d + 3 * d + d
    ssm_conv_new = jnp.concatenate([ssm_buf[:, 1:], p[:, None, xbc_off:w["n_main"]]], axis=1)
    yatt = _attn_sample(qkv, kv_caches, l, bias_d, bias0)
    qkv3 = qkv.reshape(db, 1, -1)
    kv_new = [_kv_rows(qkv3, gi, 1) for gi in range(len(ATT_GROUPS))]
    st = (sc_new, ssm_conv_new, kv_new[0], kv_new[1], kv_new[2])
    return (ysc.astype(BF16), yssm.astype(BF16), yatt.astype(BF16)), st, ssm_ht_all


def kernel(x_prompt, x_sample, state_sc_conv, state_ssm_conv, state_ssm, cache_kv_w128, cache_kv_w512,
           cache_kv_w2048, norm1_g, w_in, sc_conv_w, ssm_conv_w, ssm_conv_b, ssm_dt_bias, ssm_A_log, ssm_D,
           ssm_norm_g, q_norm_g, k_norm_g, rel_bias, w_br_sc, w_br_ssm, w_br_att, w_out, norm2_g, w_up, w_down):
    depth = w_in.shape[0]
    ng = len(ATT_GROUPS)
    bias_p, bias_d = _build_bias(rel_bias)
    bias_p = bias_p.reshape((ng, ATT_HPG) + bias_p.shape[1:])
    bias_d = bias_d.reshape((ng, ATT_HPG) + bias_d.shape[1:]).transpose(0, 2, 1, 3)
    bias_d = jnp.concatenate([bias_d, jnp.zeros_like(bias_d)], axis=2)
    bias0 = jnp.broadcast_to(rel_bias[0].reshape(ng, ATT_HPG, 1), (ng, ATT_HPG, LANES))
    bsz, s, d = x_prompt.shape
    db, t, _ = x_sample.shape
    assert t == 1, "decode kernels handle one new token per sample"
    xp, xs = x_prompt.reshape(bsz * s, d), x_sample.reshape(db, d)
    p_new, s_new, s_ssm = [], [], None
    shared = dict(wt_in=jnp.swapaxes(w_in, 1, 2), w_br_sc=w_br_sc, w_br_ssm=w_br_ssm, w_br_att=w_br_att,
                  w_out=w_out, w_up=w_up, w_down=w_down)
    for l in range(depth):
        w = _prep_layer(l, shared, norm1_g, sc_conv_w, ssm_conv_w, ssm_conv_b, ssm_dt_bias, ssm_A_log, ssm_D,
                        ssm_norm_g, q_norm_g, k_norm_g, norm2_g)
        proj, proj_s = _in_proj(xp, xs, w)
        ys, st, w_bf = _prompt_mixers(proj, w, bias_p, bsz, s, d)
        p_new.append(st)
        ys_s, st, s_ssm = _sample_mixers(proj_s, w, bias_d, bias0, state_sc_conv[l], state_ssm_conv[l], state_ssm,
                                         s_ssm, (cache_kv_w128, cache_kv_w512, cache_kv_w2048), d)
        s_new.append(st)
        xp, xs = _tail(xp, proj[0], ys, xs, proj_s[0], ys_s, w, w_bf)
    p_out = [jnp.stack(a) for a in zip(*p_new)]
    s_out = [jnp.stack(a) for a in zip(*s_new)]
    s_out.insert(2, s_ssm.reshape(state_ssm.shape))
    return (xp.reshape(bsz, s, d), xs.reshape(db, 1, d), *p_out, *s_out)
```

```python
import functools
import math

import numpy as np
import jax
import jax.numpy as jnp
from jax import lax
from jax.experimental import pallas as pl
from jax.experimental.pallas import tpu as pltpu

F32 = jnp.float32
BF16 = jnp.bfloat16

N_BRANCH = 3
SC_CONV = 3
SSM_HEAD_DIM = 64
SSM_GROUPS = 8
SSM_STATE = 128
SSM_CONV = 4
SSM_CHUNK = 128
ATT_HEAD_DIM = 128
ATT_GROUPS = ((128, 1), (512, 4), (2048, 16))
ATT_HPG = 4
ATT_BLOCK = 128
ATT_SCALE = 1.0 / math.sqrt(ATT_HEAD_DIM)
N_BUCKETS = 32
MAX_DISTANCE = 2048
EPS = 1e-6

LANES = 128
SUBLANES = 8
MXU_COLS = 256
VMEM_LIMIT_BYTES = 60 * 1024 * 1024

DEC_TILE = 8
NEG_INF = float("-inf")


def _cparams(*sem):
    return pltpu.CompilerParams(dimension_semantics=sem, vmem_limit_bytes=VMEM_LIMIT_BYTES)


def _sigmoid(v):
    return 1.0 / (1.0 + jnp.exp(-v))


def _silu(v):
    return v * _sigmoid(v)


def _bdot(a, b):
    return jnp.dot(a.astype(BF16), b.astype(BF16), preferred_element_type=F32)


def _bdot_nt(a, b):
    return lax.dot_general(a.astype(BF16), b.astype(BF16), (((1,), (1,)), ((), ())),
                           preferred_element_type=F32)


def _rmsnorm_kernel(x_ref, g_ref, o_ref):
    x = x_ref[...]
    ms = jnp.mean(x * x, axis=-1, keepdims=True)
    o_ref[...] = (x * lax.rsqrt(ms + EPS) * g_ref[...]).astype(o_ref.dtype)


def _rmsnorm(x2d, g, tm):
    m, d = x2d.shape
    return pl.pallas_call(
        _rmsnorm_kernel,
        grid=(m // tm,),
        in_specs=[pl.BlockSpec((tm, d), lambda i: (i, 0)),
                  pl.BlockSpec((1, d), lambda i: (0, 0))],
        out_specs=pl.BlockSpec((tm, d), lambda i: (i, 0)),
        out_shape=jax.ShapeDtypeStruct((m, d), BF16),
        compiler_params=_cparams("parallel"),
        name="rmsnorm",
    )(x2d, g.reshape(1, d))


ROW_CHUNK = 1024

NORM_ROWS = 512
BRANCH_TILE = (1024, 512)
MLP_TILE = (512, 1024)
SC_TILE = (1024, 1024)
SC_SAMPLE_COLS = 512


def _resident_rows_dot(a_ref, as_ref, w_ref, wb_ref, emit):
    wb_ref[...] = w_ref[...].astype(BF16)
    m = a_ref.shape[0]
    rc = min(m, ROW_CHUNK)
    for r in range(m // rc):
        rows = slice(r * rc, (r + 1) * rc)
        emit(0, rows, _bdot_nt(a_ref[rows, :], wb_ref[...]))
    emit(1, slice(0, as_ref.shape[0]), _bdot_nt(as_ref[...], wb_ref[...]))


def _mm_wcast_kernel(a_ref, as_ref, w_ref, o_ref, os_ref, wb_ref):
    outs = (o_ref, os_ref)

    def emit(which, rows, acc):
        outs[which][rows, :] = acc

    _resident_rows_dot(a_ref, as_ref, w_ref, wb_ref, emit)


def _resident(shape):
    return pl.BlockSpec(shape, lambda j: (0,) * len(shape), pipeline_mode=pl.Buffered(1))


def _matmul_wcast(a, a_s, wt_all, layer, first_block, n_cols, tn, name):
    m, k = a.shape
    ms = a_s.shape[0]
    assert m % min(m, ROW_CHUNK) == 0
    return pl.pallas_call(
        _mm_wcast_kernel,
        grid=(n_cols // tn,),
        in_specs=[_resident((m, k)), _resident((ms, k)),
                  pl.BlockSpec((None, tn, k), lambda j: (layer, first_block + j, 0))],
        out_specs=[pl.BlockSpec((m, tn), lambda j: (0, j)), pl.BlockSpec((ms, tn), lambda j: (0, j))],
        out_shape=[jax.ShapeDtypeStruct((m, n_cols), F32), jax.ShapeDtypeStruct((ms, n_cols), F32)],
        scratch_shapes=[pltpu.VMEM((tn, k), BF16)],
        compiler_params=_cparams("arbitrary"),
        name=name,
    )(a, a_s, wt_all)


def _mm_qkv_kernel(a_ref, as_ref, w_ref, g_ref, o_ref, os_ref, wb_ref, *, n_norm_blocks):
    is_qk = pl.program_id(0) < n_norm_blocks
    outs = (o_ref, os_ref)

    def emit(which, rows, acc):
        for h in range(acc.shape[1] // ATT_HEAD_DIM):
            sl = slice(h * ATT_HEAD_DIM, (h + 1) * ATT_HEAD_DIM)
            t = acc[:, sl]
            ms = jnp.mean(t * t, axis=-1, keepdims=True)
            outs[which][rows, sl] = t * jnp.where(is_qk, lax.rsqrt(ms + EPS), 1.0) * g_ref[:, sl]

    _resident_rows_dot(a_ref, as_ref, w_ref.at[0], wb_ref, emit)


def _matmul_qkv(a, a_s, wt_all, layer, first_row, n, gains, tn):
    m, k = a.shape
    ms = a_s.shape[0]
    assert first_row % SUBLANES == 0 and tn % ATT_HEAD_DIM == 0
    att_width = n // 3
    return pl.pallas_call(
        functools.partial(_mm_qkv_kernel, n_norm_blocks=2 * att_width // tn),
        grid=(n // tn,),
        in_specs=[_resident((m, k)), _resident((ms, k)),
                  pl.BlockSpec((pl.Element(1), pl.Element(tn), pl.Element(k)),
                               lambda j: (layer, pl.multiple_of(first_row + j * tn, SUBLANES), 0)),
                  pl.BlockSpec((1, tn), lambda j: (0, j))],
        out_specs=[pl.BlockSpec((m, tn), lambda j: (0, j)), pl.BlockSpec((ms, tn), lambda j: (0, j))],
        out_shape=[jax.ShapeDtypeStruct((m, n), F32), jax.ShapeDtypeStruct((ms, n), F32)],
        scratch_shapes=[pltpu.VMEM((tn, k), BF16)],
        compiler_params=_cparams("arbitrary"),
        name="matmul_qkv",
    )(a, a_s, wt_all, gains)


def _sc_prompt_kernel(b_ref, c_ref, x_ref, hc_ref, hx_ref, w_ref, y_ref, tail_ref, ext_ref, *, ts):
    t = pl.program_id(2)
    u = c_ref[...] * x_ref[...]
    hu = hc_ref[...] * hx_ref[...]
    ext_ref[0:SUBLANES, :] = jnp.where(t > 0, hu, 0.0)
    ext_ref[SUBLANES:SUBLANES + ts, :] = u
    u2 = ext_ref[pl.ds(SUBLANES - 2, ts), :]
    u1 = ext_ref[pl.ds(SUBLANES - 1, ts), :]
    y = u2 * w_ref[0:1, :] + u1 * w_ref[1:2, :] + u * w_ref[2:3, :]
    y_ref[...] = (b_ref[...] * y).astype(y_ref.dtype)
    tail_ref[...] = ext_ref[pl.ds(ts, SUBLANES), :]


def _sc_prompt(p3, conv_w, d, ts, tc):
    bsz, s, _ = p3.shape
    ncb = d // tc
    off_b, off_c, off_x = 3 * ncb, 4 * ncb, 5 * ncb
    hb = ts // SUBLANES

    def halo(off):
        return pl.BlockSpec((None, SUBLANES, tc),
                            lambda b, c, t: (b, jnp.maximum(t * hb - 1, 0), off + c))

    def main(off):
        return pl.BlockSpec((None, ts, tc), lambda b, c, t: (b, t, off + c))

    y, tail = pl.pallas_call(
        functools.partial(_sc_prompt_kernel, ts=ts),
        grid=(bsz, ncb, s // ts),
        in_specs=[main(off_b), main(off_c), main(off_x), halo(off_c), halo(off_x),
                  pl.BlockSpec((SC_CONV, tc), lambda b, c, t: (0, c))],
        out_specs=[pl.BlockSpec((None, ts, tc), lambda b, c, t: (b, t, c)),
                   pl.BlockSpec((None, SUBLANES, tc), lambda b, c, t: (b, 0, c))],
        out_shape=[jax.ShapeDtypeStruct((bsz, s, d), BF16),
                   jax.ShapeDtypeStruct((bsz, SUBLANES, d), F32)],
        scratch_shapes=[pltpu.VMEM((ts + SUBLANES, tc), F32)],
        compiler_params=_cparams("parallel", "parallel", "arbitrary"),
        name="short_conv_prompt",
    )(p3, p3, p3, p3, p3, conv_w)
    return y, tail[:, SUBLANES - (SC_CONV - 1):, :]


def _sc_sample_kernel(b_ref, c_ref, x_ref, s0_ref, s1_ref, w_ref, y_ref, u_ref):
    u = c_ref[...] * x_ref[...]
    y = s0_ref[...] * w_ref[0:1, :] + s1_ref[...] * w_ref[1:2, :] + u * w_ref[2:3, :]
    y_ref[...] = b_ref[...] * y
    u_ref[...] = u


def _sc_sample(p2, state2d, conv_w, d, tc):
    db = p2.shape[0]
    ncb = d // tc

    def col(off):
        return pl.BlockSpec((db, tc), lambda c: (0, off + c))

    return pl.pallas_call(
        _sc_sample_kernel,
        grid=(ncb,),
        in_specs=[col(3 * ncb), col(4 * ncb), col(5 * ncb), col(0), col(ncb),
                  pl.BlockSpec((SC_CONV, tc), lambda c: (0, c))],
        out_specs=[col(0), col(0)],
        out_shape=[jax.ShapeDtypeStruct((db, d), F32), jax.ShapeDtypeStruct((db, d), F32)],
        compiler_params=_cparams("parallel"),
        name="short_conv_sample",
    )(p2, p2, p2, state2d, state2d, conv_w)


def _softplus(v):
    return jnp.maximum(v, 0.0) + jnp.log1p(jnp.exp(-jnp.abs(v)))


def _gated_group_norm(y, xs, z, dexp, ng):
    val = (y + dexp * xs) * _silu(z)
    ms = jnp.mean(val * val, axis=-1, keepdims=True)
    return val * lax.rsqrt(ms + EPS) * ng


def _ssd_prompt_kernel(x_ref, bc_ref, z_ref, dtr_ref, cwx_ref, cwbc_ref, cbx_ref, cbbc_ref, dtb_ref,
                       alog_ref, dexp_ref, ng_ref, wu_ref, wd_ref, y_ref, ht_ref, wu_bf_ref, wd_bf_ref,
                       extx, extbc, xs_s, bc_s, state):
    wu_bf_ref[...] = wu_ref[...].astype(BF16)
    wd_bf_ref[...] = wd_ref[...].astype(BF16)
    q = SSM_CHUNK
    c = pl.program_id(1)
    pad = SUBLANES
    taps = SSM_CONV

    @pl.when(c == 0)
    def _():
        extx[0:pad, :] = jnp.zeros((pad, extx.shape[1]), F32)
        extbc[0:pad, :] = jnp.zeros((pad, extbc.shape[1]), F32)
        state[...] = jnp.zeros(state.shape, F32)

    extx[pad:pad + q, :] = x_ref[...]
    extbc[pad:pad + q, :] = bc_ref[...]

    def conv(ext, w_ref, b_ref):
        full = ext[...]
        acc = pltpu.roll(full, taps - 1, 0)[pad:, :] * w_ref[0:1, :]
        for i in range(1, taps - 1):
            acc = acc + pltpu.roll(full, taps - 1 - i, 0)[pad:, :] * w_ref[i:i + 1, :]
        acc = acc + full[pad:, :] * w_ref[taps - 1:taps, :]
        return _silu(acc + b_ref[...])

    xs_s[...] = conv(extx, cwx_ref, cbx_ref)
    bc_s[...] = conv(extbc, cwbc_ref, cbbc_ref)
    extx[0:pad, :] = extx[q:q + pad, :]
    extbc[0:pad, :] = extbc[q:q + pad, :]

    dt = _softplus(dtr_ref[...] + dtb_ref[...])
    a_neg = -jnp.exp(alog_ref[...])
    ii = lax.broadcasted_iota(jnp.int32, (q, q), 0)
    jj = lax.broadcasted_iota(jnp.int32, (q, q), 1)
    causal = ii >= jj
    acum = jnp.dot(causal.astype(F32), dt * a_neg, precision=lax.Precision.HIGHEST,
                   preferred_element_type=F32)
    acum_t = acum.T
    dt_t = dt.T
    lo_half = jj < SSM_HEAD_DIM
    gs = SSM_STATE
    hd = SSM_HEAD_DIM
    hpg = xs_s.shape[1] // (SSM_GROUPS * hd)
    gw = hpg * hd
    n_bc = SSM_GROUPS * gs

    for g in range(SSM_GROUPS):
        bg = bc_s[:, g * gs:(g + 1) * gs].astype(BF16)
        cg = bc_s[:, n_bc + g * gs:n_bc + (g + 1) * gs].astype(BF16)
        cb = _bdot_nt(cg, bg)
        ys = []
        for hp in range(hpg // 2):
            e0 = g * hpg + 2 * hp
            rows = slice(e0 * hd, (e0 + 2) * hd)
            xp = xs_s[:, rows]
            xp16 = xp.astype(BF16)
            cols_a, cols_dt, parts = [], [], []
            for k in range(2):
                e = e0 + k
                col_a = jnp.broadcast_to(acum[:, e:e + 1], (q, q))
                cols_a.append(col_a)
                cols_dt.append(jnp.broadcast_to(dt[:, e:e + 1], (q, q)))
                dec = jnp.exp(jnp.where(causal, col_a - acum_t[e:e + 1, :], NEG_INF))
                wm = cb * dec * dt_t[e:e + 1, :]
                parts.append(jnp.dot(wm.astype(BF16), xp16, preferred_element_type=F32))
            y_diag = jnp.where(lo_half, parts[0], parts[1])
            sel_a = jnp.where(lo_half, cols_a[0], cols_a[1])
            sel_dt = jnp.where(lo_half, cols_dt[0], cols_dt[1])
            hpair = state[rows, :]
            y_off = _bdot_nt(cg, hpair) * jnp.exp(sel_a)
            ys.append(y_diag + y_off)
            last_a = sel_a[q - 1:q, :]
            xw = xp * jnp.exp(last_a - sel_a) * sel_dt
            st = jnp.dot(xw.T.astype(BF16), bg, preferred_element_type=F32)
            cd = jnp.concatenate([jnp.broadcast_to(cols_a[0][q - 1:q, :], (hd, gs)),
                                  jnp.broadcast_to(cols_a[1][q - 1:q, :], (hd, gs))], axis=0)
            state[rows, :] = hpair * jnp.exp(cd) + st
        gsl = slice(g * gw, (g + 1) * gw)
        yg = jnp.concatenate(ys, axis=1)
        y_ref[:, gsl] = _gated_group_norm(yg, xs_s[:, gsl], z_ref[:, gsl], dexp_ref[:, gsl],
                                          ng_ref[:, gsl]).astype(y_ref.dtype)

    @pl.when(c == pl.num_programs(1) - 1)
    def _():
        ht_ref[...] = state[...]


def _ssd_prompt(p3, dt_block, cw, cb, dtb, alog, dexp, ng, d, w_up, w_down, layer):
    bsz, s, _ = p3.shape
    q = SSM_CHUNK
    nc = s // q
    heads = d // SSM_HEAD_DIM
    f = w_up.shape[2]
    slab = f // (bsz * nc)
    assert slab * bsz * nc == f and slab % LANES == 0
    const = lambda shape: pl.BlockSpec(shape, lambda b, c: (0, 0))
    y, ht, wu_bf, wd_bf = pl.pallas_call(
        _ssd_prompt_kernel,
        grid=(bsz, nc),
        in_specs=[pl.BlockSpec((None, q, d), lambda b, c: (b, c, 7)),
                  pl.BlockSpec((None, q, d), lambda b, c: (b, c, 8)),
                  pl.BlockSpec((None, q, d), lambda b, c: (b, c, 6)),
                  pl.BlockSpec((None, q, LANES), lambda b, c: (b, c, dt_block)),
                  const((SSM_CONV, d)), const((SSM_CONV, d)), const((1, d)), const((1, d)),
                  const((1, LANES)), const((1, LANES)), const((1, d)), const((1, d)),
                  pl.BlockSpec((None, d, slab), lambda b, c: (layer, 0, b * nc + c)),
                  pl.BlockSpec((None, slab, d), lambda b, c: (layer, b * nc + c, 0))],
        out_specs=[pl.BlockSpec((None, q, d), lambda b, c: (b, c, 0)),
                   pl.BlockSpec((None, heads * SSM_HEAD_DIM, SSM_STATE), lambda b, c: (b, 0, 0)),
                   pl.BlockSpec((None, d, slab), lambda b, c: (0, 0, b * nc + c)),
                   pl.BlockSpec((None, slab, d), lambda b, c: (0, b * nc + c, 0))],
        out_shape=[jax.ShapeDtypeStruct((bsz, s, d), BF16),
                   jax.ShapeDtypeStruct((bsz, heads * SSM_HEAD_DIM, SSM_STATE), F32),
                   jax.ShapeDtypeStruct((1, d, f), BF16), jax.ShapeDtypeStruct((1, f, d), BF16)],
        scratch_shapes=[pltpu.VMEM((q + SUBLANES, d), F32), pltpu.VMEM((q + SUBLANES, d), F32),
                        pltpu.VMEM((q, d), F32), pltpu.VMEM((q, d), F32),
                        pltpu.VMEM((heads * SSM_HEAD_DIM, SSM_STATE), F32)],
        compiler_params=_cparams("arbitrary", "arbitrary"),
        name="ssd_prompt",
    )(p3, p3, p3, p3, cw[:, :d], cw[:, d:], cb[:, :d], cb[:, d:], dtb, alog, dexp, ng, w_up, w_down)
    return y, ht, (wu_bf, wd_bf)


def _ssd_sample_kernel(*refs, layer, owns_all_layers):
    (x_ref, bc_ref, z_ref, dtr_ref, s0x, s0bc, s1x, s1bc, s2x, s2bc, h0_ref, cwx_ref, cwbc_ref, cbx_ref,
     cbbc_ref, dtb_ref, alog_ref, dexp_ref, ng_ref) = refs[:19]
    y_ref, ht_all_ref, padx, padbc, padd, pada, y_s = refs[-7:]
    if owns_all_layers:
        ht_ref = ht_all_ref.at[layer]
        for other in range(ht_all_ref.shape[0]):
            if other != layer:
                ht_all_ref[other] = jnp.zeros(ht_all_ref.shape[1:], F32)
    else:
        ht_ref = ht_all_ref
    bt = DEC_TILE
    hd = SSM_HEAD_DIM
    gs = SSM_STATE
    d = x_ref.shape[1]
    n_chunk = d // LANES
    hpg = d // (SSM_GROUPS * hd)
    gw = hpg * hd
    n_bc = SSM_GROUPS * gs

    @pl.when(pl.program_id(0) == 0)
    def _():
        padx[...] = jnp.zeros(padx.shape, F32)
        padbc[...] = jnp.zeros(padbc.shape, F32)
        padd[...] = jnp.zeros(padd.shape, F32)
        pada[...] = jnp.zeros(pada.shape, F32)

    def conv(s0, s1, s2, u, w_ref, b_ref):
        acc = s0[...] * w_ref[0:1, :] + s1[...] * w_ref[1:2, :] + s2[...] * w_ref[2:3, :] + u[...] * w_ref[3:4, :]
        return _silu(acc + b_ref[...])

    padx[0:bt, :] = conv(s0x, s1x, s2x, x_ref, cwx_ref, cbx_ref)
    padbc[0:bt, :] = conv(s0bc, s1bc, s2bc, bc_ref, cwbc_ref, cbbc_ref)
    dt = _softplus(dtr_ref[...] + dtb_ref[...])
    padd[0:bt, :] = dt
    pada[0:bt, :] = jnp.exp(dt * (-jnp.exp(alog_ref[...])))
    dt_t = padd[...].T
    da_t = pada[...].T
    wide = bt * gs
    row_w = lax.broadcasted_iota(jnp.int32, (LANES, wide), 0)
    blk_w = lax.broadcasted_iota(jnp.int32, (LANES, wide), 1) // gs
    row_t = lax.broadcasted_iota(jnp.int32, (wide, LANES), 0) // gs
    lane_t = lax.broadcasted_iota(jnp.int32, (wide, LANES), 1)

    bbig = cbig_t = None
    for ci in range(n_chunk):
        g = (2 * ci) // hpg
        if (2 * ci) % hpg == 0:
            bg = padbc[:, g * gs:(g + 1) * gs]
            bbig = jnp.where(row_w == blk_w, jnp.concatenate([bg] * bt, axis=1), 0.0).astype(BF16)
            cg_t = padbc[:, n_bc + g * gs:n_bc + (g + 1) * gs].T
            cbig_t = jnp.where(row_t == lane_t, jnp.concatenate([cg_t] * bt, axis=0), 0.0).astype(BF16)
        csl = slice(ci * LANES, (ci + 1) * LANES)
        x_t = padx[:, csl].T
        dt_rows = jnp.concatenate([jnp.broadcast_to(dt_t[2 * ci:2 * ci + 1, :], (hd, LANES)),
                                   jnp.broadcast_to(dt_t[2 * ci + 1:2 * ci + 2, :], (hd, LANES))], axis=0)
        inc = jnp.dot((x_t * dt_rows).astype(BF16), bbig, preferred_element_type=F32)
        hs = []
        for k in range(bt):
            da_rows = jnp.concatenate(
                [jnp.broadcast_to(da_t[2 * ci:2 * ci + 1, k:k + 1], (hd, gs)),
                 jnp.broadcast_to(da_t[2 * ci + 1:2 * ci + 2, k:k + 1], (hd, gs))], axis=0)
            hn = h0_ref[k, csl, :] * da_rows + inc[:, k * gs:(k + 1) * gs]
            ht_ref[k, csl, :] = hn
            hs.append(hn.astype(BF16))
        y_t = jnp.dot(jnp.concatenate(hs, axis=1), cbig_t, preferred_element_type=F32)
        y_s[:, csl] = y_t.T[0:bt, :]

    for g in range(SSM_GROUPS):
        gsl = slice(g * gw, (g + 1) * gw)
        y_ref[:, gsl] = _gated_group_norm(y_s[:, gsl], padx[0:bt, gsl], z_ref[:, gsl], dexp_ref[:, gsl],
                                          ng_ref[:, gsl])


def _ssd_sample(p2, dt_block, conv_state2d, h0_all, layer, ht_all, cw, cb, dtb, alog, dexp, ng, d):
    db = p2.shape[0]
    bt = DEC_TILE
    depth, _, hp, _ = h0_all.shape
    first = ht_all is None
    row = lambda off: pl.BlockSpec((bt, d), lambda i: (i, off))
    const = lambda shape: pl.BlockSpec(shape, lambda i: (0, 0))
    in_specs = [row(7), row(8), row(6), pl.BlockSpec((bt, LANES), lambda i: (i, dt_block)),
                row(0), row(1), row(2), row(3), row(4), row(5),
                pl.BlockSpec((None, bt, hp, SSM_STATE), lambda i: (layer, i, 0, 0)),
                const((SSM_CONV, d)), const((SSM_CONV, d)), const((1, d)), const((1, d)),
                const((1, LANES)), const((1, LANES)), const((1, d)), const((1, d))]
    args = [p2, p2, p2, p2, conv_state2d, conv_state2d, conv_state2d, conv_state2d, conv_state2d,
            conv_state2d, h0_all, cw[:, :d], cw[:, d:], cb[:, :d], cb[:, d:], dtb, alog, dexp, ng]
    if first:
        state_out = pl.BlockSpec((depth, bt, hp, SSM_STATE), lambda i: (0, i, 0, 0))
        aliases = {}
    else:
        state_out = pl.BlockSpec((None, bt, hp, SSM_STATE), lambda i: (layer, i, 0, 0))
        in_specs.append(pl.BlockSpec(memory_space=pl.ANY))
        args.append(ht_all)
        aliases = {len(args) - 1: 1}
    y, ht_all = pl.pallas_call(
        functools.partial(_ssd_sample_kernel, layer=layer, owns_all_layers=first),
        grid=(db // bt,),
        in_specs=in_specs,
        out_specs=[pl.BlockSpec((bt, d), lambda i: (i, 0)), state_out],
        out_shape=[jax.ShapeDtypeStruct((db, d), F32),
                   jax.ShapeDtypeStruct((depth, db, hp, SSM_STATE), F32)],
        scratch_shapes=[pltpu.VMEM((LANES, d), F32), pltpu.VMEM((LANES, d), F32),
                        pltpu.VMEM((LANES, LANES), F32), pltpu.VMEM((LANES, LANES), F32),
                        pltpu.VMEM((bt, d), F32)],
        input_output_aliases=aliases,
        compiler_params=_cparams("arbitrary"),
        name="ssd_sample",
    )(*args)
    return y, ht_all


def _t5_bucket_np(dist):
    d = np.asarray(dist, np.int32)
    max_exact = N_BUCKETS // 2
    df = np.maximum(d, 1).astype(np.float32)
    large = max_exact + (np.log(df / np.float32(max_exact)) / np.float32(math.log(MAX_DISTANCE / max_exact))
                         * np.float32(N_BUCKETS - max_exact)).astype(np.int32)
    large = np.minimum(large, N_BUCKETS - 1)
    return np.where(d < max_exact, d, large).astype(np.int32)


def _bias_index_tables():
    blk = ATT_BLOCK
    qi = np.arange(blk)[:, None]
    kj = np.arange(2 * blk)[None, :]
    sdist = qi + blk - kj
    idx_p, idx_d = [], []
    for window, dil in ATT_GROUPS:
        nk = window // dil
        band = (sdist >= 0) & (sdist <= nk)
        idx_p.append(np.where(band, _t5_bucket_np(np.clip(sdist, 0, nk) * dil), -1))
        steps = nk - np.arange(nk)
        idx_d.append(np.broadcast_to(_t5_bucket_np(steps * dil)[:, None], (nk, LANES)))
    return np.stack(idx_p).astype(np.int32), np.stack(idx_d).astype(np.int32)


def _bias_kernel(tab_ref, idxp_ref, idxd_ref, bp_ref, bd_ref):
    h = pl.program_id(0)
    ip = idxp_ref[...]
    idd = idxd_ref[...]
    accp = jnp.full(ip.shape, NEG_INF, F32)
    accd = jnp.full(idd.shape, NEG_INF, F32)
    for b in range(N_BUCKETS):
        v = tab_ref[b, h]
        accp = jnp.where(ip == b, v, accp)
        accd = jnp.where(idd == b, v, accd)
    bp_ref[...] = accp
    bd_ref[...] = accd


def _build_bias(rel_bias):
    idx_p, idx_d = _bias_index_tables()
    n_heads = rel_bias.shape[1]
    blk = ATT_BLOCK
    nk = idx_d.shape[1]
    return pl.pallas_call(
        _bias_kernel,
        grid=(n_heads,),
        in_specs=[pl.BlockSpec(memory_space=pltpu.SMEM),
                  pl.BlockSpec((None, blk, 2 * blk), lambda h: (h // ATT_HPG, 0, 0)),
                  pl.BlockSpec((None, nk, LANES), lambda h: (h // ATT_HPG, 0, 0))],
        out_specs=[pl.BlockSpec((None, blk, 2 * blk), lambda h: (h, 0, 0)),
                   pl.BlockSpec((None, nk, LANES), lambda h: (h, 0, 0))],
        out_shape=[jax.ShapeDtypeStruct((n_heads, blk, 2 * blk), F32),
                   jax.ShapeDtypeStruct((n_heads, nk, LANES), F32)],
        compiler_params=_cparams("arbitrary"),
        name="t5_bias",
    )(rel_bias, jnp.asarray(idx_p), jnp.asarray(idx_d))


ATT_TILE = ATT_BLOCK * max(dil for _, dil in ATT_GROUPS)
MERGE_ROWS = 256
ATT_UNITS = 8


def _merge_three(parts):
    m_all = jnp.maximum(jnp.maximum(parts[0][1], parts[1][1]), parts[2][1])
    w = jnp.exp(parts[0][1] - m_all)
    num = w * parts[0][0]
    den = w * parts[0][2]
    for o, m, l in parts[1:]:
        w = jnp.exp(m - m_all)
        num = num + w * o
        den = den + w * l
    return num / den


def _attn_prompt_kernel(*refs, n_cast):
    ng = len(ATT_GROUPS)
    blk = ATT_BLOCK
    hd = ATT_HEAD_DIM
    bias_ref = refs[5 * ng]
    w_srcs = refs[5 * ng + 1:5 * ng + 1 + n_cast]
    y_ref = refs[5 * ng + 1 + n_cast]
    w_dsts = refs[5 * ng + 2 + n_cast:5 * ng + 2 + 2 * n_cast]
    scr = refs[5 * ng + 2 + 2 * n_cast:]
    for src, dst in zip(w_srcs, w_dsts):
        dst[...] = src[...].astype(BF16)
    first_tile = pl.program_id(1) == 0
    ones = jnp.ones((2 * blk, hd), BF16)

    def rows_of(start, n, dil):
        return pl.ds(start, n) if dil == 1 else pl.ds(start, n, stride=dil)

    def unit(gi, dil, start, k2, v2, bias2):
        q_ref = refs[5 * gi]
        o_s, m_s, l_s = scr[3 * gi:3 * gi + 3]
        rows = rows_of(start, blk, dil)
        s = _bdot_nt(q_ref[rows, :] * ATT_SCALE, k2) + bias2
        m = jnp.max(s, axis=-1, keepdims=True)
        p = jnp.exp(s - m)
        ol = _bdot(p, jnp.concatenate([v2.astype(BF16), ones], axis=1))
        o_s[rows, :] = ol[:, :hd]
        m_s[rows, :] = jnp.broadcast_to(m, (blk, hd))
        l_s[rows, :] = ol[:, hd:]

    for gi, (window, dil) in enumerate(ATT_GROUPS):
        _, kc_ref, vc_ref, kp_ref, vp_ref = refs[5 * gi:5 * gi + 5]
        span = blk * dil
        n_sub = ATT_TILE // span
        bias = bias_ref[gi]
        bias_head = jnp.concatenate([jnp.where(first_tile, NEG_INF, bias[:, :blk]), bias[:, blk:]], axis=1)

        def head_unit(r, gi=gi, dil=dil, kc_ref=kc_ref, vc_ref=vc_ref, kp_ref=kp_ref, vp_ref=vp_ref,
                      bias_head=bias_head):
            prow, crow = rows_of(r, blk, dil), rows_of(r, blk, dil)
            k2 = jnp.concatenate([kp_ref[prow, :], kc_ref[crow, :]], axis=0)
            v2 = jnp.concatenate([vp_ref[prow, :], vc_ref[crow, :]], axis=0)
            unit(gi, dil, r, k2, v2, bias_head)

        def inner_unit(j, r, gi=gi, dil=dil, span=span, kc_ref=kc_ref, vc_ref=vc_ref, bias=bias):
            both = rows_of((j - 1) * span + r, 2 * blk, dil)
            unit(gi, dil, j * span + r, kc_ref[both, :], vc_ref[both, :], bias)

        if dil >= ATT_UNITS:
            def head_body(it, carry, head_unit=head_unit):
                for k in range(ATT_UNITS):
                    head_unit(it * ATT_UNITS + k)
                return carry
            lax.fori_loop(0, dil // ATT_UNITS, head_body, 0)
        else:
            for r in range(dil):
                head_unit(r)
        if n_sub > 1:
            per_it = max(ATT_UNITS // dil, 1)
            n_it = (n_sub - 1) // per_it

            def inner_body(it, carry, inner_unit=inner_unit, per_it=per_it, dil=dil):
                for k in range(per_it):
                    for r in range(dil):
                        inner_unit(1 + it * per_it + k, r)
                return carry
            lax.fori_loop(0, n_it, inner_body, 0)
            for j in range(1 + n_it * per_it, n_sub):
                for r in range(dil):
                    inner_unit(j, r)

    for c in range(ATT_TILE // MERGE_ROWS):
        rs = slice(c * MERGE_ROWS, (c + 1) * MERGE_ROWS)
        parts = [(scr[3 * gi][rs, :], scr[3 * gi + 1][rs, :], scr[3 * gi + 2][rs, :]) for gi in range(ng)]
        y_ref[rs, :] = _merge_three(parts).astype(y_ref.dtype)


def _attn_prompt(qkv, bias_p, bsz, s, layer, cast_ws):
    hdim = ATT_HEAD_DIM
    ng = len(ATT_GROUPS)
    n_heads = qkv.shape[1] // (3 * hdim)
    tile = ATT_TILE
    assert s % tile == 0
    nt = s // tile
    in_specs, args = [], []
    for gi, (window, dil) in enumerate(ATT_GROUPS):
        span = ATT_BLOCK * dil
        per = tile // span

        def cur(part, gi=gi):
            return pl.BlockSpec((tile, hdim), lambda b, i, h: (b * nt + i, part * n_heads + gi * ATT_HPG + h))

        def prev(part, gi=gi, per=per):
            return pl.BlockSpec((span, hdim),
                                lambda b, i, h: (jnp.maximum((b * nt + i) * per - 1, 0),
                                                 part * n_heads + gi * ATT_HPG + h))

        in_specs += [cur(0), cur(1), cur(2), prev(1), prev(2)]
        args += [qkv] * 5
    in_specs.append(pl.BlockSpec((ng, None, ATT_BLOCK, 2 * ATT_BLOCK), lambda b, i, h: (0, h, 0, 0)))
    args.append(bias_p)
    steps = bsz * nt * ATT_HPG
    out_specs = [pl.BlockSpec((tile, hdim), lambda b, i, h: (b * nt + i, h))]
    out_shape = [jax.ShapeDtypeStruct((bsz * s, ATT_HPG * hdim), BF16)]
    for w_all, axis in cast_ws:
        _, r, c = w_all.shape
        slab = w_all.shape[axis] // steps
        assert slab * steps == w_all.shape[axis] and slab % (LANES if axis == 2 else 2 * SUBLANES) == 0
        shape = (None, slab, c) if axis == 1 else (None, r, slab)

        def at(first, axis=axis):
            step = lambda b, i, h: (b * nt + i) * ATT_HPG + h
            if axis == 1:
                return lambda b, i, h: (first, step(b, i, h), 0)
            return lambda b, i, h: (first, 0, step(b, i, h))

        in_specs.append(pl.BlockSpec(shape, at(layer)))
        args.append(w_all)
        out_specs.append(pl.BlockSpec(shape, at(0)))
        out_shape.append(jax.ShapeDtypeStruct((1, r, c), BF16))
    outs = pl.pallas_call(
        functools.partial(_attn_prompt_kernel, n_cast=len(cast_ws)),
        grid=(bsz, nt, ATT_HPG),
        in_specs=in_specs,
        out_specs=out_specs,
        out_shape=out_shape,
        scratch_shapes=[pltpu.VMEM((tile, hdim), F32)] * (3 * ng),
        compiler_params=_cparams("arbitrary", "arbitrary", "arbitrary"),
        name="attn_prompt",
    )(*args)
    return outs[0], outs[1:]


REDUCE_WAYS = 4
DEC_KEYS = 32


def _reduce_rows(x, op):
    n = x.shape[0]
    part = op(x.reshape((REDUCE_WAYS, n // REDUCE_WAYS) + x.shape[1:]), axis=1)
    return op(part, axis=0)


def _attn_sample_kernel(qkv_ref, c0_ref, c1_ref, c2_ref, bias_ref, bias0_ref, y_ref):
    ng = len(ATT_GROUPS)
    hp = ATT_HPG
    caches = (c0_ref, c1_ref, c2_ref)
    zeros = jnp.zeros((hp, ATT_HEAD_DIM), F32)
    ones = jnp.ones((hp, ATT_HEAD_DIM), F32)

    def body(b, carry):
        parts = []
        for gi in range(ng):
            q = qkv_ref[b, gi]
            k_new = qkv_ref[b, ng + gi]
            v_new = qkv_ref[b, 2 * ng + gi]
            q2 = jnp.concatenate([q * ATT_SCALE, zeros], axis=0)
            s_self = jnp.sum(k_new * q, axis=-1, keepdims=True) * ATT_SCALE + bias0_ref[gi]
            m = jnp.concatenate([s_self, zeros], axis=0)
            l = jnp.concatenate([ones, ones], axis=0)
            o = jnp.concatenate([zeros, v_new], axis=0)
            for c in range(caches[gi].shape[1] // DEC_KEYS):
                ks = slice(c * DEC_KEYS, (c + 1) * DEC_KEYS)
                kv = caches[gi][b, ks]
                s = jnp.sum(kv * q2[None], axis=-1, keepdims=True) + bias_ref[gi, ks]
                m_new = jnp.maximum(m, _reduce_rows(s, jnp.max))
                alpha = jnp.exp(m - m_new)
                p = jnp.exp(s - m_new[None])
                l = alpha * l + _reduce_rows(p, jnp.sum)
                p_on_v = pltpu.roll(p, hp, 1)
                o = pltpu.roll(alpha, hp, 0) * o + _reduce_rows(p_on_v * kv, jnp.sum)
                m = m_new
            parts.append((o[hp:], m[:hp], l[:hp]))
        y_ref[b] = _merge_three(parts)
        return carry

    lax.fori_loop(0, DEC_TILE, body, 0)


def _attn_sample(qkv2, caches, layer, bias_d, bias0):
    db = qkv2.shape[0]
    bt = DEC_TILE
    hdim = ATT_HEAD_DIM
    ng = len(ATT_GROUPS)
    views, specs = [], []
    for (window, dil), cache in zip(ATT_GROUPS, caches):
        nk = window // dil
        depth, _, wb = cache.shape[:3]
        assert wb == window, "cached window must equal the group's window"
        views.append(cache.reshape(depth, db, nk, dil, 2 * ATT_HPG, hdim))
        specs.append(pl.BlockSpec((None, bt, nk, None, 2 * ATT_HPG, hdim),
                                  lambda i: (layer, i, 0, 0, 0, 0)))
    y = pl.pallas_call(
        _attn_sample_kernel,
        grid=(db // bt,),
        in_specs=[pl.BlockSpec((bt, 3 * ng, ATT_HPG, hdim), lambda i: (i, 0, 0, 0))] + specs +
                 [pl.BlockSpec(bias_d.shape, lambda i: (0, 0, 0, 0)),
                  pl.BlockSpec(bias0.shape, lambda i: (0, 0, 0))],
        out_specs=pl.BlockSpec((bt, ATT_HPG, hdim), lambda i: (i, 0, 0)),
        out_shape=jax.ShapeDtypeStruct((db, ATT_HPG, hdim), F32),
        compiler_params=_cparams("parallel"),
        name="attn_sample",
    )(qkv2.reshape(db, 3 * ng, ATT_HPG, hdim), *views, bias_d, bias0)
    return y.reshape(db, ATT_HPG * hdim)


def _branch_kernel(ysc_ref, yssm_ref, yatt_ref, ysc_s, yssm_s, yatt_s, wsc_ref, wssm_ref, watt_ref,
                   g0_ref, g1_ref, g2_ref, g0_s, g1_s, g2_s, o_ref, os_ref):
    def merged(ysc, yssm, yatt, g0, g1, g2, out):
        a = jnp.dot(ysc[...], wsc_ref[...], preferred_element_type=F32)
        b = jnp.dot(yssm[...], wssm_ref[...], preferred_element_type=F32)
        c = jnp.dot(yatt[...], watt_ref[...], preferred_element_type=F32)
        out[...] = (_sigmoid(g0[...]) * a + _sigmoid(g1[...]) * b + _sigmoid(g2[...]) * c).astype(out.dtype)

    merged(ysc_ref, yssm_ref, yatt_ref, g0_ref, g1_ref, g2_ref, o_ref)

    @pl.when(pl.program_id(1) == 0)
    def _():
        merged(ysc_s, yssm_s, yatt_s, g0_s, g1_s, g2_s, os_ref)


def _branch_merge(ys, ys_s, wsc, wssm, watt, layer, p2, p2_s, tm, tn):
    m, d = ys[0].shape
    ms = ys_s[0].shape[0]
    da = ys[2].shape[1]
    nj, ni = d // tn, m // tm
    rows = lambda width: pl.BlockSpec((tm, width), lambda j, i: (i, 0))
    rows_s = lambda width: pl.BlockSpec((ms, width), lambda j, i: (0, 0))
    wcol = lambda k: pl.BlockSpec((None, k, tn), lambda j, i: (layer, 0, j))
    gate = lambda br: pl.BlockSpec((tm, tn), lambda j, i: (i, br * nj + j))
    gate_s = lambda br: pl.BlockSpec((ms, tn), lambda j, i: (0, br * nj + j))
    return pl.pallas_call(
        _branch_kernel,
        grid=(nj, ni),
        in_specs=[rows(d), rows(d), rows(da), rows_s(d), rows_s(d), rows_s(da), wcol(d), wcol(d), wcol(da),
                  gate(0), gate(1), gate(2), gate_s(0), gate_s(1), gate_s(2)],
        out_specs=[pl.BlockSpec((tm, tn), lambda j, i: (i, j)), pl.BlockSpec((ms, tn), lambda j, i: (0, j))],
        out_shape=[jax.ShapeDtypeStruct((m, d), BF16), jax.ShapeDtypeStruct((ms, d), BF16)],
        compiler_params=_cparams("parallel", "arbitrary"),
        name="branch_merge",
    )(*ys, *ys_s, wsc, wssm, watt, p2, p2, p2, p2_s, p2_s, p2_s)


def _outproj_kernel(x_ref, mg_ref, x_s, mg_s, w_ref, g_ref, x1_ref, h2_ref, x1_s, h2_s):
    def project(x, mg, x1_out, h2_out):
        x1 = x[...] + jnp.dot(mg[...], w_ref[...], preferred_element_type=F32)
        x1_out[...] = x1
        ms = jnp.mean(x1 * x1, axis=-1, keepdims=True)
        h2_out[...] = (x1 * lax.rsqrt(ms + EPS) * g_ref[...]).astype(h2_out.dtype)

    project(x_ref, mg_ref, x1_ref, h2_ref)

    @pl.when(pl.program_id(0) == 0)
    def _():
        project(x_s, mg_s, x1_s, h2_s)


def _outproj(x2d, merged, x2d_s, merged_s, w_out, layer, g2, tm):
    m, d = x2d.shape
    ms = x2d_s.shape[0]
    rows = pl.BlockSpec((tm, d), lambda i: (i, 0))
    rows_s = pl.BlockSpec((ms, d), lambda i: (0, 0))
    return pl.pallas_call(
        _outproj_kernel,
        grid=(m // tm,),
        in_specs=[rows, rows, rows_s, rows_s, pl.BlockSpec((None, d, d), lambda i: (layer, 0, 0)),
                  pl.BlockSpec((1, d), lambda i: (0, 0))],
        out_specs=[rows, rows, rows_s, rows_s],
        out_shape=[jax.ShapeDtypeStruct((m, d), F32), jax.ShapeDtypeStruct((m, d), BF16),
                   jax.ShapeDtypeStruct((ms, d), F32), jax.ShapeDtypeStruct((ms, d), BF16)],
        compiler_params=_cparams("arbitrary"),
        name="outproj_norm",
    )(x2d, merged, x2d_s, merged_s, w_out, g2.reshape(1, d))


def _mlp_kernel(x1_ref, h2_ref, x1_s, h2_s, wu_ref, wd_ref, o_ref, os_ref):
    def accumulate(x1, h2, out):
        @pl.when(pl.program_id(1) == 0)
        def _():
            out[...] = x1[...]

        u = jnp.dot(h2[...], wu_ref[...], preferred_element_type=F32)
        u = jnp.square(jnp.maximum(u, 0.0)).astype(BF16)
        out[...] += jnp.dot(u, wd_ref[...], preferred_element_type=F32)

    accumulate(x1_ref, h2_ref, o_ref)

    @pl.when(pl.program_id(0) == 0)
    def _():
        accumulate(x1_s, h2_s, os_ref)


def _mlp(x1, h2, x1_s, h2_s, w_up, w_down, layer, tm, tf):
    m, d = x1.shape
    ms = x1_s.shape[0]
    f = w_up.shape[2]
    rows = pl.BlockSpec((tm, d), lambda i, j: (i, 0))
    rows_s = pl.BlockSpec((ms, d), lambda i, j: (0, 0))
    return pl.pallas_call(
        _mlp_kernel,
        grid=(m // tm, f // tf),
        in_specs=[rows, rows, rows_s, rows_s, pl.BlockSpec((None, d, tf), lambda i, j: (layer, 0, j)),
                  pl.BlockSpec((None, tf, d), lambda i, j: (layer, j, 0))],
        out_specs=[rows, rows_s],
        out_shape=[jax.ShapeDtypeStruct((m, d), F32), jax.ShapeDtypeStruct((ms, d), F32)],
        compiler_params=_cparams("arbitrary", "arbitrary"),
        name="mlp",
    )(x1, h2, x1_s, h2_s, w_up, w_down)


def _row_tile(m, pref):
    return pref if m % pref == 0 else m


def _pad_lanes(v):
    return jnp.pad(v.astype(F32), (0, LANES - v.shape[0])).reshape(1, LANES)


def _prep_layer(l, shared, norm1_g, sc_conv_w, ssm_conv_w, ssm_conv_b, ssm_dt_bias, ssm_A_log, ssm_D,
                ssm_norm_g, q_norm_g, k_norm_g, norm2_g):
    wt_in = shared["wt_in"]
    d = wt_in.shape[2]
    heads = d // SSM_HEAD_DIM
    aw = ATT_HPG * len(ATT_GROUPS) * ATT_HEAD_DIM
    n_main = N_BRANCH * d + 3 * d + d + (d + 2 * SSM_GROUPS * SSM_STATE)
    assert n_main % LANES == 0 and heads <= LANES
    n_att_heads = aw // ATT_HEAD_DIM
    gains = jnp.concatenate([jnp.tile(q_norm_g[l], n_att_heads), jnp.tile(k_norm_g[l], n_att_heads),
                             jnp.ones((aw,), F32)]).reshape(1, 3 * aw)
    return dict(
        shared, layer=l, n_main=n_main, qkv_row=n_main + heads, n_qkv=3 * aw,
        norm1_g=norm1_g[l], norm2_g=norm2_g[l], qk_gains=gains,
        sc_conv_w=sc_conv_w[l], ssm_conv_w=ssm_conv_w[l], ssm_conv_b=ssm_conv_b[l].reshape(1, -1),
        dt_bias=_pad_lanes(ssm_dt_bias[l]), a_log=_pad_lanes(ssm_A_log[l]),
        d_exp=jnp.repeat(ssm_D[l].astype(F32), SSM_HEAD_DIM).reshape(1, d),
        ssm_norm_g=ssm_norm_g[l].reshape(1, d),
    )


def _in_proj(x2d, x2d_s, w):
    m = x2d.shape[0]
    h = _rmsnorm(x2d, w["norm1_g"], _row_tile(m, NORM_ROWS))
    h_s = _rmsnorm(x2d_s, w["norm1_g"], x2d_s.shape[0])
    p = _matmul_wcast(h, h_s, w["wt_in"], w["layer"], 0, w["n_main"] + MXU_COLS, MXU_COLS, name="in_proj_main")
    qkv = _matmul_qkv(h, h_s, w["wt_in"], w["layer"], w["qkv_row"], w["n_qkv"], w["qk_gains"], MXU_COLS)
    return (p[0], qkv[0]), (p[1], qkv[1])


def _tail(x2d, p, ys, x2d_s, p_s, ys_s, w, w_bf):
    m = x2d.shape[0]
    w_br_sc, w_br_ssm, w_br_att, w_out, w_up, w_down = w_bf
    merged, merged_s = _branch_merge(ys, ys_s, w_br_sc, w_br_ssm, w_br_att, 0, p, p_s,
                                     _row_tile(m, BRANCH_TILE[0]), BRANCH_TILE[1])
    x1, h2, x1_s, h2_s = _outproj(x2d, merged, x2d_s, merged_s, w_out, 0, w["norm2_g"], _row_tile(m, NORM_ROWS))
    return _mlp(x1, h2, x1_s, h2_s, w_up, w_down, 0, _row_tile(m, MLP_TILE[0]), MLP_TILE[1])


def _kv_rows(qkv3, gi, keep):
    aw = qkv3.shape[2] // 3
    gw = ATT_HPG * ATT_HEAD_DIM
    bsz, ln, _ = qkv3.shape
    k = qkv3[:, ln - keep:, aw + gi * gw:aw + (gi + 1) * gw].reshape(bsz, keep, ATT_HPG, ATT_HEAD_DIM)
    v = qkv3[:, ln - keep:, 2 * aw + gi * gw:2 * aw + (gi + 1) * gw].reshape(bsz, keep, ATT_HPG, ATT_HEAD_DIM)
    return jnp.stack([k, v], axis=2)


def _prompt_mixers(proj, w, bias_p, bsz, s, d):
    p, qkv = proj
    p3 = p.reshape(bsz, s, -1)
    qkv3 = qkv.reshape(bsz, s, -1)
    ysc, sc_new = _sc_prompt(p3, w["sc_conv_w"], d, *SC_TILE)
    yssm, ht, w_mlp = _ssd_prompt(p3, w["n_main"] // LANES, w["ssm_conv_w"], w["ssm_conv_b"], w["dt_bias"],
                                  w["a_log"], w["d_exp"], w["ssm_norm_g"], d, w["w_up"], w["w_down"], w["layer"])
    xbc_off = N_BRANCH * d + 3 * d + d
    ssm_conv_new = p3[:, s - (SSM_CONV - 1):, xbc_off:w["n_main"]]
    kv_new = [_kv_rows(qkv3, gi, min(window, s)) for gi, (window, _) in enumerate(ATT_GROUPS)]
    yatt, w_bf = _attn_prompt(qkv, bias_p, bsz, s, w["layer"],
                              [(w["w_br_sc"], 2), (w["w_br_ssm"], 2), (w["w_br_att"], 1), (w["w_out"], 2)])
    heads = d // SSM_HEAD_DIM
    st = (sc_new, ssm_conv_new, ht.reshape(bsz, heads, SSM_HEAD_DIM, SSM_STATE), kv_new[0], kv_new[1], kv_new[2])
    return (ysc.reshape(bsz * s, d), yssm.reshape(bsz * s, d), yatt), st, tuple(w_bf) + w_mlp


def _sample_mixers(proj, w, bias_d, bias0, sc_buf, ssm_buf, ssm_h0_all, ssm_ht_all, kv_caches, d):
    p, qkv = proj
    db = p.shape[0]
    l = w["layer"]
    ysc, u_new = _sc_sample(p, sc_buf.reshape(db, -1), w["sc_conv_w"], d, SC_SAMPLE_COLS)
    sc_new = jnp.concatenate([sc_buf[:, 1:], u_new[:, None, :]], axis=1)
    heads = d // SSM_HEAD_DIM
    depth = ssm_h0_all.shape[0]
    yssm, ssm_ht_all = _ssd_sample(p, w["n_main"] // LANES, ssm_buf.reshape(db, -1),
                                   ssm_h0_all.reshape(depth, db, heads * SSM_HEAD_DIM, SSM_STATE), l, ssm_ht_all,
                                   w["ssm_conv_w"], w["ssm_conv_b"], w["dt_bias"], w["a_log"], w["d_exp"],
                                   w["ssm_norm_g"], d)
    xbc_off = N_BRANCH * d + 3 * d + d
    ssm_conv_new = jnp.concatenate([ssm_buf[:, 1:], p[:, None, xbc_off:w["n_main"]]], axis=1)
    yatt = _attn_sample(qkv, kv_caches, l, bias_d, bias0)
    qkv3 = qkv.reshape(db, 1, -1)
    kv_new = [_kv_rows(qkv3, gi, 1) for gi in range(len(ATT_GROUPS))]
    st = (sc_new, ssm_conv_new, kv_new[0], kv_new[1], kv_new[2])
    return (ysc.astype(BF16), yssm.astype(BF16), yatt.astype(BF16)), st, ssm_ht_all


def kernel(x_prompt, x_sample, state_sc_conv, state_ssm_conv, state_ssm, cache_kv_w128, cache_kv_w512,
           cache_kv_w2048, norm1_g, w_in, sc_conv_w, ssm_conv_w, ssm_conv_b, ssm_dt_bias, ssm_A_log, ssm_D,
           ssm_norm_g, q_norm_g, k_norm_g, rel_bias, w_br_sc, w_br_ssm, w_br_att, w_out, norm2_g, w_up, w_down):
    depth = w_in.shape[0]
    ng = len(ATT_GROUPS)
    bias_p, bias_d = _build_bias(rel_bias)
    bias_p = bias_p.reshape((ng, ATT_HPG) + bias_p.shape[1:])
    bias_d = bias_d.reshape((ng, ATT_HPG) + bias_d.shape[1:]).transpose(0, 2, 1, 3)
    bias_d = jnp.concatenate([bias_d, jnp.zeros_like(bias_d)], axis=2)
    bias0 = jnp.broadcast_to(rel_bias[0].reshape(ng, ATT_HPG, 1), (ng, ATT_HPG, LANES))
    bsz, s, d = x_prompt.shape
    db, t, _ = x_sample.shape
    assert t == 1, "decode kernels handle one new token per sample"
    xp, xs = x_prompt.reshape(bsz * s, d), x_sample.reshape(db, d)
    p_new, s_new, s_ssm = [], [], None
    shared = dict(wt_in=jnp.swapaxes(w_in, 1, 2), w_br_sc=w_br_sc, w_br_ssm=w_br_ssm, w_br_att=w_br_att,
                  w_out=w_out, w_up=w_up, w_down=w_down)
    for l in range(depth):
        w = _prep_layer(l, shared, norm1_g, sc_conv_w, ssm_conv_w, ssm_conv_b, ssm_dt_bias, ssm_A_log, ssm_D,
                        ssm_norm_g, q_norm_g, k_norm_g, norm2_g)
        proj, proj_s = _in_proj(xp, xs, w)
        ys, st, w_bf = _prompt_mixers(proj, w, bias_p, bsz, s, d)
        p_new.append(st)
        ys_s, st, s_ssm = _sample_mixers(proj_s, w, bias_d, bias0, state_sc_conv[l], state_ssm_conv[l], state_ssm,
                                         s_ssm, (cache_kv_w128, cache_kv_w512, cache_kv_w2048), d)
        s_new.append(st)
        xp, xs = _tail(xp, proj[0], ys, xs, proj_s[0], ys_s, w, w_bf)
    p_out = [jnp.stack(a) for a in zip(*p_new)]
    s_out = [jnp.stack(a) for a in zip(*s_new)]
    s_out.insert(2, s_ssm.reshape(state_ssm.shape))
    return (xp.reshape(bsz, s, d), xs.reshape(db, 1, d), *p_out, *s_out)
```
